```python
import jax, jax.numpy as jnp
from jax import lax
import numpy as np

D_MODEL = 1024
BATCH = 32
SEQ = 256
DEPTH = 4
DEC_BATCH = 2
DEC_SEQ = 2048
PAST_LEN = 512

GRID_W = 64
HEAD_DIM = 64
RW_HEADS = 4
RW_WIDTH = RW_HEADS * HEAD_DIM
DECAY_LORA = 64
ICLR_LORA = 64
GATE_LORA = 128
NA_HEADS = 4
NA_WIDTH = NA_HEADS * HEAD_DIM
NA_WIN_ROWS = 8
NA_WIN_COLS = 16
GQA_Q_HEADS = 8
GQA_KV_HEADS = 2
RW_PROJ = 3 * RW_WIDTH + DECAY_LORA + ICLR_LORA + GATE_LORA
NA_PROJ = 3 * NA_WIDTH
GQA_PROJ = (GQA_Q_HEADS + 2 * GQA_KV_HEADS) * HEAD_DIM
IN_PROJ = RW_PROJ + NA_PROJ + GQA_PROJ
MIX_WIDTH = RW_WIDTH + NA_WIDTH + GQA_Q_HEADS * HEAD_DIM
RW_SPLITS = [RW_WIDTH, 2 * RW_WIDTH, 3 * RW_WIDTH, 3 * RW_WIDTH + DECAY_LORA, 3 * RW_WIDTH + DECAY_LORA + ICLR_LORA]
SHIFT_TAPS = 3
N_EXPERTS = 32
TOP_K = 4
D_FF_EXPERT = D_MODEL
SWIGLU_LIMIT = 7.0
SWIGLU_ALPHA = 1.702
ROPE_THETA = 10000.0
Q_BLOCK = 128
MOE_BLOCK = 128
NORM_EPS = 1e-6
LNX_EPS = 64e-5
N_MOD = 6

kernel_name = 'hybrid_rwkv7_natten_gqa_moe_flow_step'


def rms_norm(x, g):
    xf = x.astype(jnp.float32)
    y = xf * lax.rsqrt(jnp.mean(xf * xf, axis=-1, keepdims=True) + NORM_EPS)
    return (y * g.astype(jnp.float32)).astype(x.dtype)


def centred_shift(z, w):
    zp = jnp.pad(z, ((0, 0), (1, 1), (0, 0)))
    return zp[:, :-2] * w[0] + zp[:, 1:-1] * w[1] + zp[:, 2:] * w[2]


def wkv7_scan(r, decay, k, v, a, b, s0, reverse):
    def step(state, inp):
        r_t, w_t, k_t, v_t, a_t, b_t = inp
        sa = jnp.einsum('bhij,bhj->bhi', state, a_t)
        state = state * w_t[:, :, None, :] + sa[..., None] * b_t[:, :, None, :] + v_t[..., None] * k_t[:, :, None, :]
        return state, jnp.einsum('bhij,bhj->bhi', state, r_t)
    xs = tuple(jnp.moveaxis(t.astype(jnp.float32), 1, 0) for t in (r, decay, k, v, a, b))
    s_fin, ys = lax.scan(step, s0.astype(jnp.float32), xs, reverse=reverse)
    return s_fin, jnp.moveaxis(ys, 0, 1)


def rwkv7_mixer(z, lp, s0):
    B, L, _ = z.shape
    z = centred_shift(z, lp['rw_shift'])
    r, k, v, wd, ad, gd = jnp.split(z, RW_SPLITS, axis=-1)
    hs = lambda t: t.reshape(B, L, RW_HEADS, HEAD_DIM)
    kk = hs(k * lp['rw_k_k']).astype(jnp.float32)
    kk = kk / jnp.maximum(jnp.sqrt(jnp.sum(kk * kk, axis=-1, keepdims=True)), 1e-12)
    gate = jax.nn.sigmoid(gd) @ lp['rw_g_up']
    tw = jnp.tanh(wd)
    r_h, v_h = hs(r), hs(v)
    o_sum = 0.0
    bonus = 0.0
    finals = []
    for d in range(2):
        w_log = -jax.nn.softplus(-(lp['rw_w0'][d] + tw @ lp['rw_w_up'][d])) - 0.5
        decay = jnp.exp(-jnp.exp(w_log.astype(jnp.float32)))
        a = jax.nn.sigmoid(lp['rw_a0'][d] + ad @ lp['rw_a_up'][d])
        k_h = hs(k * (1 + (a - 1) * lp['rw_k_a']))
        s_fin, o_d = wkv7_scan(r_h, hs(decay), k_h, v_h, -kk, kk * hs(a), s0[:, d], d == 1)
        o_sum = o_sum + o_d
        bonus = bonus + jnp.sum(r_h * k_h * lp['rw_r_k'], axis=-1, keepdims=True) * v_h
        finals.append(s_fin)
    mu = jnp.mean(o_sum, axis=-1, keepdims=True)
    var = jnp.mean(jnp.square(o_sum - mu), axis=-1, keepdims=True)
    o = ((o_sum - mu) * lax.rsqrt(var + LNX_EPS)).reshape(B, L, RW_WIDTH)
    o = (o * lp['rw_ln_g'] + lp['rw_ln_b']).astype(z.dtype)
    out = (o + bonus.reshape(B, L, RW_WIDTH).astype(z.dtype)) * gate
    return out, jnp.stack(finals, axis=1)


def block_attention(q, k, v):
    B, Lq, Hq, hd = q.shape
    Hkv = k.shape[2]
    G = Hq // Hkv
    nb = Lq // Q_BLOCK
    qb = jnp.moveaxis(q.reshape(B, nb, Q_BLOCK, Hkv, G, hd), 1, 0)
    scale = hd ** -0.5

    def one(qblk):
        s = jnp.einsum('bqhgd,bkhd->bhgqk', qblk, k, preferred_element_type=jnp.float32) * scale
        p = jax.nn.softmax(s, axis=-1)
        return jnp.einsum('bhgqk,bkhd->bqhgd', p.astype(v.dtype), v)

    out = lax.map(one, qb)
    return jnp.moveaxis(out, 0, 1).reshape(B, Lq, Hq * hd)


def neighbourhood_attention(q, k, v, k_ctx, v_ctx, rpb):
    B, L, H, hd = q.shape
    rows = L // GRID_W
    kh = min(NA_WIN_ROWS, rows)
    kw = NA_WIN_COLS
    col_start = np.clip(np.arange(GRID_W) - kw // 2, 0, GRID_W - kw)
    col_idx = col_start[:, None] + np.arange(kw)
    col_off = col_idx - np.arange(GRID_W)[:, None] + NA_WIN_COLS - 1
    qg = q.reshape(B, rows, GRID_W, H, hd)
    kg = k.reshape(B, rows, GRID_W, H, hd)
    vg = v.reshape(B, rows, GRID_W, H, hd)
    scale = hd ** -0.5
    n_loc = kh * kw

    def one_row(args):
        q_row, i = args
        start = jnp.clip(i - kh // 2, 0, rows - kh)
        kb = lax.dynamic_slice_in_dim(kg, start, kh, axis=1)[:, :, col_idx]
        vb = lax.dynamic_slice_in_dim(vg, start, kh, axis=1)[:, :, col_idx]
        row_off = start + jnp.arange(kh) - i + NA_WIN_ROWS - 1
        bias = rpb[:, row_off[:, None, None], col_off[None, :, :]]
        bias = jnp.transpose(bias, (0, 2, 1, 3)).reshape(H, GRID_W, n_loc).astype(jnp.float32)
        s_loc = jnp.einsum('bwhd,brwchd->bhwrc', q_row, kb, preferred_element_type=jnp.float32)
        s_loc = s_loc.reshape(B, H, GRID_W, n_loc) * scale + bias
        s_ctx = jnp.einsum('bwhd,bkhd->bhwk', q_row, k_ctx, preferred_element_type=jnp.float32) * scale
        p = jax.nn.softmax(jnp.concatenate([s_loc, s_ctx], axis=-1), axis=-1).astype(v.dtype)
        p_loc = p[..., :n_loc].reshape(B, H, GRID_W, kh, kw)
        return (jnp.einsum('bhwrc,brwchd->bwhd', p_loc, vb)
                + jnp.einsum('bhwk,bkhd->bwhd', p[..., n_loc:], v_ctx))

    out = lax.map(one_row, (jnp.moveaxis(qg, 1, 0), jnp.arange(rows)))
    return jnp.moveaxis(out, 0, 1).reshape(B, L, H * hd)


def axial_rope(x):
    L = x.shape[1]
    t = jnp.arange(L)
    row = (t // GRID_W).astype(jnp.float32)
    col = (t % GRID_W).astype(jnp.float32)
    axis_dim = HEAD_DIM // 2
    inv = ROPE_THETA ** (-jnp.arange(0, axis_dim, 2, dtype=jnp.float32) / axis_dim)

    def rot(xa, pos):
        ang = pos[:, None] * inv[None, :]
        cos = jnp.cos(ang)[None, :, None, :]
        sin = jnp.sin(ang)[None, :, None, :]
        x1, x2 = xa[..., :axis_dim // 2], xa[..., axis_dim // 2:]
        return jnp.concatenate([x1 * cos - x2 * sin, x1 * sin + x2 * cos], axis=-1)

    xf = x.astype(jnp.float32)
    return jnp.concatenate([rot(xf[..., :axis_dim], row), rot(xf[..., axis_dim:], col)], axis=-1).astype(x.dtype)


def moe_ffn(h, router_w, router_b, w_gu, b_gu, w_down, b_down):
    n_tok, d = h.shape
    logits = (h @ router_w + router_b).astype(jnp.float32)
    top_val, top_idx = lax.top_k(logits, TOP_K)
    gates = jax.nn.softmax(top_val, axis=-1)
    n_assign = n_tok * TOP_K
    flat_e = top_idx.reshape(-1)
    flat_tok = jnp.arange(n_assign, dtype=jnp.int32) // TOP_K
    order = jnp.argsort(flat_e)
    sorted_e = flat_e[order]
    sorted_tok = flat_tok[order]
    sorted_gate = gates.reshape(-1)[order]
    counts = jnp.bincount(flat_e, length=N_EXPERTS)
    padded = (counts + MOE_BLOCK - 1) // MOE_BLOCK * MOE_BLOCK
    pad_end = jnp.cumsum(padded)
    pad_start = pad_end - padded
    start = jnp.cumsum(counts) - counts
    dest = pad_start[sorted_e] + jnp.arange(n_assign, dtype=jnp.int32) - start[sorted_e]
    n_blocks = -(-n_assign // MOE_BLOCK) + N_EXPERTS
    slot_tok = jnp.zeros((n_blocks * MOE_BLOCK,), jnp.int32).at[dest].set(sorted_tok)
    block_e = jnp.minimum(jnp.searchsorted(pad_end, jnp.arange(n_blocks) * MOE_BLOCK, side='right'), N_EXPERTS - 1)
    xb = h[slot_tok].reshape(n_blocks, MOE_BLOCK, d)

    def expert_block(args):
        xblk, e = args
        gu = xblk @ w_gu[e] + b_gu[e]
        glu = jnp.minimum(gu[:, :D_FF_EXPERT], SWIGLU_LIMIT)
        lin = jnp.clip(gu[:, D_FF_EXPERT:], -SWIGLU_LIMIT, SWIGLU_LIMIT)
        act = glu * jax.nn.sigmoid(SWIGLU_ALPHA * glu) * (lin + 1)
        return act @ w_down[e] + b_down[e]

    yb = lax.map(expert_block, (xb, block_e)).reshape(-1, d)
    contrib = yb[dest] * sorted_gate[:, None].astype(yb.dtype)
    return jnp.zeros_like(h).at[sorted_tok].add(contrib)


def trunk_layer(x, cvec, lp, ctx):
    B, L, _ = x.shape
    mods = jnp.split(jax.nn.silu(cvec) @ lp['w_mod'] + lp['b_mod'], N_MOD, axis=-1)
    shift1, scale1, gate1, shift2, scale2, gate2 = [m[:, None, :] for m in mods]
    h = rms_norm(x, lp['norm1_g']) * (1 + scale1) + shift1
    z = h @ lp['w_in']
    z_rw, z_na, z_gqa = jnp.split(z, [RW_PROJ, RW_PROJ + NA_PROJ], axis=-1)
    heads = lambda t, n: t.reshape(B, L, n, HEAD_DIM)
    na_q, na_k, na_v = [heads(t, NA_HEADS) for t in jnp.split(z_na, 3, axis=-1)]
    g_q, g_k, g_v = jnp.split(z_gqa, [GQA_Q_HEADS * HEAD_DIM, (GQA_Q_HEADS + GQA_KV_HEADS) * HEAD_DIM], axis=-1)
    g_q = rms_norm(heads(g_q, GQA_Q_HEADS), lp['q_norm'])
    g_k = rms_norm(heads(g_k, GQA_KV_HEADS), lp['k_norm'])
    g_v = heads(g_v, GQA_KV_HEADS)
    if ctx is None:
        s0 = jnp.zeros((B, 2, RW_HEADS, HEAD_DIM, HEAD_DIM), jnp.float32)
        rw_out, rw_state = rwkv7_mixer(z_rw, lp, s0)
        na_out = block_attention(na_q, na_k, na_v)
        gqa_out = block_attention(g_q, g_k, g_v)
        new_ctx = (na_k, na_v, g_k, g_v, rw_state)
    else:
        c_na_k, c_na_v, c_g_k, c_g_v, c_state = ctx
        rw_out, _ = rwkv7_mixer(z_rw, lp, c_state)
        na_out = neighbourhood_attention(na_q, na_k, na_v, c_na_k, c_na_v, lp['na_rpb'])
        gqa_out = block_attention(axial_rope(g_q),
                                  jnp.concatenate([c_g_k, axial_rope(g_k)], axis=1),
                                  jnp.concatenate([c_g_v, g_v], axis=1))
        new_ctx = None
    mix = jnp.concatenate([rw_out, na_out, gqa_out], axis=-1) @ lp['w_out']
    x = x + gate1 * mix
    h2 = rms_norm(x, lp['norm2_g']) * (1 + scale2) + shift2
    ff = moe_ffn(h2.reshape(B * L, -1), lp['router_w'], lp['router_b'], lp['moe_w_gu'],
                 lp['moe_b_gu'], lp['moe_w_down'], lp['moe_b_down']).reshape(B, L, -1)
    x = x + gate2 * ff
    return x, new_ctx


def setup_inputs(seed: int = 0) -> dict:
    key = jax.random.key(seed)
    ks = iter(jax.random.split(key, 48))
    nrm = lambda shape, s: s * jax.random.normal(next(ks), shape, jnp.float32)
    D = D_MODEL
    return {
        'x_prompt': nrm((BATCH, SEQ, D), 1.0),
        'x_sample': nrm((DEC_BATCH, DEC_SEQ, D), 1.0),
        'cache_na_k': nrm((DEC_BATCH, DEPTH, PAST_LEN, NA_HEADS, HEAD_DIM), 1.0),
        'cache_na_v': nrm((DEC_BATCH, DEPTH, PAST_LEN, NA_HEADS, HEAD_DIM), 1.0),
        'cache_gqa_k': nrm((DEC_BATCH, DEPTH, PAST_LEN, GQA_KV_HEADS, HEAD_DIM), 1.0),
        'cache_gqa_v': nrm((DEC_BATCH, DEPTH, PAST_LEN, GQA_KV_HEADS, HEAD_DIM), 1.0),
        'state_rwkv': nrm((DEC_BATCH, DEPTH, 2, RW_HEADS, HEAD_DIM, HEAD_DIM), 0.5),
        'c': nrm((DEC_BATCH, D), 1.0),
        'c_ctx': nrm((D,), 1.0),
        'w_mod': nrm((DEPTH, D, N_MOD * D), 0.3 * D ** -0.5),
        'b_mod': nrm((DEPTH, N_MOD * D), 0.02),
        'norm1_g': 1.0 + nrm((DEPTH, D), 0.1),
        'norm2_g': 1.0 + nrm((DEPTH, D), 0.1),
        'w_in': nrm((DEPTH, D, IN_PROJ), D ** -0.5),
        'rw_shift': jnp.array([0.25, 0.5, 0.25], jnp.float32)[None, :, None] + nrm((DEPTH, SHIFT_TAPS, RW_PROJ), 0.05),
        'rw_w0': -1.0 + nrm((DEPTH, 2, RW_WIDTH), 0.5),
        'rw_w_up': nrm((DEPTH, 2, DECAY_LORA, RW_WIDTH), 0.05),
        'rw_a0': nrm((DEPTH, 2, RW_WIDTH), 0.5),
        'rw_a_up': nrm((DEPTH, 2, ICLR_LORA, RW_WIDTH), 0.5 * ICLR_LORA ** -0.5),
        'rw_g_up': nrm((DEPTH, GATE_LORA, RW_WIDTH), GATE_LORA ** -0.5),
        'rw_k_k': 1.0 + nrm((DEPTH, RW_WIDTH), 0.1),
        'rw_k_a': 1.0 + nrm((DEPTH, RW_WIDTH), 0.1),
        'rw_r_k': nrm((DEPTH, RW_HEADS, HEAD_DIM), 0.1),
        'rw_ln_g': 1.0 + nrm((DEPTH, RW_WIDTH), 0.1),
        'rw_ln_b': nrm((DEPTH, RW_WIDTH), 0.01),
        'na_rpb': nrm((DEPTH, NA_HEADS, 2 * NA_WIN_ROWS - 1, 2 * NA_WIN_COLS - 1), 0.1),
        'q_norm': 1.0 + nrm((DEPTH, HEAD_DIM), 0.1),
        'k_norm': 1.0 + nrm((DEPTH, HEAD_DIM), 0.1),
        'w_out': nrm((DEPTH, MIX_WIDTH, D), MIX_WIDTH ** -0.5),
        'router_w': nrm((DEPTH, D, N_EXPERTS), D ** -0.5),
        'router_b': nrm((DEPTH, N_EXPERTS), 0.01),
        'moe_w_gu': nrm((DEPTH, N_EXPERTS, D, 2 * D_FF_EXPERT), D ** -0.5),
        'moe_b_gu': nrm((DEPTH, N_EXPERTS, 2 * D_FF_EXPERT), 0.01),
        'moe_w_down': nrm((DEPTH, N_EXPERTS, D_FF_EXPERT, D), D_FF_EXPERT ** -0.5),
        'moe_b_down': nrm((DEPTH, N_EXPERTS, D), 0.01),
        'final_norm_g': 1.0 + nrm((D,), 0.1),
    }


def reference(x_prompt, x_sample, cache_na_k, cache_na_v, cache_gqa_k, cache_gqa_v, state_rwkv, c, c_ctx,
              w_mod, b_mod, norm1_g, norm2_g, w_in, rw_shift, rw_w0, rw_w_up, rw_a0, rw_a_up, rw_g_up,
              rw_k_k, rw_k_a, rw_r_k, rw_ln_g, rw_ln_b, na_rpb, q_norm, k_norm, w_out, router_w, router_b,
              moe_w_gu, moe_b_gu, moe_w_down, moe_b_down, final_norm_g):
    y_p = x_prompt
    y_s = x_sample
    cvec_ctx = c_ctx[None, :]
    na_k_l, na_v_l, g_k_l, g_v_l, st_l = [], [], [], [], []
    for l in range(DEPTH):
        lp = {
            'w_mod': w_mod[l], 'b_mod': b_mod[l], 'norm1_g': norm1_g[l], 'norm2_g': norm2_g[l],
            'w_in': w_in[l], 'rw_shift': rw_shift[l], 'rw_w0': rw_w0[l], 'rw_w_up': rw_w_up[l],
            'rw_a0': rw_a0[l], 'rw_a_up': rw_a_up[l], 'rw_g_up': rw_g_up[l], 'rw_k_k': rw_k_k[l],
            'rw_k_a': rw_k_a[l], 'rw_r_k': rw_r_k[l], 'rw_ln_g': rw_ln_g[l], 'rw_ln_b': rw_ln_b[l],
            'na_rpb': na_rpb[l], 'q_norm': q_norm[l], 'k_norm': k_norm[l], 'w_out': w_out[l],
            'router_w': router_w[l], 'router_b': router_b[l], 'moe_w_gu': moe_w_gu[l],
            'moe_b_gu': moe_b_gu[l], 'moe_w_down': moe_w_down[l], 'moe_b_down': moe_b_down[l],
        }
        y_p, (nk, nv, gk, gv, st) = trunk_layer(y_p, cvec_ctx, lp, None)
        na_k_l.append(nk)
        na_v_l.append(nv)
        g_k_l.append(gk)
        g_v_l.append(gv)
        st_l.append(st)
        ctx_l = (cache_na_k[:, l], cache_na_v[:, l], cache_gqa_k[:, l], cache_gqa_v[:, l], state_rwkv[:, l])
        y_s, _ = trunk_layer(y_s, c, lp, ctx_l)
    y_prompt = rms_norm(y_p, final_norm_g)
    y_sample = rms_norm(y_s, final_norm_g)
    new_na_k = jnp.stack(na_k_l, axis=1)
    new_na_v = jnp.stack(na_v_l, axis=1)
    new_gqa_k = jnp.stack(g_k_l, axis=1)
    new_gqa_v = jnp.stack(g_v_l, axis=1)
    new_state_rwkv = jnp.stack(st_l, axis=1)
    return (y_prompt, y_sample, new_na_k, new_na_v, new_gqa_k, new_gqa_v, new_state_rwkv)
```

```python
import functools

import numpy as np
import jax
import jax.numpy as jnp
from jax import lax
from jax.experimental import pallas as pl
from jax.experimental.pallas import tpu as pltpu

F32 = jnp.float32
BF16 = jnp.bfloat16

D_MODEL = 1024
BATCH = 32
SEQ = 256
DEPTH = 4
DEC_BATCH = 2
DEC_SEQ = 2048
PAST_LEN = 512
GRID_W = 64
GRID_H = DEC_SEQ // GRID_W
HEAD_DIM = 64
RW_HEADS = 4
RW_WIDTH = RW_HEADS * HEAD_DIM
DECAY_LORA = 64
ICLR_LORA = 64
GATE_LORA = 128
NA_HEADS = 4
NA_WIDTH = NA_HEADS * HEAD_DIM
NA_WIN_ROWS = 8
NA_WIN_COLS = 16
GQA_Q_HEADS = 8
GQA_KV_HEADS = 2
GQA_GROUP = GQA_Q_HEADS // GQA_KV_HEADS
GQ_WIDTH = GQA_Q_HEADS * HEAD_DIM
GKV_WIDTH = GQA_KV_HEADS * HEAD_DIM
RW_PROJ = 3 * RW_WIDTH + DECAY_LORA + ICLR_LORA + GATE_LORA
NA_PROJ = 3 * NA_WIDTH
GQA_PROJ = GQ_WIDTH + 2 * GKV_WIDTH
IN_PROJ = RW_PROJ + NA_PROJ + GQA_PROJ
ATT_WIDTH = GQ_WIDTH + NA_WIDTH
N_EXPERTS = 32
TOP_K = 4
D_FF = D_MODEL
SWIGLU_LIMIT = 7.0
SWIGLU_ALPHA = 1.702
ROPE_THETA = 10000.0
NORM_EPS = 1e-6
LNX_EPS = 64e-5
N_MOD = 6
ATT_SCALE = HEAD_DIM ** -0.5
NEG_BIG = -1e30

T_CTX = BATCH * SEQ
T_LAT = DEC_BATCH * DEC_SEQ
T_ALL = T_CTX + T_LAT
N_GROUPS = 1 + DEC_BATCH

LANES = 128
SUBLANES = 8
TM_TOK = 512
CHUNK = 64
TQ_GQA = 256
TM_EXP = 256
TM_CMB = 256
N_SLOT_BLOCKS = T_ALL * TOP_K // TM_EXP + N_EXPERTS
N_SLOTS = N_SLOT_BLOCKS * TM_EXP
VMEM_LIMIT = 56 * 1024 * 1024

NT_DIMS = (((1,), (1,)), ((), ()))
TN_DIMS = (((0,), (0,)), ((), ()))


def _bdot(a, b, dims=None):
    a = a.astype(BF16)
    b = b.astype(BF16)
    if dims is None:
        return jnp.dot(a, b, preferred_element_type=F32)
    return lax.dot_general(a, b, dims, preferred_element_type=F32)


def _split(a):
    hi = a.astype(BF16)
    lo = (a - hi.astype(F32)).astype(BF16)
    return hi, lo


def _dot3(a, b, dims=None):
    ah, al = _split(a)
    bh, bl = _split(b)
    return _bdot(ah, bh, dims) + _bdot(ah, bl, dims) + _bdot(al, bh, dims)


def _dot_exact_lhs(a_exact, b):
    h1 = b.astype(BF16)
    r1 = b - h1.astype(F32)
    h2 = r1.astype(BF16)
    h3 = (r1 - h2.astype(F32)).astype(BF16)
    return _bdot(a_exact, h1) + _bdot(a_exact, h2) + _bdot(a_exact, h3)


def _head_ones(n):
    r = lax.broadcasted_iota(jnp.int32, (n, n), 0) // HEAD_DIM
    c = lax.broadcasted_iota(jnp.int32, (n, n), 1) // HEAD_DIM
    return (r == c).astype(BF16)


def _head_sum(x, ones_bd):
    hi, lo = _split(x)
    return (jnp.dot(hi, ones_bd, preferred_element_type=F32)
            + jnp.dot(lo, ones_bd, preferred_element_type=F32))


def _sigmoid(x):
    return 1.0 / (1.0 + jnp.exp(-x))


def _cparams(sem):
    return pltpu.CompilerParams(dimension_semantics=sem, vmem_limit_bytes=VMEM_LIMIT)


def _group_of_block(i, rows_per_block):
    first_lat = T_CTX // rows_per_block
    per_sample = DEC_SEQ // rows_per_block
    return jnp.where(i < first_lat, 0, 1 + (i - first_lat) // per_sample)


def _mods_kernel(c_ref, w_ref, b_ref, o_ref):
    c = c_ref[...]
    s = c * _sigmoid(c)
    o_ref[0] = _dot3(s, w_ref[0]) + b_ref[0]


def _mods_call(cvecs, w_mod, b_mod):
    tn = 1536
    n_rows = cvecs.shape[0]
    return pl.pallas_call(
        _mods_kernel,
        out_shape=jax.ShapeDtypeStruct((DEPTH, n_rows, N_MOD * D_MODEL), F32),
        grid=(DEPTH, N_MOD * D_MODEL // tn),
        in_specs=[
            pl.BlockSpec((n_rows, D_MODEL), lambda l, j: (0, 0)),
            pl.BlockSpec((1, D_MODEL, tn), lambda l, j: (l, 0, j)),
            pl.BlockSpec((1, 1, tn), lambda l, j: (l, 0, j)),
        ],
        out_specs=pl.BlockSpec((1, n_rows, tn), lambda l, j: (l, 0, j)),
        compiler_params=_cparams(("arbitrary", "arbitrary")),
        name="adaln_mods",
    )(cvecs, w_mod, b_mod.reshape(DEPTH, 1, N_MOD * D_MODEL))


def _rms(x, g):
    ms = jnp.mean(x * x, axis=-1, keepdims=True)
    return x * lax.rsqrt(ms + NORM_EPS) * g


def _in_proj_kernel(x_ref, mod_ref, g_ref, w_ref, qkg_ref, rc_ref, rs1_ref, rs2_ref,
                    zrw_ref, zna_ref, zgq_ref, zgkv_ref):
    x = x_ref[...]
    shift1 = mod_ref[0, 0:1, :]
    scale1 = mod_ref[0, 1:2, :]
    h = _rms(x, g_ref[...]) * (1.0 + scale1) + shift1
    z = jnp.dot(h.astype(BF16), w_ref[...], preferred_element_type=F32)
    zrw_ref[...] = z[:, :RW_PROJ]
    zna_ref[...] = z[:, RW_PROJ:RW_PROJ + NA_PROJ]
    qk_w = GQ_WIDTH + GKV_WIDTH
    qk = z[:, RW_PROJ + NA_PROJ:RW_PROJ + NA_PROJ + qk_w]
    ones_bd = _head_ones(LANES)
    sq = qk * qk
    ssq = jnp.concatenate(
        [_head_sum(sq[:, j * LANES:(j + 1) * LANES], ones_bd) for j in range(qk_w // LANES)], axis=1)
    qkn = qk * lax.rsqrt(ssq * (1.0 / HEAD_DIM) + NORM_EPS) * qkg_ref[...]
    reps = qk_w // LANES
    rc = jnp.concatenate([rc_ref[...]] * reps, axis=1)
    rs1 = jnp.concatenate([rs1_ref[...]] * reps, axis=1)
    rs2 = jnp.concatenate([rs2_ref[...]] * reps, axis=1)
    half = HEAD_DIM // 4
    qkr = qkn * rc + pltpu.roll(qkn, half, 1) * rs1 + pltpu.roll(qkn, qk_w - half, 1) * rs2
    zgq_ref[...] = qkr[:, :GQ_WIDTH]
    zgkv_ref[:, :GKV_WIDTH] = qkr[:, GQ_WIDTH:]
    zgkv_ref[:, GKV_WIDTH:] = z[:, RW_PROJ + NA_PROJ + qk_w:]


def _rope_tables():
    t = np.arange(DEC_SEQ)
    pos = np.stack([t // GRID_W, t % GRID_W], axis=1).astype(np.float32)
    axis_dim = HEAD_DIM // 2
    inv = ROPE_THETA ** (-np.arange(0, axis_dim, 2, dtype=np.float32) / axis_dim)
    d = np.arange(LANES) % HEAD_DIM
    part = d // axis_dim
    within = d % axis_dim
    freq = within % (axis_dim // 2)
    second = within // (axis_dim // 2)
    ang = jnp.asarray(pos)[:, part] * jnp.asarray(inv)[freq][None, :]
    cos = jnp.cos(ang)
    sin = jnp.sin(ang)
    s1 = jnp.where(second[None, :] == 1, sin, 0.0)
    s2 = jnp.where(second[None, :] == 0, -sin, 0.0)
    ident = jnp.ones((TM_TOK, LANES), F32)
    zero = jnp.zeros((TM_TOK, LANES), F32)
    return (jnp.concatenate([cos, ident], 0), jnp.concatenate([s1, zero], 0),
            jnp.concatenate([s2, zero], 0))


def _in_proj_call(x, mods_l, g1, w_in_bf, qk_g, rope):
    n_blk = T_ALL // TM_TOK
    lat_blk = DEC_SEQ // TM_TOK
    first_lat = T_CTX // TM_TOK

    def rope_idx(i):
        return (jnp.where(i < first_lat, lat_blk, (i - first_lat) % lat_blk), 0)

    row = lambda i: (i, 0)
    rope_spec = pl.BlockSpec((TM_TOK, LANES), rope_idx)
    return pl.pallas_call(
        _in_proj_kernel,
        out_shape=(jax.ShapeDtypeStruct((T_ALL, RW_PROJ), F32),
                   jax.ShapeDtypeStruct((T_ALL, NA_PROJ), F32),
                   jax.ShapeDtypeStruct((T_ALL, GQ_WIDTH), F32),
                   jax.ShapeDtypeStruct((T_ALL, 2 * GKV_WIDTH), F32)),
        grid=(n_blk,),
        in_specs=[
            pl.BlockSpec((TM_TOK, D_MODEL), row),
            pl.BlockSpec((1, N_MOD, D_MODEL), lambda i: (_group_of_block(i, TM_TOK), 0, 0)),
            pl.BlockSpec((1, D_MODEL), lambda i: (0, 0)),
            pl.BlockSpec((D_MODEL, IN_PROJ), lambda i: (0, 0)),
            pl.BlockSpec((1, GQ_WIDTH + GKV_WIDTH), lambda i: (0, 0)),
            rope_spec, rope_spec, rope_spec,
        ],
        out_specs=(pl.BlockSpec((TM_TOK, RW_PROJ), row), pl.BlockSpec((TM_TOK, NA_PROJ), row),
                   pl.BlockSpec((TM_TOK, GQ_WIDTH), row), pl.BlockSpec((TM_TOK, 2 * GKV_WIDTH), row)),
        compiler_params=_cparams(("arbitrary",)),
        name="in_proj",
    )(x, mods_l, g1, w_in_bf, qk_g, *rope)


def _softplus(x):
    return jnp.maximum(x, 0.0) + jnp.log(1.0 + jnp.exp(-jnp.abs(x)))


def _rw_pre(z, zprev, znext, shift_ref, kk_ref, ka_ref, rk_ref, w0_ref, wup_ref, a0_ref, aup_ref, d,
            ones_bd):
    rows = lax.broadcasted_iota(jnp.int32, z.shape, 0)
    zp = jnp.where(rows == 0, zprev, pltpu.roll(z, 1, 0))
    zn = jnp.where(rows == CHUNK - 1, znext, pltpu.roll(z, CHUNK - 1, 0))
    zs = zp * shift_ref[0:1, :] + z * shift_ref[1:2, :] + zn * shift_ref[2:3, :]
    r = zs[:, 0:RW_WIDTH]
    k = zs[:, RW_WIDTH:2 * RW_WIDTH]
    v = zs[:, 2 * RW_WIDTH:3 * RW_WIDTH]
    o = 3 * RW_WIDTH
    wd = zs[:, o:o + DECAY_LORA]
    ad = zs[:, o + DECAY_LORA:o + DECAY_LORA + ICLR_LORA]
    gd = zs[:, o + DECAY_LORA + ICLR_LORA:]
    kk = k * kk_ref[...]
    kk = kk / jnp.maximum(jnp.sqrt(_head_sum(kk * kk, ones_bd)), 1e-12)
    tw = jnp.tanh(wd)
    wl = w0_ref[d:d + 1, :] + _dot3(tw, wup_ref[d])
    lw = -jnp.exp(-_softplus(-wl) - 0.5)
    a_sig = _sigmoid(a0_ref[d:d + 1, :] + _dot3(ad, aup_ref[d]))
    k_d = k * (1.0 + (a_sig - 1.0) * ka_ref[...])
    bonus = _head_sum(r * k_d * rk_ref[...], ones_bd) * v
    return dict(r=r, k=k_d, v=v, a=-kk, b=kk * a_sig, lw=lw, bonus=bonus, gd=gd, ad=ad, k_raw=k)


def _wkv_chunk(r, lw, k, v, a, b, h0, rev):
    c = CHUNK
    t = lax.broadcasted_iota(jnp.int32, (c, c), 0)
    j = lax.broadcasted_iota(jnp.int32, (c, c), 1)
    incl = (j >= t) if rev else (j <= t)
    strict = (j > t) if rev else (j < t)
    cs = _dot_exact_lhs(incl.astype(BF16), lw)
    tot = cs[0:1, :] if rev else cs[c - 1:c, :]
    at = a * jnp.exp(cs - lw)
    rt = r * jnp.exp(cs)
    e_inv = jnp.exp(-cs)
    bt = b * e_inv
    kt = k * e_inv
    e_rem = jnp.exp(tot - cs)
    aa = _bdot(jnp.concatenate([at, rt], 0), jnp.concatenate([bt, kt], 0), NT_DIMS)
    a_ab = jnp.where(strict, aa[:c, :c], 0.0)
    a_ak = jnp.where(strict, aa[:c, c:], 0.0)
    a_rb = jnp.where(incl, aa[c:, :c], 0.0)
    a_rk = jnp.where(incl, aa[c:, c:], 0.0)
    w = _bdot(jnp.concatenate([a_ak, at], 1), jnp.concatenate([v, h0], 0))
    x = a_ab
    n_sq = int(np.log2(c))
    for s in range(n_sq):
        if s < n_sq - 1:
            p = _bdot(x, jnp.concatenate([w, x], 1))
            w = w + p[:, :HEAD_DIM]
            x = p[:, HEAD_DIM:]
        else:
            w = w + _bdot(x, w)
    y = _bdot(jnp.concatenate([a_rb, a_rk, rt], 1), jnp.concatenate([w, v, h0], 0))
    eye = lax.broadcasted_iota(jnp.int32, (HEAD_DIM, HEAD_DIM), 0) == \
        lax.broadcasted_iota(jnp.int32, (HEAD_DIM, HEAD_DIM), 1)
    g_col = jnp.sum(jnp.where(eye, jnp.exp(tot), 0.0), axis=1, keepdims=True)
    h_new = g_col * h0 + _bdot(jnp.concatenate([b * e_rem, k * e_rem], 0),
                               jnp.concatenate([w, v], 0), TN_DIMS)
    return y, h_new


def _rwkv_kernel(zf_ref, zfp_ref, zfn_ref, zb_ref, zbp_ref, zbn_ref, s0_ref,
                 shift_ref, kk_ref, ka_ref, rk_ref, w0_ref, wup_ref, a0_ref, aup_ref, gup_ref,
                 *refs, n_chunks, has_init, emit_state, n_alias):
    refs = refs[n_alias:]
    if emit_state:
        yf_ref, yb_ref, bonus_ref, gate_ref, st_ref, h_ref = refs
    else:
        yf_ref, yb_ref, bonus_ref, gate_ref, h_ref = refs
        st_ref = None
    i = pl.program_id(1)
    ones_bd = _head_ones(RW_WIDTH)

    @pl.when(i == 0)
    def _():
        if has_init:
            h_ref[...] = s0_ref[0]
        else:
            h_ref[...] = jnp.zeros(h_ref.shape, F32)

    params = (shift_ref, kk_ref, ka_ref, rk_ref, w0_ref, wup_ref, a0_ref, aup_ref)
    first = i == 0
    last = i == n_chunks - 1
    zero_row = jnp.zeros((1, RW_PROJ), F32)
    pf = _rw_pre(zf_ref[...], jnp.where(first, zero_row, zfp_ref[SUBLANES - 1:SUBLANES, :]),
                 jnp.where(last, zero_row, zfn_ref[0:1, :]), *params, 0, ones_bd)
    pb = _rw_pre(zb_ref[...], jnp.where(last, zero_row, zbp_ref[SUBLANES - 1:SUBLANES, :]),
                 jnp.where(first, zero_row, zbn_ref[0:1, :]), *params, 1, ones_bd)
    a_sig_b = _sigmoid(a0_ref[1:2, :] + _dot3(pf["ad"], aup_ref[1]))
    k_b = pf["k_raw"] * (1.0 + (a_sig_b - 1.0) * ka_ref[...])
    bonus_ref[...] = pf["bonus"] + _head_sum(pf["r"] * k_b * rk_ref[...], ones_bd) * pf["v"]
    gate_ref[...] = _bdot(_sigmoid(pf["gd"]), gup_ref[...])

    for d, p, y_ref in ((0, pf, yf_ref), (1, pb, yb_ref)):
        for h in range(RW_HEADS):
            sl = slice(h * HEAD_DIM, (h + 1) * HEAD_DIM)
            y, h_new = _wkv_chunk(p["r"][:, sl], p["lw"][:, sl], p["k"][:, sl], p["v"][:, sl],
                                  p["a"][:, sl], p["b"][:, sl], h_ref[d, h], d == 1)
            y_ref[:, sl] = y
            h_ref[d, h] = h_new

    if emit_state:
        @pl.when(last)
        def _():
            for d in range(2):
                for h in range(RW_HEADS):
                    st_ref[0, d, h] = h_ref[d, h].T


def _rwkv_call(zrw, s0_t, lp, prev_outs, *, n_seq, seq_len, row_base, has_init, emit_state):
    n_chunks = seq_len // CHUNK
    base_c = row_base // CHUNK
    per8 = CHUNK // SUBLANES
    last8 = T_ALL // SUBLANES - 1

    def fwd(b, i):
        return base_c + b * n_chunks + i

    def bwd(b, i):
        return base_c + b * n_chunks + (n_chunks - 1 - i)

    main = lambda cidx: pl.BlockSpec((CHUNK, RW_PROJ), lambda b, i: (cidx(b, i), 0))
    prev8 = lambda cidx: pl.BlockSpec(
        (SUBLANES, RW_PROJ), lambda b, i: (jnp.maximum(cidx(b, i) * per8 - 1, 0), 0))
    next8 = lambda cidx: pl.BlockSpec(
        (SUBLANES, RW_PROJ), lambda b, i: (jnp.minimum((cidx(b, i) + 1) * per8, last8), 0))
    full = lambda shape: pl.BlockSpec(shape, lambda b, i: (0,) * len(shape))
    out_f = pl.BlockSpec((CHUNK, RW_WIDTH), lambda b, i: (fwd(b, i), 0))
    out_b = pl.BlockSpec((CHUNK, RW_WIDTH), lambda b, i: (bwd(b, i), 0))
    tok = jax.ShapeDtypeStruct((T_ALL, RW_WIDTH), F32)
    out_shape = [tok, tok, tok, tok]
    out_specs = [out_f, out_b, out_f, out_f]
    if emit_state:
        out_shape.append(jax.ShapeDtypeStruct((n_seq, 2, RW_HEADS, HEAD_DIM, HEAD_DIM), F32))
        out_specs.append(pl.BlockSpec((1, 2, RW_HEADS, HEAD_DIM, HEAD_DIM), lambda b, i: (b, 0, 0, 0, 0)))
    n_alias = len(prev_outs)
    n_in = 16
    aliases = {n_in + j: j for j in range(n_alias)}
    state_spec = pl.BlockSpec((1, 2, RW_HEADS, HEAD_DIM, HEAD_DIM),
                              lambda b, i: (b if has_init else 0, 0, 0, 0, 0))
    kern = functools.partial(_rwkv_kernel, n_chunks=n_chunks, has_init=has_init,
                             emit_state=emit_state, n_alias=n_alias)
    return pl.pallas_call(
        kern,
        out_shape=tuple(out_shape),
        grid=(n_seq, n_chunks),
        in_specs=[main(fwd), prev8(fwd), next8(fwd), main(bwd), prev8(bwd), next8(bwd), state_spec,
                  full((3, RW_PROJ)), full((1, RW_WIDTH)), full((1, RW_WIDTH)), full((1, RW_WIDTH)),
                  full((2, RW_WIDTH)), full((2, DECAY_LORA, RW_WIDTH)),
                  full((2, RW_WIDTH)), full((2, ICLR_LORA, RW_WIDTH)), full((GATE_LORA, RW_WIDTH))]
                 + [pl.BlockSpec(memory_space=pl.ANY)] * n_alias,
        out_specs=tuple(out_specs),
        scratch_shapes=[pltpu.VMEM((2, RW_HEADS, HEAD_DIM, HEAD_DIM), F32)],
        input_output_aliases=aliases,
        compiler_params=_cparams(("arbitrary", "arbitrary")),
        name="rwkv_scan_init" if has_init else "rwkv_scan_zero",
    )(zrw, zrw, zrw, zrw, zrw, zrw, s0_t, lp["rw_shift"], lp["rw_k_k"], lp["rw_k_a"], lp["rw_r_k"],
      lp["rw_w0"], lp["rw_w_up"], lp["rw_a0"], lp["rw_a_up"], lp["rw_g_up"], *prev_outs)


def _softmax_pv(scores, values):
    m = scores[0].max(axis=-1, keepdims=True)
    for s in scores[1:]:
        m = jnp.maximum(m, s.max(axis=-1, keepdims=True))
    es = [jnp.exp(s - m) for s in scores]
    l = es[0].sum(axis=-1, keepdims=True)
    for e in es[1:]:
        l = l + e.sum(axis=-1, keepdims=True)
    inv = 1.0 / l
    o = _bdot(es[0] * inv, values[0])
    for e, v in zip(es[1:], values[1:]):
        o = o + _bdot(e * inv, v)
    return o


def _head(x, h):
    return x[:, h * HEAD_DIM:(h + 1) * HEAD_DIM]


def _ctx_attn_kernel(zna_ref, zgq_ref, zgkv_ref, att_ref, nk_ref, nv_ref, gk_ref, gv_ref):
    zna = zna_ref[...]
    q, k, v = zna[:, :NA_WIDTH], zna[:, NA_WIDTH:2 * NA_WIDTH], zna[:, 2 * NA_WIDTH:]
    nk_ref[0] = k
    nv_ref[0] = v
    gq = zgq_ref[...]
    gkv = zgkv_ref[...]
    gk, gv = gkv[:, :GKV_WIDTH], gkv[:, GKV_WIDTH:]
    gk_ref[0] = gk
    gv_ref[0] = gv
    for h in range(GQA_Q_HEADS):
        kv = h // GQA_GROUP
        s = _bdot(_head(gq, h), _head(gk, kv), NT_DIMS) * ATT_SCALE
        att_ref[:, h * HEAD_DIM:(h + 1) * HEAD_DIM] = _softmax_pv([s], [_head(gv, kv)]).astype(BF16)
    for h in range(NA_HEADS):
        s = _bdot(_head(q, h), _head(k, h), NT_DIMS) * ATT_SCALE
        o = _softmax_pv([s], [_head(v, h)])
        att_ref[:, GQ_WIDTH + h * HEAD_DIM:GQ_WIDTH + (h + 1) * HEAD_DIM] = o.astype(BF16)


def _ctx_attn_call(zna, zgq, zgkv):
    row = lambda b: (b, 0)
    bat = lambda b: (b, 0, 0)
    return pl.pallas_call(
        _ctx_attn_kernel,
        out_shape=(jax.ShapeDtypeStruct((T_ALL, ATT_WIDTH), BF16),
                   jax.ShapeDtypeStruct((BATCH, SEQ, NA_WIDTH), F32),
                   jax.ShapeDtypeStruct((BATCH, SEQ, NA_WIDTH), F32),
                   jax.ShapeDtypeStruct((BATCH, SEQ, GKV_WIDTH), F32),
                   jax.ShapeDtypeStruct((BATCH, SEQ, GKV_WIDTH), F32)),
        grid=(BATCH,),
        in_specs=[pl.BlockSpec((SEQ, NA_PROJ), row), pl.BlockSpec((SEQ, GQ_WIDTH), row),
                  pl.BlockSpec((SEQ, 2 * GKV_WIDTH), row)],
        out_specs=(pl.BlockSpec((SEQ, ATT_WIDTH), row),
                   pl.BlockSpec((1, SEQ, NA_WIDTH), bat), pl.BlockSpec((1, SEQ, NA_WIDTH), bat),
                   pl.BlockSpec((1, SEQ, GKV_WIDTH), bat), pl.BlockSpec((1, SEQ, GKV_WIDTH), bat)),
        compiler_params=_cparams(("arbitrary",)),
        name="ctx_attention",
    )(zna, zgq, zgkv)


def _lat_na_kernel(zna_ref, ck_ref, cv_ref, tb_ref, att_in_ref, att_ref):
    del att_in_ref
    i = pl.program_id(1)
    start = jnp.clip(i - NA_WIN_ROWS // 2, 0, GRID_H - NA_WIN_ROWS)
    n_loc = NA_WIN_ROWS * GRID_W
    q = zna_ref[pl.ds(pl.multiple_of(i * GRID_W, GRID_W), GRID_W), 0:NA_WIDTH]
    w0 = pl.multiple_of(start * GRID_W, GRID_W)
    kwin = zna_ref[pl.ds(w0, n_loc), NA_WIDTH:2 * NA_WIDTH]
    vwin = zna_ref[pl.ds(w0, n_loc), 2 * NA_WIDTH:3 * NA_WIDTH]
    kc = ck_ref[0]
    vc = cv_ref[0]
    dr0 = start - i + NA_WIN_ROWS - 1
    outs = []
    for h in range(NA_HEADS):
        bias = jnp.concatenate([tb_ref[h, dr0 + r] for r in range(NA_WIN_ROWS)], axis=1)
        qh = _head(q, h)
        s_loc = _bdot(qh, _head(kwin, h), NT_DIMS) * ATT_SCALE + bias
        s_ctx = _bdot(qh, _head(kc, h), NT_DIMS) * ATT_SCALE
        outs.append(_softmax_pv([s_loc, s_ctx], [_head(vwin, h), _head(vc, h)]))
    att_ref[...] = jnp.concatenate(outs, axis=1).astype(BF16)


def _na_bias_table(rpb):
    w = np.arange(GRID_W)[:, None]
    kc = np.arange(GRID_W)[None, :]
    cs = np.clip(w - NA_WIN_COLS // 2, 0, GRID_W - NA_WIN_COLS)
    valid = (kc >= cs) & (kc < cs + NA_WIN_COLS)
    off = np.clip(kc - w + NA_WIN_COLS - 1, 0, 2 * NA_WIN_COLS - 2)
    tb = rpb[:, :, off]
    return jnp.where(jnp.asarray(valid)[None, None], tb, NEG_BIG).astype(F32)


def _lat_na_call(zna, ck, cv, tb, att):
    lat_blk = T_CTX // DEC_SEQ
    first_row = T_CTX // GRID_W
    return pl.pallas_call(
        _lat_na_kernel,
        out_shape=jax.ShapeDtypeStruct((T_ALL, ATT_WIDTH), BF16),
        grid=(DEC_BATCH, GRID_H),
        in_specs=[pl.BlockSpec((DEC_SEQ, NA_PROJ), lambda b, i: (lat_blk + b, 0)),
                  pl.BlockSpec((1, PAST_LEN, NA_WIDTH), lambda b, i: (b, 0, 0)),
                  pl.BlockSpec((1, PAST_LEN, NA_WIDTH), lambda b, i: (b, 0, 0)),
                  pl.BlockSpec((NA_HEADS, 2 * NA_WIN_ROWS - 1, GRID_W, GRID_W), lambda b, i: (0, 0, 0, 0)),
                  pl.BlockSpec(memory_space=pl.ANY)],
        out_specs=pl.BlockSpec((GRID_W, NA_WIDTH),
                               lambda b, i: (first_row + b * GRID_H + i, GQ_WIDTH // NA_WIDTH)),
        input_output_aliases={4: 0},
        compiler_params=_cparams(("arbitrary", "arbitrary")),
        name="latent_neighbourhood_attention",
    )(zna, ck, cv, tb, att)


def _lat_gqa_kernel(zgq_ref, zgkv_ref, ck_ref, cv_ref, att_in_ref, att_ref):
    del att_in_ref
    q = zgq_ref[...]
    kv = zgkv_ref[...]
    kl, vl = kv[:, :GKV_WIDTH], kv[:, GKV_WIDTH:]
    kc = ck_ref[0]
    vc = cv_ref[0]
    for h in range(GQA_Q_HEADS):
        g = h // GQA_GROUP
        qh = _head(q, h)
        s_c = _bdot(qh, _head(kc, g), NT_DIMS) * ATT_SCALE
        s_l = _bdot(qh, _head(kl, g), NT_DIMS) * ATT_SCALE
        o = _softmax_pv([s_c, s_l], [_head(vc, g), _head(vl, g)])
        att_ref[:, h * HEAD_DIM:(h + 1) * HEAD_DIM] = o.astype(BF16)


def _lat_gqa_call(zgq, zgkv, ck, cv, att):
    n_q = DEC_SEQ // TQ_GQA
    first_q = T_CTX // TQ_GQA
    lat_blk = T_CTX // DEC_SEQ
    return pl.pallas_call(
        _lat_gqa_kernel,
        out_shape=jax.ShapeDtypeStruct((T_ALL, ATT_WIDTH), BF16),
        grid=(DEC_BATCH, n_q),
        in_specs=[pl.BlockSpec((TQ_GQA, GQ_WIDTH), lambda b, j: (first_q + b * n_q + j, 0)),
                  pl.BlockSpec((DEC_SEQ, 2 * GKV_WIDTH), lambda b, j: (lat_blk + b, 0)),
                  pl.BlockSpec((1, PAST_LEN, GKV_WIDTH), lambda b, j: (b, 0, 0)),
                  pl.BlockSpec((1, PAST_LEN, GKV_WIDTH), lambda b, j: (b, 0, 0)),
                  pl.BlockSpec(memory_space=pl.ANY)],
        out_specs=pl.BlockSpec((TQ_GQA, GQ_WIDTH), lambda b, j: (first_q + b * n_q + j, 0)),
        input_output_aliases={4: 0},
        compiler_params=_cparams(("arbitrary", "arbitrary")),
        name="latent_gqa_attention",
    )(zgq, zgkv, ck, cv, att)


def _out_proj_kernel(x_ref, yf_ref, yb_ref, bonus_ref, gate_ref, att_ref, mod_ref, lng_ref, lnb_ref,
                     wout_ref, g2_ref, rw_ref, rb_ref,
                     x1_ref, h2_ref, idx_ref, rank_ref, gates_ref, cnt_ref, run_ref):
    i = pl.program_id(0)

    @pl.when(i == 0)
    def _():
        run_ref[...] = jnp.zeros(run_ref.shape, F32)

    ones_bd = _head_ones(RW_WIDTH)
    o = yf_ref[...] + yb_ref[...]
    mu = _head_sum(o, ones_bd) * (1.0 / HEAD_DIM)
    dlt = o - mu
    var = _head_sum(dlt * dlt, ones_bd) * (1.0 / HEAD_DIM)
    ln = dlt * lax.rsqrt(var + LNX_EPS) * lng_ref[...] + lnb_ref[...]
    rw = ((ln + bonus_ref[...]) * gate_ref[...]).astype(BF16)
    att = att_ref[...]
    mix = (jnp.dot(rw, wout_ref[0:RW_WIDTH, :], preferred_element_type=F32)
           + jnp.dot(att[:, :GQ_WIDTH], wout_ref[RW_WIDTH + NA_WIDTH:, :], preferred_element_type=F32)
           + jnp.dot(att[:, GQ_WIDTH:], wout_ref[RW_WIDTH:RW_WIDTH + NA_WIDTH, :],
                     preferred_element_type=F32))
    gate1 = mod_ref[0, 2:3, :]
    shift2 = mod_ref[0, 3:4, :]
    scale2 = mod_ref[0, 4:5, :]
    x1 = x_ref[...] + gate1 * mix
    x1_ref[...] = x1
    h2 = _rms(x1, g2_ref[...]) * (1.0 + scale2) + shift2
    h2_ref[...] = h2

    logits = _dot3(h2, rw_ref[...]) + rb_ref[...]
    tm = logits.shape[0]
    col = lax.broadcasted_iota(jnp.int32, (tm, N_EXPERTS), 1)
    lane4 = lax.broadcasted_iota(jnp.int32, (tm, TOP_K), 1)
    work = logits
    sels, vals = [], []
    idx_out = jnp.zeros((tm, TOP_K), jnp.int32)
    for k in range(TOP_K):
        m = work.max(axis=-1, keepdims=True)
        idx = jnp.min(jnp.where(work == m, col, N_EXPERTS), axis=-1, keepdims=True)
        sel = col == idx
        sels.append(sel)
        vals.append(m)
        idx_out = jnp.where(lane4 == k, idx, idx_out)
        work = jnp.where(sel, -jnp.inf, work)
    es = [jnp.exp(v - vals[0]) for v in vals]
    inv = 1.0 / (es[0] + es[1] + es[2] + es[3])
    gates = jnp.zeros((tm, TOP_K), F32)
    for k in range(TOP_K):
        gates = jnp.where(lane4 == k, es[k] * inv, gates)
    assign = jnp.zeros((tm, N_EXPERTS), F32)
    for sel in sels:
        assign = assign + sel.astype(F32)
    r_i = lax.broadcasted_iota(jnp.int32, (tm, tm), 0)
    c_i = lax.broadcasted_iota(jnp.int32, (tm, tm), 1)
    before = jnp.dot((c_i < r_i).astype(BF16), assign.astype(BF16), preferred_element_type=F32)
    pos = before + run_ref[...]
    rank = jnp.zeros((tm, TOP_K), F32)
    for k in range(TOP_K):
        rk = jnp.sum(jnp.where(sels[k], pos, 0.0), axis=-1, keepdims=True)
        rank = jnp.where(lane4 == k, rk, rank)
    run_ref[...] = run_ref[...] + jnp.sum(assign, axis=0, keepdims=True)
    idx_ref[...] = idx_out
    rank_ref[...] = rank.astype(jnp.int32)
    gates_ref[...] = gates
    cnt_ref[...] = run_ref[...].astype(jnp.int32)


def _out_proj_call(x, yf, yb, bonus, gate, att, mods_l, lp):
    n_blk = T_ALL // TM_TOK
    row = lambda i: (i, 0)
    full2 = lambda r, c: pl.BlockSpec((r, c), lambda i: (0, 0))
    tokw = lambda w: pl.BlockSpec((TM_TOK, w), row)
    return pl.pallas_call(
        _out_proj_kernel,
        out_shape=(jax.ShapeDtypeStruct((T_ALL, D_MODEL), F32),
                   jax.ShapeDtypeStruct((T_ALL, D_MODEL), F32),
                   jax.ShapeDtypeStruct((T_ALL, TOP_K), jnp.int32),
                   jax.ShapeDtypeStruct((T_ALL, TOP_K), jnp.int32),
                   jax.ShapeDtypeStruct((T_ALL, TOP_K), F32),
                   jax.ShapeDtypeStruct((1, N_EXPERTS), jnp.int32)),
        grid=(n_blk,),
        in_specs=[tokw(D_MODEL), tokw(RW_WIDTH), tokw(RW_WIDTH), tokw(RW_WIDTH), tokw(RW_WIDTH),
                  tokw(ATT_WIDTH),
                  pl.BlockSpec((1, N_MOD, D_MODEL), lambda i: (_group_of_block(i, TM_TOK), 0, 0)),
                  full2(1, RW_WIDTH), full2(1, RW_WIDTH), full2(D_MODEL, D_MODEL), full2(1, D_MODEL),
                  full2(D_MODEL, N_EXPERTS), full2(1, N_EXPERTS)],
        out_specs=(tokw(D_MODEL), tokw(D_MODEL), tokw(TOP_K), tokw(TOP_K), tokw(TOP_K),
                   full2(1, N_EXPERTS)),
        scratch_shapes=[pltpu.VMEM((1, N_EXPERTS), F32)],
        compiler_params=_cparams(("arbitrary",)),
        name="out_proj_router",
    )(x, yf, yb, bonus, gate, att, mods_l, lp["rw_ln_g"], lp["rw_ln_b"], lp["w_out"], lp["norm2_g"],
      lp["router_w"], lp["router_b"])


def _expert_kernel(be_ref, meta_ref, slot_tok_ref, h2_hbm, wgu_ref, bgu_ref, wd_ref, bd_ref,
                   y_ref, xbuf, sem, wgu_bf, wd_bf):
    i = pl.program_id(0)
    n_used = meta_ref[0]
    slot = i % 2

    def gather(blk, buf_slot):
        def body(r, carry):
            tok = slot_tok_ref[blk * TM_EXP + r]
            pltpu.make_async_copy(h2_hbm.at[pl.ds(tok, 1), :], xbuf.at[buf_slot, pl.ds(r, 1), :],
                                  sem.at[buf_slot]).start()
            return carry
        lax.fori_loop(0, TM_EXP, body, 0, unroll=8)

    @pl.when(i == 0)
    def _():
        gather(0, 0)

    @pl.when(i + 1 < n_used)
    def _():
        gather(i + 1, 1 - slot)

    @pl.when(i < n_used)
    def _():
        pltpu.make_async_copy(h2_hbm.at[pl.ds(0, TM_EXP), :], xbuf.at[slot], sem.at[slot]).wait()
        e = be_ref[i]
        e_prev = be_ref[jnp.maximum(i - 1, 0)]

        @pl.when((i == 0) | (e != e_prev))
        def _():
            wgu_bf[...] = wgu_ref[0, 0].astype(BF16)
            wd_bf[...] = wd_ref[0, 0].astype(BF16)

        x = xbuf[slot].astype(BF16)
        gu = jnp.dot(x, wgu_bf[...], preferred_element_type=F32) + bgu_ref[0, 0]
        glu = jnp.minimum(gu[:, :D_FF], SWIGLU_LIMIT)
        lin = jnp.clip(gu[:, D_FF:], -SWIGLU_LIMIT, SWIGLU_LIMIT)
        act = glu * _sigmoid(SWIGLU_ALPHA * glu) * (lin + 1.0)
        y_ref[...] = jnp.dot(act.astype(BF16), wd_bf[...], preferred_element_type=F32) + bd_ref[0, 0]


def _expert_call(block_e, meta, slot_tok, h2, w_gu, b_gu, w_down, b_down, layer):
    def wmap(i, be, meta_, st):
        return (layer, be[i], 0, 0)

    def omap(i, be, meta_, st):
        return (jnp.minimum(i, meta_[0] - 1), 0)

    grid_spec = pltpu.PrefetchScalarGridSpec(
        num_scalar_prefetch=3,
        grid=(N_SLOT_BLOCKS,),
        in_specs=[pl.BlockSpec(memory_space=pl.ANY),
                  pl.BlockSpec((1, 1, D_MODEL, 2 * D_FF), wmap),
                  pl.BlockSpec((1, 1, 1, 2 * D_FF), wmap),
                  pl.BlockSpec((1, 1, D_FF, D_MODEL), wmap),
                  pl.BlockSpec((1, 1, 1, D_MODEL), wmap)],
        out_specs=pl.BlockSpec((TM_EXP, D_MODEL), omap),
        scratch_shapes=[pltpu.VMEM((2, TM_EXP, D_MODEL), F32), pltpu.SemaphoreType.DMA((2,)),
                        pltpu.VMEM((D_MODEL, 2 * D_FF), BF16), pltpu.VMEM((D_FF, D_MODEL), BF16)],
    )
    return pl.pallas_call(
        _expert_kernel,
        out_shape=jax.ShapeDtypeStruct((N_SLOTS, D_MODEL), F32),
        grid_spec=grid_spec,
        compiler_params=_cparams(("arbitrary",)),
        name="moe_experts",
    )(block_e, meta, slot_tok, h2, w_gu, b_gu.reshape(DEPTH, N_EXPERTS, 1, 2 * D_FF), w_down,
      b_down.reshape(DEPTH, N_EXPERTS, 1, D_MODEL))


def _combine_kernel(dest_ref, ys_hbm, x1_ref, gates_ref, mod_ref, fg_ref, o_ref, ybuf, sem, *, final):
    i = pl.program_id(0)
    n = pl.num_programs(0)
    slot = i % 2

    def gather(blk, buf_slot):
        def body(r, carry):
            for k in range(TOP_K):
                d = dest_ref[(blk * TM_CMB + r) * TOP_K + k]
                pltpu.make_async_copy(ys_hbm.at[pl.ds(d, 1), :], ybuf.at[buf_slot, k, pl.ds(r, 1), :],
                                      sem.at[buf_slot]).start()
            return carry
        lax.fori_loop(0, TM_CMB, body, 0, unroll=4)

    @pl.when(i == 0)
    def _():
        gather(0, 0)

    @pl.when(i + 1 < n)
    def _():
        gather(i + 1, 1 - slot)

    for k in range(TOP_K):
        pltpu.make_async_copy(ys_hbm.at[pl.ds(0, TM_CMB), :], ybuf.at[slot, k], sem.at[slot]).wait()
    gates = gates_ref[...]
    ff = gates[:, 0:1] * ybuf[slot, 0]
    for k in range(1, TOP_K):
        ff = ff + gates[:, k:k + 1] * ybuf[slot, k]
    x = x1_ref[...] + mod_ref[0, 5:6, :] * ff
    if final:
        x = _rms(x, fg_ref[...])
    o_ref[...] = x


def _combine_call(dest, ys, x1, gates, mods_l, final_g, final):
    n_blk = T_ALL // TM_CMB
    row = lambda i, d: (i, 0)
    grid_spec = pltpu.PrefetchScalarGridSpec(
        num_scalar_prefetch=1,
        grid=(n_blk,),
        in_specs=[pl.BlockSpec(memory_space=pl.ANY),
                  pl.BlockSpec((TM_CMB, D_MODEL), row),
                  pl.BlockSpec((TM_CMB, TOP_K), row),
                  pl.BlockSpec((1, N_MOD, D_MODEL), lambda i, d: (_group_of_block(i, TM_CMB), 0, 0)),
                  pl.BlockSpec((1, D_MODEL), lambda i, d: (0, 0))],
        out_specs=pl.BlockSpec((TM_CMB, D_MODEL), row),
        scratch_shapes=[pltpu.VMEM((2, TOP_K, TM_CMB, D_MODEL), F32), pltpu.SemaphoreType.DMA((2,))],
    )
    return pl.pallas_call(
        functools.partial(_combine_kernel, final=final),
        out_shape=jax.ShapeDtypeStruct((T_ALL, D_MODEL), F32),
        grid_spec=grid_spec,
        compiler_params=_cparams(("arbitrary",)),
        name="moe_combine_final" if final else "moe_combine",
    )(dest, ys, x1, gates, mods_l, final_g)


def _routing_tables(idx, rank, counts):
    counts = counts.reshape(N_EXPERTS)
    padded = (counts + TM_EXP - 1) // TM_EXP * TM_EXP
    pad_end = jnp.cumsum(padded)
    pad_start = pad_end - padded
    dest = (pad_start[idx] + rank).reshape(-1).astype(jnp.int32)
    tok = jnp.arange(T_ALL * TOP_K, dtype=jnp.int32) // TOP_K
    slot_tok = jnp.zeros((N_SLOTS,), jnp.int32).at[dest].set(tok)
    n_used = (pad_end[-1] // TM_EXP).astype(jnp.int32)
    blk = jnp.minimum(jnp.arange(N_SLOT_BLOCKS, dtype=jnp.int32), n_used - 1)
    block_e = jnp.sum((pad_end[None, :] <= (blk * TM_EXP)[:, None]).astype(jnp.int32), axis=1)
    block_e = jnp.minimum(block_e, N_EXPERTS - 1)
    return dest, slot_tok, block_e, n_used.reshape(1)


def kernel(x_prompt, x_sample, cache_na_k, cache_na_v, cache_gqa_k, cache_gqa_v, state_rwkv, c, c_ctx,
           w_mod, b_mod, norm1_g, norm2_g, w_in, rw_shift, rw_w0, rw_w_up, rw_a0, rw_a_up, rw_g_up,
           rw_k_k, rw_k_a, rw_r_k, rw_ln_g, rw_ln_b, na_rpb, q_norm, k_norm, w_out, router_w, router_b,
           moe_w_gu, moe_b_gu, moe_w_down, moe_b_down, final_norm_g):
    x = jnp.concatenate([x_prompt.reshape(T_CTX, D_MODEL), x_sample.reshape(T_LAT, D_MODEL)], axis=0)
    cvecs = jnp.concatenate([c_ctx[None, :], c, jnp.zeros((SUBLANES - N_GROUPS, D_MODEL), F32)], axis=0)
    mods = _mods_call(cvecs, w_mod, b_mod)
    mods = mods[:, :N_GROUPS].reshape(DEPTH, N_GROUPS, N_MOD, D_MODEL)
    rope = _rope_tables()
    w_in_bf = w_in.astype(BF16)
    w_out_bf = w_out.astype(BF16)
    rw_g_up_bf = rw_g_up.astype(BF16)
    final_g = final_norm_g.reshape(1, D_MODEL)

    na_k_l, na_v_l, g_k_l, g_v_l, st_l = [], [], [], [], []
    for l in range(DEPTH):
        lp = {
            "rw_shift": rw_shift[l], "rw_k_k": rw_k_k[l].reshape(1, RW_WIDTH),
            "rw_k_a": rw_k_a[l].reshape(1, RW_WIDTH), "rw_r_k": rw_r_k[l].reshape(1, RW_WIDTH),
            "rw_w0": rw_w0[l], "rw_w_up": rw_w_up[l], "rw_a0": rw_a0[l], "rw_a_up": rw_a_up[l],
            "rw_g_up": rw_g_up_bf[l], "rw_ln_g": rw_ln_g[l].reshape(1, RW_WIDTH),
            "rw_ln_b": rw_ln_b[l].reshape(1, RW_WIDTH), "w_out": w_out_bf[l],
            "norm2_g": norm2_g[l].reshape(1, D_MODEL), "router_w": router_w[l],
            "router_b": router_b[l].reshape(1, N_EXPERTS),
        }
        qk_g = jnp.concatenate([jnp.tile(q_norm[l], GQA_Q_HEADS), jnp.tile(k_norm[l], GQA_KV_HEADS)])
        zrw, zna, zgq, zgkv = _in_proj_call(x, mods[l], norm1_g[l].reshape(1, D_MODEL), w_in_bf[l],
                                            qk_g.reshape(1, GQ_WIDTH + GKV_WIDTH), rope)

        s0_lat = jnp.swapaxes(state_rwkv[:, l], -1, -2)
        rw_ctx = _rwkv_call(zrw, s0_lat, lp, (), n_seq=BATCH, seq_len=SEQ, row_base=0,
                            has_init=False, emit_state=True)
        yf, yb, bonus, gate = _rwkv_call(zrw, s0_lat, lp, rw_ctx[:4], n_seq=DEC_BATCH, seq_len=DEC_SEQ,
                                         row_base=T_CTX, has_init=True, emit_state=False)
        st_l.append(rw_ctx[4])

        att, nk, nv, gk, gv = _ctx_attn_call(zna, zgq, zgkv)
        na_k_l.append(nk)
        na_v_l.append(nv)
        g_k_l.append(gk)
        g_v_l.append(gv)
        att = _lat_na_call(zna, cache_na_k[:, l].reshape(DEC_BATCH, PAST_LEN, NA_WIDTH),
                           cache_na_v[:, l].reshape(DEC_BATCH, PAST_LEN, NA_WIDTH),
                           _na_bias_table(na_rpb[l]), att)
        att = _lat_gqa_call(zgq, zgkv, cache_gqa_k[:, l].reshape(DEC_BATCH, PAST_LEN, GKV_WIDTH),
                            cache_gqa_v[:, l].reshape(DEC_BATCH, PAST_LEN, GKV_WIDTH), att)

        x1, h2, idx, rank, gates, counts = _out_proj_call(x, yf, yb, bonus, gate, att, mods[l], lp)
        dest, slot_tok, block_e, n_used = _routing_tables(idx, rank, counts)
        ys = _expert_call(block_e, n_used, slot_tok, h2, moe_w_gu, moe_b_gu, moe_w_down, moe_b_down, l)
        x = _combine_call(dest, ys, x1, gates, mods[l], final_g, l == DEPTH - 1)

    y_prompt = x[:T_CTX].reshape(BATCH, SEQ, D_MODEL)
    y_sample = x[T_CTX:].reshape(DEC_BATCH, DEC_SEQ, D_MODEL)
    heads = lambda ts, n: jnp.stack(ts, axis=1).reshape(BATCH, DEPTH, SEQ, n, HEAD_DIM)
    return (y_prompt, y_sample, heads(na_k_l, NA_HEADS), heads(na_v_l, NA_HEADS),
            heads(g_k_l, GQA_KV_HEADS), heads(g_v_l, GQA_KV_HEADS), jnp.stack(st_l, axis=1))
```

```python
import functools

import numpy as np
import jax
import jax.numpy as jnp
from jax import lax
from jax.experimental import pallas as pl
from jax.experimental.pallas import tpu as pltpu

F32 = jnp.float32
BF16 = jnp.bfloat16

D_MODEL = 1024
BATCH = 32
SEQ = 256
DEPTH = 4
DEC_BATCH = 2
DEC_SEQ = 2048
PAST_LEN = 512
GRID_W = 64
GRID_H = DEC_SEQ // GRID_W
HEAD_DIM = 64
RW_HEADS = 4
RW_WIDTH = RW_HEADS * HEAD_DIM
DECAY_LORA = 64
ICLR_LORA = 64
GATE_LORA = 128
NA_HEADS = 4
NA_WIDTH = NA_HEADS * HEAD_DIM
NA_WIN_ROWS = 8
NA_WIN_COLS = 16
GQA_Q_HEADS = 8
GQA_KV_HEADS = 2
GQA_GROUP = GQA_Q_HEADS // GQA_KV_HEADS
GQ_WIDTH = GQA_Q_HEADS * HEAD_DIM
GKV_WIDTH = GQA_KV_HEADS * HEAD_DIM
RW_PROJ = 3 * RW_WIDTH + DECAY_LORA + ICLR_LORA + GATE_LORA
NA_PROJ = 3 * NA_WIDTH
GQA_PROJ = GQ_WIDTH + 2 * GKV_WIDTH
IN_PROJ = RW_PROJ + NA_PROJ + GQA_PROJ
ATT_WIDTH = GQ_WIDTH + NA_WIDTH
N_EXPERTS = 32
TOP_K = 4
D_FF = D_MODEL
SWIGLU_LIMIT = 7.0
SWIGLU_ALPHA = 1.702
ROPE_THETA = 10000.0
NORM_EPS = 1e-6
LNX_EPS = 64e-5
N_MOD = 6
ATT_SCALE = HEAD_DIM ** -0.5
NEG_BIG = -1e30

T_CTX = BATCH * SEQ
T_LAT = DEC_BATCH * DEC_SEQ
T_ALL = T_CTX + T_LAT
N_GROUPS = 1 + DEC_BATCH

LANES = 128
SUBLANES = 8
TM_TOK = 512
CHUNK = 64
RW_SEQ_GROUP = 2
TQ_GQA = 256
TM_EXP = 256
TM_CMB = 256
N_SLOT_BLOCKS = T_ALL * TOP_K // TM_EXP + N_EXPERTS
N_SLOTS = N_SLOT_BLOCKS * TM_EXP
VMEM_LIMIT = 56 * 1024 * 1024

NT_DIMS = (((1,), (1,)), ((), ()))
TN_DIMS = (((0,), (0,)), ((), ()))


def _bdot(a, b, dims=None):
    a = a.astype(BF16)
    b = b.astype(BF16)
    if dims is None:
        return jnp.dot(a, b, preferred_element_type=F32)
    return lax.dot_general(a, b, dims, preferred_element_type=F32)


def _split(a):
    hi = a.astype(BF16)
    lo = (a - hi.astype(F32)).astype(BF16)
    return hi, lo


def _dot3(a, b, dims=None):
    ah, al = _split(a)
    bh, bl = _split(b)
    return _bdot(ah, bh, dims) + _bdot(ah, bl, dims) + _bdot(al, bh, dims)


def _dot_exact_lhs(a_exact, b):
    h1 = b.astype(BF16)
    r1 = b - h1.astype(F32)
    h2 = r1.astype(BF16)
    h3 = (r1 - h2.astype(F32)).astype(BF16)
    return _bdot(a_exact, h1) + _bdot(a_exact, h2) + _bdot(a_exact, h3)


def _head_ones(n):
    r = lax.broadcasted_iota(jnp.int32, (n, n), 0) // HEAD_DIM
    c = lax.broadcasted_iota(jnp.int32, (n, n), 1) // HEAD_DIM
    return (r == c).astype(BF16)


def _head_sum(x, ones_bd):
    hi, lo = _split(x)
    return (jnp.dot(hi, ones_bd, preferred_element_type=F32)
            + jnp.dot(lo, ones_bd, preferred_element_type=F32))


def _sigmoid(x):
    return 1.0 / (1.0 + jnp.exp(-x))


def _cparams(sem):
    return pltpu.CompilerParams(dimension_semantics=sem, vmem_limit_bytes=VMEM_LIMIT)


def _group_of_block(i, rows_per_block):
    first_lat = T_CTX // rows_per_block
    per_sample = DEC_SEQ // rows_per_block
    return jnp.where(i < first_lat, 0, 1 + (i - first_lat) // per_sample)


def _mods_kernel(c_ref, w_ref, b_ref, o_ref):
    c = c_ref[...]
    s = c * _sigmoid(c)
    o_ref[0] = _dot3(s, w_ref[0]) + b_ref[0]


def _mods_call(cvecs, w_mod, b_mod):
    tn = 1536
    n_rows = cvecs.shape[0]
    return pl.pallas_call(
        _mods_kernel,
        out_shape=jax.ShapeDtypeStruct((DEPTH, n_rows, N_MOD * D_MODEL), F32),
        grid=(DEPTH, N_MOD * D_MODEL // tn),
        in_specs=[
            pl.BlockSpec((n_rows, D_MODEL), lambda l, j: (0, 0)),
            pl.BlockSpec((1, D_MODEL, tn), lambda l, j: (l, 0, j)),
            pl.BlockSpec((1, 1, tn), lambda l, j: (l, 0, j)),
        ],
        out_specs=pl.BlockSpec((1, n_rows, tn), lambda l, j: (l, 0, j)),
        compiler_params=_cparams(("arbitrary", "arbitrary")),
        name="adaln_mods",
    )(cvecs, w_mod, b_mod.reshape(DEPTH, 1, N_MOD * D_MODEL))


def _rms(x, g):
    ms = jnp.mean(x * x, axis=-1, keepdims=True)
    return x * lax.rsqrt(ms + NORM_EPS) * g


def _in_proj_kernel(x_ref, mod_ref, g_ref, w_ref, qkg_ref, rc_ref, rs1_ref, rs2_ref,
                    zrw_ref, zna_ref, zgq_ref, zgkv_ref):
    x = x_ref[...]
    shift1 = mod_ref[0, 0:1, :]
    scale1 = mod_ref[0, 1:2, :]
    h = _rms(x, g_ref[...]) * (1.0 + scale1) + shift1
    z = jnp.dot(h.astype(BF16), w_ref[...], preferred_element_type=F32)
    zrw_ref[...] = z[:, :RW_PROJ]
    zna_ref[...] = z[:, RW_PROJ:RW_PROJ + NA_PROJ]
    qk_w = GQ_WIDTH + GKV_WIDTH
    qk = z[:, RW_PROJ + NA_PROJ:RW_PROJ + NA_PROJ + qk_w]
    ones_bd = _head_ones(LANES)
    sq = qk * qk
    ssq = jnp.concatenate(
        [_head_sum(sq[:, j * LANES:(j + 1) * LANES], ones_bd) for j in range(qk_w // LANES)], axis=1)
    qkn = qk * lax.rsqrt(ssq * (1.0 / HEAD_DIM) + NORM_EPS) * qkg_ref[...]
    reps = qk_w // LANES
    rc = jnp.concatenate([rc_ref[...]] * reps, axis=1)
    rs1 = jnp.concatenate([rs1_ref[...]] * reps, axis=1)
    rs2 = jnp.concatenate([rs2_ref[...]] * reps, axis=1)
    half = HEAD_DIM // 4
    qkr = qkn * rc + pltpu.roll(qkn, half, 1) * rs1 + pltpu.roll(qkn, qk_w - half, 1) * rs2
    zgq_ref[...] = qkr[:, :GQ_WIDTH]
    zgkv_ref[:, :GKV_WIDTH] = qkr[:, GQ_WIDTH:]
    zgkv_ref[:, GKV_WIDTH:] = z[:, RW_PROJ + NA_PROJ + qk_w:]


def _rope_tables():
    t = np.arange(DEC_SEQ)
    pos = np.stack([t // GRID_W, t % GRID_W], axis=1).astype(np.float32)
    axis_dim = HEAD_DIM // 2
    inv = ROPE_THETA ** (-np.arange(0, axis_dim, 2, dtype=np.float32) / axis_dim)
    d = np.arange(LANES) % HEAD_DIM
    part = d // axis_dim
    within = d % axis_dim
    freq = within % (axis_dim // 2)
    second = within // (axis_dim // 2)
    ang = jnp.asarray(pos)[:, part] * jnp.asarray(inv)[freq][None, :]
    cos = jnp.cos(ang)
    sin = jnp.sin(ang)
    s1 = jnp.where(second[None, :] == 1, sin, 0.0)
    s2 = jnp.where(second[None, :] == 0, -sin, 0.0)
    ident = jnp.ones((TM_TOK, LANES), F32)
    zero = jnp.zeros((TM_TOK, LANES), F32)
    return (jnp.concatenate([cos, ident], 0), jnp.concatenate([s1, zero], 0),
            jnp.concatenate([s2, zero], 0))


def _in_proj_call(x, mods_l, g1, w_in_bf, qk_g, rope):
    n_blk = T_ALL // TM_TOK
    lat_blk = DEC_SEQ // TM_TOK
    first_lat = T_CTX // TM_TOK

    def rope_idx(i):
        return (jnp.where(i < first_lat, lat_blk, (i - first_lat) % lat_blk), 0)

    row = lambda i: (i, 0)
    rope_spec = pl.BlockSpec((TM_TOK, LANES), rope_idx)
    return pl.pallas_call(
        _in_proj_kernel,
        out_shape=(jax.ShapeDtypeStruct((T_ALL, RW_PROJ), F32),
                   jax.ShapeDtypeStruct((T_ALL, NA_PROJ), F32),
                   jax.ShapeDtypeStruct((T_ALL, GQ_WIDTH), F32),
                   jax.ShapeDtypeStruct((T_ALL, 2 * GKV_WIDTH), F32)),
        grid=(n_blk,),
        in_specs=[
            pl.BlockSpec((TM_TOK, D_MODEL), row),
            pl.BlockSpec((1, N_MOD, D_MODEL), lambda i: (_group_of_block(i, TM_TOK), 0, 0)),
            pl.BlockSpec((1, D_MODEL), lambda i: (0, 0)),
            pl.BlockSpec((D_MODEL, IN_PROJ), lambda i: (0, 0)),
            pl.BlockSpec((1, GQ_WIDTH + GKV_WIDTH), lambda i: (0, 0)),
            rope_spec, rope_spec, rope_spec,
        ],
        out_specs=(pl.BlockSpec((TM_TOK, RW_PROJ), row), pl.BlockSpec((TM_TOK, NA_PROJ), row),
                   pl.BlockSpec((TM_TOK, GQ_WIDTH), row), pl.BlockSpec((TM_TOK, 2 * GKV_WIDTH), row)),
        compiler_params=_cparams(("arbitrary",)),
        name="in_proj",
    )(x, mods_l, g1, w_in_bf, qk_g, *rope)


def _softplus(x):
    return jnp.maximum(x, 0.0) + jnp.log(1.0 + jnp.exp(-jnp.abs(x)))


def _rw_pre(z, zprev, znext, shift_ref, kk_ref, ka_ref, rk_ref, w0_ref, wup_ref, a0_ref, aup_ref, d,
            ones_bd):
    rows = lax.broadcasted_iota(jnp.int32, z.shape, 0)
    zp = jnp.where(rows == 0, zprev, pltpu.roll(z, 1, 0))
    zn = jnp.where(rows == CHUNK - 1, znext, pltpu.roll(z, CHUNK - 1, 0))
    zs = zp * shift_ref[0:1, :] + z * shift_ref[1:2, :] + zn * shift_ref[2:3, :]
    r = zs[:, 0:RW_WIDTH]
    k = zs[:, RW_WIDTH:2 * RW_WIDTH]
    v = zs[:, 2 * RW_WIDTH:3 * RW_WIDTH]
    o = 3 * RW_WIDTH
    wd = zs[:, o:o + DECAY_LORA]
    ad = zs[:, o + DECAY_LORA:o + DECAY_LORA + ICLR_LORA]
    gd = zs[:, o + DECAY_LORA + ICLR_LORA:]
    kk = k * kk_ref[...]
    kk = kk / jnp.maximum(jnp.sqrt(_head_sum(kk * kk, ones_bd)), 1e-12)
    tw = jnp.tanh(wd)
    wl = w0_ref[d:d + 1, :] + _dot3(tw, wup_ref[d])
    lw = -jnp.exp(-_softplus(-wl) - 0.5)
    a_sig = _sigmoid(a0_ref[d:d + 1, :] + _dot3(ad, aup_ref[d]))
    k_d = k * (1.0 + (a_sig - 1.0) * ka_ref[...])
    bonus = _head_sum(r * k_d * rk_ref[...], ones_bd) * v
    return dict(r=r, k=k_d, v=v, a=-kk, b=kk * a_sig, lw=lw, bonus=bonus, gd=gd, ad=ad, k_raw=k)


def _chunk_masks(rev):
    t = lax.broadcasted_iota(jnp.int32, (CHUNK, CHUNK), 0)
    j = lax.broadcasted_iota(jnp.int32, (CHUNK, CHUNK), 1)
    return ((j >= t), (j > t)) if rev else ((j <= t), (j < t))


def _wkv_scale(p, incl, rev):
    lw = p["lw"]
    cs = _dot_exact_lhs(incl.astype(BF16), lw)
    tot = cs[0:1, :] if rev else cs[CHUNK - 1:CHUNK, :]
    e_inv = jnp.exp(-cs)
    e_rem = jnp.exp(tot - cs)
    bf = lambda x: x.astype(BF16)
    return dict(at=bf(p["a"] * jnp.exp(cs - lw)), rt=bf(p["r"] * jnp.exp(cs)),
                bt=bf(p["b"] * e_inv), kt=bf(p["k"] * e_inv),
                bh=bf(p["b"] * e_rem), kh=bf(p["k"] * e_rem), v=bf(p["v"]), gtot=jnp.exp(tot))


def _wkv_chunks(items):
    bf = lambda x: x.astype(BF16)
    c = CHUNK
    n_sq = int(np.log2(c))
    ar = [jnp.concatenate([it["at"], it["rt"]], 0) for it in items]
    m_b = [_bdot(a, it["bt"], NT_DIMS) for a, it in zip(ar, items)]
    m_k = [_bdot(a, it["kt"], NT_DIMS) for a, it in zip(ar, items)]
    s_bf = [bf(it["s"]) for it in items]
    xs = [bf(jnp.where(it["strict"], m[:c], 0.0)) for m, it in zip(m_b, items)]
    a_rb = [bf(jnp.where(it["incl"], m[c:], 0.0)) for m, it in zip(m_b, items)]
    a_ak = [bf(jnp.where(it["strict"], m[:c], 0.0)) for m, it in zip(m_k, items)]
    a_rk = [bf(jnp.where(it["incl"], m[c:], 0.0)) for m, it in zip(m_k, items)]
    ws = [_bdot(ak, it["v"]) + _bdot(it["at"], s, NT_DIMS) for ak, it, s in zip(a_ak, items, s_bf)]
    for step in range(n_sq):
        wb = [bf(w) for w in ws]
        ws = [w + _bdot(x, b) for w, x, b in zip(ws, xs, wb)]
        if step < n_sq - 1:
            xs = [bf(_bdot(x, x)) for x in xs]
    wb = [bf(w) for w in ws]
    ys = [_bdot(rb, w) + _bdot(rk, it["v"]) + _bdot(it["rt"], s, NT_DIMS)
          for rb, rk, w, it, s in zip(a_rb, a_rk, wb, items, s_bf)]
    s_new = [it["s"] * it["gtot"] + _bdot(w, it["bh"], TN_DIMS) + _bdot(it["v"], it["kh"], TN_DIMS)
             for w, it in zip(wb, items)]
    return ys, s_new


def _rwkv_kernel(zf_ref, zfp_ref, zfn_ref, zb_ref, zbp_ref, zbn_ref, s0_ref,
                 shift_ref, kk_ref, ka_ref, rk_ref, w0_ref, wup_ref, a0_ref, aup_ref, gup_ref,
                 *refs, n_chunks, n_group, has_init, emit_state, n_alias):
    refs = refs[n_alias:]
    if emit_state:
        yf_ref, yb_ref, bonus_ref, gate_ref, st_ref, h_ref = refs
    else:
        yf_ref, yb_ref, bonus_ref, gate_ref, h_ref = refs
        st_ref = None
    i = pl.program_id(1)
    ones_bd = _head_ones(RW_WIDTH)

    @pl.when(i == 0)
    def _():
        if has_init:
            h_ref[...] = s0_ref[...]
        else:
            h_ref[...] = jnp.zeros(h_ref.shape, F32)

    params = (shift_ref, kk_ref, ka_ref, rk_ref, w0_ref, wup_ref, a0_ref, aup_ref)
    first = i == 0
    last = i == n_chunks - 1
    zero_row = jnp.zeros((1, RW_PROJ), F32)
    masks = (_chunk_masks(False), _chunk_masks(True))
    items = []
    for g in range(n_group):
        pf = _rw_pre(zf_ref[g], jnp.where(first, zero_row, zfp_ref[g, 0, SUBLANES - 1:SUBLANES, :]),
                     jnp.where(last, zero_row, zfn_ref[g, 0, 0:1, :]), *params, 0, ones_bd)
        pb = _rw_pre(zb_ref[g], jnp.where(last, zero_row, zbp_ref[g, 0, SUBLANES - 1:SUBLANES, :]),
                     jnp.where(first, zero_row, zbn_ref[g, 0, 0:1, :]), *params, 1, ones_bd)
        a_sig_b = _sigmoid(a0_ref[1:2, :] + _dot3(pf["ad"], aup_ref[1]))
        k_b = pf["k_raw"] * (1.0 + (a_sig_b - 1.0) * ka_ref[...])
        bonus_ref[g] = pf["bonus"] + _head_sum(pf["r"] * k_b * rk_ref[...], ones_bd) * pf["v"]
        gate_ref[g] = _bdot(_sigmoid(pf["gd"]), gup_ref[...])
        for d, p in ((0, pf), (1, pb)):
            incl, strict = masks[d]
            sc = _wkv_scale(p, incl, d == 1)
            for h in range(RW_HEADS):
                sl = slice(h * HEAD_DIM, (h + 1) * HEAD_DIM)
                it = {k: v[:, sl] for k, v in sc.items()}
                it.update(s=h_ref[g, d, h], incl=incl, strict=strict, where=(g, d, h))
                items.append(it)

    ys, s_new = _wkv_chunks(items)
    for it, y, s in zip(items, ys, s_new):
        g, d, h = it["where"]
        y_ref = yb_ref if d else yf_ref
        y_ref[g, :, h * HEAD_DIM:(h + 1) * HEAD_DIM] = y
        h_ref[g, d, h] = s

    if emit_state:
        @pl.when(last)
        def _():
            for it, s in zip(items, s_new):
                g, d, h = it["where"]
                st_ref[g, d, h] = s


def _rwkv_call(zrw, s0, lp, prev_outs, *, n_seq, seq_len, row_base, has_init, emit_state):
    g = RW_SEQ_GROUP
    n_chunks = seq_len // CHUNK
    n_rows8 = seq_len // SUBLANES
    per8 = CHUNK // SUBLANES
    n_view = T_ALL // seq_len
    base_g = row_base // seq_len // g
    z3 = zrw.reshape(n_view, seq_len, RW_PROJ)
    z4 = zrw.reshape(n_view, n_rows8, SUBLANES, RW_PROJ)

    fwd = lambda i: i
    bwd = lambda i: n_chunks - 1 - i
    main = lambda c: pl.BlockSpec((g, CHUNK, RW_PROJ), lambda b, i: (base_g + b, c(i), 0))
    prev8 = lambda c: pl.BlockSpec((g, 1, SUBLANES, RW_PROJ),
                                   lambda b, i: (base_g + b, jnp.maximum(c(i) * per8 - 1, 0), 0, 0))
    next8 = lambda c: pl.BlockSpec((g, 1, SUBLANES, RW_PROJ),
                                   lambda b, i: (base_g + b, jnp.minimum((c(i) + 1) * per8, n_rows8 - 1), 0, 0))
    full = lambda shape: pl.BlockSpec(shape, lambda b, i: (0,) * len(shape))
    out_f = pl.BlockSpec((g, CHUNK, RW_WIDTH), lambda b, i: (base_g + b, i, 0))
    out_b = pl.BlockSpec((g, CHUNK, RW_WIDTH), lambda b, i: (base_g + b, bwd(i), 0))
    tok = jax.ShapeDtypeStruct((n_view, seq_len, RW_WIDTH), F32)
    out_shape = [tok, tok, tok, tok]
    out_specs = [out_f, out_b, out_f, out_f]
    state_block = (g, 2, RW_HEADS, HEAD_DIM, HEAD_DIM)
    if emit_state:
        out_shape.append(jax.ShapeDtypeStruct((n_seq, 2, RW_HEADS, HEAD_DIM, HEAD_DIM), F32))
        out_specs.append(pl.BlockSpec(state_block, lambda b, i: (b, 0, 0, 0, 0)))
    n_alias = len(prev_outs)
    n_in = 16
    aliases = {n_in + j: j for j in range(n_alias)}
    state_spec = pl.BlockSpec(state_block, lambda b, i: (b if has_init else 0, 0, 0, 0, 0))
    kern = functools.partial(_rwkv_kernel, n_chunks=n_chunks, n_group=g, has_init=has_init,
                             emit_state=emit_state, n_alias=n_alias)
    prev_views = [p.reshape(n_view, seq_len, RW_WIDTH) for p in prev_outs]
    outs = pl.pallas_call(
        kern,
        out_shape=tuple(out_shape),
        grid=(n_seq // g, n_chunks),
        in_specs=[main(fwd), prev8(fwd), next8(fwd), main(bwd), prev8(bwd), next8(bwd), state_spec,
                  full((3, RW_PROJ)), full((1, RW_WIDTH)), full((1, RW_WIDTH)), full((1, RW_WIDTH)),
                  full((2, RW_WIDTH)), full((2, DECAY_LORA, RW_WIDTH)),
                  full((2, RW_WIDTH)), full((2, ICLR_LORA, RW_WIDTH)), full((GATE_LORA, RW_WIDTH))]
                 + [pl.BlockSpec(memory_space=pl.ANY)] * n_alias,
        out_specs=tuple(out_specs),
        scratch_shapes=[pltpu.VMEM(state_block, F32)],
        input_output_aliases=aliases,
        compiler_params=_cparams(("arbitrary", "arbitrary")),
        name="rwkv_scan_init" if has_init else "rwkv_scan_zero",
    )(z3, z4, z4, z3, z4, z4, s0, lp["rw_shift"], lp["rw_k_k"], lp["rw_k_a"], lp["rw_r_k"],
      lp["rw_w0"], lp["rw_w_up"], lp["rw_a0"], lp["rw_a_up"], lp["rw_g_up"], *prev_views)
    return tuple(o.reshape(T_ALL, RW_WIDTH) for o in outs[:4]) + tuple(outs[4:])


def _softmax_pv(scores, values):
    m = scores[0].max(axis=-1, keepdims=True)
    for s in scores[1:]:
        m = jnp.maximum(m, s.max(axis=-1, keepdims=True))
    es = [jnp.exp(s - m) for s in scores]
    l = es[0].sum(axis=-1, keepdims=True)
    for e in es[1:]:
        l = l + e.sum(axis=-1, keepdims=True)
    inv = 1.0 / l
    o = _bdot(es[0] * inv, values[0])
    for e, v in zip(es[1:], values[1:]):
        o = o + _bdot(e * inv, v)
    return o


def _head(x, h):
    return x[:, h * HEAD_DIM:(h + 1) * HEAD_DIM]


def _ctx_attn_kernel(zna_ref, zgq_ref, zgkv_ref, att_ref, nk_ref, nv_ref, gk_ref, gv_ref):
    zna = zna_ref[...]
    q, k, v = zna[:, :NA_WIDTH], zna[:, NA_WIDTH:2 * NA_WIDTH], zna[:, 2 * NA_WIDTH:]
    nk_ref[0] = k
    nv_ref[0] = v
    gq = zgq_ref[...]
    gkv = zgkv_ref[...]
    gk, gv = gkv[:, :GKV_WIDTH], gkv[:, GKV_WIDTH:]
    gk_ref[0] = gk
    gv_ref[0] = gv
    for h in range(GQA_Q_HEADS):
        kv = h // GQA_GROUP
        s = _bdot(_head(gq, h), _head(gk, kv), NT_DIMS) * ATT_SCALE
        att_ref[:, h * HEAD_DIM:(h + 1) * HEAD_DIM] = _softmax_pv([s], [_head(gv, kv)]).astype(BF16)
    for h in range(NA_HEADS):
        s = _bdot(_head(q, h), _head(k, h), NT_DIMS) * ATT_SCALE
        o = _softmax_pv([s], [_head(v, h)])
        att_ref[:, GQ_WIDTH + h * HEAD_DIM:GQ_WIDTH + (h + 1) * HEAD_DIM] = o.astype(BF16)


def _ctx_attn_call(zna, zgq, zgkv):
    row = lambda b: (b, 0)
    bat = lambda b: (b, 0, 0)
    return pl.pallas_call(
        _ctx_attn_kernel,
        out_shape=(jax.ShapeDtypeStruct((T_ALL, ATT_WIDTH), BF16),
                   jax.ShapeDtypeStruct((BATCH, SEQ, NA_WIDTH), F32),
                   jax.ShapeDtypeStruct((BATCH, SEQ, NA_WIDTH), F32),
                   jax.ShapeDtypeStruct((BATCH, SEQ, GKV_WIDTH), F32),
                   jax.ShapeDtypeStruct((BATCH, SEQ, GKV_WIDTH), F32)),
        grid=(BATCH,),
        in_specs=[pl.BlockSpec((SEQ, NA_PROJ), row), pl.BlockSpec((SEQ, GQ_WIDTH), row),
                  pl.BlockSpec((SEQ, 2 * GKV_WIDTH), row)],
        out_specs=(pl.BlockSpec((SEQ, ATT_WIDTH), row),
                   pl.BlockSpec((1, SEQ, NA_WIDTH), bat), pl.BlockSpec((1, SEQ, NA_WIDTH), bat),
                   pl.BlockSpec((1, SEQ, GKV_WIDTH), bat), pl.BlockSpec((1, SEQ, GKV_WIDTH), bat)),
        compiler_params=_cparams(("arbitrary",)),
        name="ctx_attention",
    )(zna, zgq, zgkv)


def _lat_na_kernel(zna_ref, ck_ref, cv_ref, tb_ref, att_in_ref, att_ref):
    del att_in_ref
    i = pl.program_id(1)
    start = jnp.clip(i - NA_WIN_ROWS // 2, 0, GRID_H - NA_WIN_ROWS)
    n_loc = NA_WIN_ROWS * GRID_W
    q = zna_ref[pl.ds(pl.multiple_of(i * GRID_W, GRID_W), GRID_W), 0:NA_WIDTH]
    w0 = pl.multiple_of(start * GRID_W, GRID_W)
    kwin = zna_ref[pl.ds(w0, n_loc), NA_WIDTH:2 * NA_WIDTH]
    vwin = zna_ref[pl.ds(w0, n_loc), 2 * NA_WIDTH:3 * NA_WIDTH]
    kc = ck_ref[0]
    vc = cv_ref[0]
    dr0 = start - i + NA_WIN_ROWS - 1
    outs = []
    for h in range(NA_HEADS):
        bias = jnp.concatenate([tb_ref[h, dr0 + r] for r in range(NA_WIN_ROWS)], axis=1)
        qh = _head(q, h)
        s_loc = _bdot(qh, _head(kwin, h), NT_DIMS) * ATT_SCALE + bias
        s_ctx = _bdot(qh, _head(kc, h), NT_DIMS) * ATT_SCALE
        outs.append(_softmax_pv([s_loc, s_ctx], [_head(vwin, h), _head(vc, h)]))
    att_ref[...] = jnp.concatenate(outs, axis=1).astype(BF16)


def _na_bias_table(rpb):
    w = np.arange(GRID_W)[:, None]
    kc = np.arange(GRID_W)[None, :]
    cs = np.clip(w - NA_WIN_COLS // 2, 0, GRID_W - NA_WIN_COLS)
    valid = (kc >= cs) & (kc < cs + NA_WIN_COLS)
    off = np.clip(kc - w + NA_WIN_COLS - 1, 0, 2 * NA_WIN_COLS - 2)
    tb = rpb[:, :, off]
    return jnp.where(jnp.asarray(valid)[None, None], tb, NEG_BIG).astype(F32)


def _lat_na_call(zna, ck, cv, tb, att):
    lat_blk = T_CTX // DEC_SEQ
    first_row = T_CTX // GRID_W
    return pl.pallas_call(
        _lat_na_kernel,
        out_shape=jax.ShapeDtypeStruct((T_ALL, ATT_WIDTH), BF16),
        grid=(DEC_BATCH, GRID_H),
        in_specs=[pl.BlockSpec((DEC_SEQ, NA_PROJ), lambda b, i: (lat_blk + b, 0)),
                  pl.BlockSpec((1, PAST_LEN, NA_WIDTH), lambda b, i: (b, 0, 0)),
                  pl.BlockSpec((1, PAST_LEN, NA_WIDTH), lambda b, i: (b, 0, 0)),
                  pl.BlockSpec((NA_HEADS, 2 * NA_WIN_ROWS - 1, GRID_W, GRID_W), lambda b, i: (0, 0, 0, 0)),
                  pl.BlockSpec(memory_space=pl.ANY)],
        out_specs=pl.BlockSpec((GRID_W, NA_WIDTH),
                               lambda b, i: (first_row + b * GRID_H + i, GQ_WIDTH // NA_WIDTH)),
        input_output_aliases={4: 0},
        compiler_params=_cparams(("arbitrary", "arbitrary")),
        name="latent_neighbourhood_attention",
    )(zna, ck, cv, tb, att)


def _lat_gqa_kernel(zgq_ref, zgkv_ref, ck_ref, cv_ref, att_in_ref, att_ref):
    del att_in_ref
    q = zgq_ref[...]
    kv = zgkv_ref[...]
    kl, vl = kv[:, :GKV_WIDTH], kv[:, GKV_WIDTH:]
    kc = ck_ref[0]
    vc = cv_ref[0]
    for h in range(GQA_Q_HEADS):
        g = h // GQA_GROUP
        qh = _head(q, h)
        s_c = _bdot(qh, _head(kc, g), NT_DIMS) * ATT_SCALE
        s_l = _bdot(qh, _head(kl, g), NT_DIMS) * ATT_SCALE
        o = _softmax_pv([s_c, s_l], [_head(vc, g), _head(vl, g)])
        att_ref[:, h * HEAD_DIM:(h + 1) * HEAD_DIM] = o.astype(BF16)


def _lat_gqa_call(zgq, zgkv, ck, cv, att):
    n_q = DEC_SEQ // TQ_GQA
    first_q = T_CTX // TQ_GQA
    lat_blk = T_CTX // DEC_SEQ
    return pl.pallas_call(
        _lat_gqa_kernel,
        out_shape=jax.ShapeDtypeStruct((T_ALL, ATT_WIDTH), BF16),
        grid=(DEC_BATCH, n_q),
        in_specs=[pl.BlockSpec((TQ_GQA, GQ_WIDTH), lambda b, j: (first_q + b * n_q + j, 0)),
                  pl.BlockSpec((DEC_SEQ, 2 * GKV_WIDTH), lambda b, j: (lat_blk + b, 0)),
                  pl.BlockSpec((1, PAST_LEN, GKV_WIDTH), lambda b, j: (b, 0, 0)),
                  pl.BlockSpec((1, PAST_LEN, GKV_WIDTH), lambda b, j: (b, 0, 0)),
                  pl.BlockSpec(memory_space=pl.ANY)],
        out_specs=pl.BlockSpec((TQ_GQA, GQ_WIDTH), lambda b, j: (first_q + b * n_q + j, 0)),
        input_output_aliases={4: 0},
        compiler_params=_cparams(("arbitrary", "arbitrary")),
        name="latent_gqa_attention",
    )(zgq, zgkv, ck, cv, att)


def _out_proj_kernel(x_ref, yf_ref, yb_ref, bonus_ref, gate_ref, att_ref, mod_ref, lng_ref, lnb_ref,
                     wout_ref, g2_ref, rw_ref, rb_ref,
                     x1_ref, h2_ref, idx_ref, rank_ref, gates_ref, cnt_ref, run_ref):
    i = pl.program_id(0)

    @pl.when(i == 0)
    def _():
        run_ref[...] = jnp.zeros(run_ref.shape, F32)

    ones_bd = _head_ones(RW_WIDTH)
    o = yf_ref[...] + yb_ref[...]
    mu = _head_sum(o, ones_bd) * (1.0 / HEAD_DIM)
    dlt = o - mu
    var = _head_sum(dlt * dlt, ones_bd) * (1.0 / HEAD_DIM)
    ln = dlt * lax.rsqrt(var + LNX_EPS) * lng_ref[...] + lnb_ref[...]
    rw = ((ln + bonus_ref[...]) * gate_ref[...]).astype(BF16)
    att = att_ref[...]
    mix = (jnp.dot(rw, wout_ref[0:RW_WIDTH, :], preferred_element_type=F32)
           + jnp.dot(att[:, :GQ_WIDTH], wout_ref[RW_WIDTH + NA_WIDTH:, :], preferred_element_type=F32)
           + jnp.dot(att[:, GQ_WIDTH:], wout_ref[RW_WIDTH:RW_WIDTH + NA_WIDTH, :],
                     preferred_element_type=F32))
    gate1 = mod_ref[0, 2:3, :]
    shift2 = mod_ref[0, 3:4, :]
    scale2 = mod_ref[0, 4:5, :]
    x1 = x_ref[...] + gate1 * mix
    x1_ref[...] = x1
    h2 = _rms(x1, g2_ref[...]) * (1.0 + scale2) + shift2
    h2_ref[...] = h2

    logits = _dot3(h2, rw_ref[...]) + rb_ref[...]
    tm = logits.shape[0]
    col = lax.broadcasted_iota(jnp.int32, (tm, N_EXPERTS), 1)
    lane4 = lax.broadcasted_iota(jnp.int32, (tm, TOP_K), 1)
    work = logits
    sels, vals = [], []
    idx_out = jnp.zeros((tm, TOP_K), jnp.int32)
    for k in range(TOP_K):
        m = work.max(axis=-1, keepdims=True)
        idx = jnp.min(jnp.where(work == m, col, N_EXPERTS), axis=-1, keepdims=True)
        sel = col == idx
        sels.append(sel)
        vals.append(m)
        idx_out = jnp.where(lane4 == k, idx, idx_out)
        work = jnp.where(sel, -jnp.inf, work)
    es = [jnp.exp(v - vals[0]) for v in vals]
    inv = 1.0 / (es[0] + es[1] + es[2] + es[3])
    gates = jnp.zeros((tm, TOP_K), F32)
    for k in range(TOP_K):
        gates = jnp.where(lane4 == k, es[k] * inv, gates)
    assign = jnp.zeros((tm, N_EXPERTS), F32)
    for sel in sels:
        assign = assign + sel.astype(F32)
    r_i = lax.broadcasted_iota(jnp.int32, (tm, tm), 0)
    c_i = lax.broadcasted_iota(jnp.int32, (tm, tm), 1)
    before = jnp.dot((c_i < r_i).astype(BF16), assign.astype(BF16), preferred_element_type=F32)
    pos = before + run_ref[...]
    rank = jnp.zeros((tm, TOP_K), F32)
    for k in range(TOP_K):
        rk = jnp.sum(jnp.where(sels[k], pos, 0.0), axis=-1, keepdims=True)
        rank = jnp.where(lane4 == k, rk, rank)
    run_ref[...] = run_ref[...] + jnp.sum(assign, axis=0, keepdims=True)
    idx_ref[...] = idx_out
    rank_ref[...] = rank.astype(jnp.int32)
    gates_ref[...] = gates
    cnt_ref[...] = run_ref[...].astype(jnp.int32)


def _out_proj_call(x, yf, yb, bonus, gate, att, mods_l, lp):
    n_blk = T_ALL // TM_TOK
    row = lambda i: (i, 0)
    full2 = lambda r, c: pl.BlockSpec((r, c), lambda i: (0, 0))
    tokw = lambda w: pl.BlockSpec((TM_TOK, w), row)
    return pl.pallas_call(
        _out_proj_kernel,
        out_shape=(jax.ShapeDtypeStruct((T_ALL, D_MODEL), F32),
                   jax.ShapeDtypeStruct((T_ALL, D_MODEL), F32),
                   jax.ShapeDtypeStruct((T_ALL, TOP_K), jnp.int32),
                   jax.ShapeDtypeStruct((T_ALL, TOP_K), jnp.int32),
                   jax.ShapeDtypeStruct((T_ALL, TOP_K), F32),
                   jax.ShapeDtypeStruct((1, N_EXPERTS), jnp.int32)),
        grid=(n_blk,),
        in_specs=[tokw(D_MODEL), tokw(RW_WIDTH), tokw(RW_WIDTH), tokw(RW_WIDTH), tokw(RW_WIDTH),
                  tokw(ATT_WIDTH),
                  pl.BlockSpec((1, N_MOD, D_MODEL), lambda i: (_group_of_block(i, TM_TOK), 0, 0)),
                  full2(1, RW_WIDTH), full2(1, RW_WIDTH), full2(D_MODEL, D_MODEL), full2(1, D_MODEL),
                  full2(D_MODEL, N_EXPERTS), full2(1, N_EXPERTS)],
        out_specs=(tokw(D_MODEL), tokw(D_MODEL), tokw(TOP_K), tokw(TOP_K), tokw(TOP_K),
                   full2(1, N_EXPERTS)),
        scratch_shapes=[pltpu.VMEM((1, N_EXPERTS), F32)],
        compiler_params=_cparams(("arbitrary",)),
        name="out_proj_router",
    )(x, yf, yb, bonus, gate, att, mods_l, lp["rw_ln_g"], lp["rw_ln_b"], lp["w_out"], lp["norm2_g"],
      lp["router_w"], lp["router_b"])


def _expert_kernel(be_ref, meta_ref, slot_tok_ref, h2_hbm, wgu_ref, bgu_ref, wd_ref, bd_ref,
                   y_ref, xbuf, sem, wgu_bf, wd_bf):
    i = pl.program_id(0)
    n_used = meta_ref[0]
    slot = i % 2

    def row_copy(blk, r, buf_slot):
        tok = slot_tok_ref[blk * TM_EXP + r]
        return pltpu.make_async_copy(h2_hbm.at[pl.ds(tok, 1), :], xbuf.at[buf_slot, pl.ds(r, 1), :],
                                     sem.at[buf_slot])

    def wait_tile(buf_slot):
        pltpu.make_async_copy(h2_hbm.at[pl.ds(0, TM_EXP), :], xbuf.at[buf_slot], sem.at[buf_slot]).wait()

    @pl.when(i == 0)
    def _():
        def body(r, carry):
            row_copy(0, r, 0).start()
            return carry
        lax.fori_loop(0, TM_EXP, body, 0, unroll=8)

    @pl.when(i < n_used)
    def _():
        wait_tile(slot)
        e = be_ref[i]
        e_prev = be_ref[jnp.maximum(i - 1, 0)]

        @pl.when((i == 0) | (e != e_prev))
        def _():
            wgu_bf[...] = wgu_ref[0, 0].astype(BF16)
            wd_bf[...] = wd_ref[0, 0].astype(BF16)

        x = xbuf[slot].astype(BF16)
        nxt = jnp.minimum(i + 1, n_used - 1)
        for r in range(TM_EXP):
            row_copy(nxt, r, 1 - slot).start()
        gu = jnp.dot(x, wgu_bf[...], preferred_element_type=F32) + bgu_ref[0, 0]
        glu = jnp.minimum(gu[:, :D_FF], SWIGLU_LIMIT)
        lin = jnp.clip(gu[:, D_FF:], -SWIGLU_LIMIT, SWIGLU_LIMIT)
        act = glu * _sigmoid(SWIGLU_ALPHA * glu) * (lin + 1.0)
        y_ref[...] = jnp.dot(act.astype(BF16), wd_bf[...], preferred_element_type=F32) + bd_ref[0, 0]

        @pl.when(i == n_used - 1)
        def _():
            wait_tile(1 - slot)


def _expert_call(block_e, meta, slot_tok, h2, w_gu, b_gu, w_down, b_down, layer):
    def wmap(i, be, meta_, st):
        return (layer, be[i], 0, 0)

    def omap(i, be, meta_, st):
        return (jnp.minimum(i, meta_[0] - 1), 0)

    grid_spec = pltpu.PrefetchScalarGridSpec(
        num_scalar_prefetch=3,
        grid=(N_SLOT_BLOCKS,),
        in_specs=[pl.BlockSpec(memory_space=pl.ANY),
                  pl.BlockSpec((1, 1, D_MODEL, 2 * D_FF), wmap),
                  pl.BlockSpec((1, 1, 1, 2 * D_FF), wmap),
                  pl.BlockSpec((1, 1, D_FF, D_MODEL), wmap),
                  pl.BlockSpec((1, 1, 1, D_MODEL), wmap)],
        out_specs=pl.BlockSpec((TM_EXP, D_MODEL), omap),
        scratch_shapes=[pltpu.VMEM((2, TM_EXP, D_MODEL), F32), pltpu.SemaphoreType.DMA((2,)),
                        pltpu.VMEM((D_MODEL, 2 * D_FF), BF16), pltpu.VMEM((D_FF, D_MODEL), BF16)],
    )
    return pl.pallas_call(
        _expert_kernel,
        out_shape=jax.ShapeDtypeStruct((N_SLOTS, D_MODEL), F32),
        grid_spec=grid_spec,
        compiler_params=_cparams(("arbitrary",)),
        name="moe_experts",
    )(block_e, meta, slot_tok, h2, w_gu, b_gu.reshape(DEPTH, N_EXPERTS, 1, 2 * D_FF), w_down,
      b_down.reshape(DEPTH, N_EXPERTS, 1, D_MODEL))


def _combine_kernel(dest_ref, ys_hbm, x1_ref, gates_ref, mod_ref, fg_ref, o_ref, ybuf, sem, *, final):
    i = pl.program_id(0)
    n = pl.num_programs(0)
    slot = i % 2

    def gather(blk, buf_slot):
        def body(r, carry):
            for k in range(TOP_K):
                d = dest_ref[(blk * TM_CMB + r) * TOP_K + k]
                pltpu.make_async_copy(ys_hbm.at[pl.ds(d, 1), :], ybuf.at[buf_slot, k, pl.ds(r, 1), :],
                                      sem.at[buf_slot]).start()
            return carry
        lax.fori_loop(0, TM_CMB, body, 0, unroll=4)

    @pl.when(i == 0)
    def _():
        gather(0, 0)

    @pl.when(i + 1 < n)
    def _():
        gather(i + 1, 1 - slot)

    for k in range(TOP_K):
        pltpu.make_async_copy(ys_hbm.at[pl.ds(0, TM_CMB), :], ybuf.at[slot, k], sem.at[slot]).wait()
    gates = gates_ref[...]
    ff = gates[:, 0:1] * ybuf[slot, 0]
    for k in range(1, TOP_K):
        ff = ff + gates[:, k:k + 1] * ybuf[slot, k]
    x = x1_ref[...] + mod_ref[0, 5:6, :] * ff
    if final:
        x = _rms(x, fg_ref[...])
    o_ref[...] = x


def _combine_call(dest, ys, x1, gates, mods_l, final_g, final):
    n_blk = T_ALL // TM_CMB
    row = lambda i, d: (i, 0)
    grid_spec = pltpu.PrefetchScalarGridSpec(
        num_scalar_prefetch=1,
        grid=(n_blk,),
        in_specs=[pl.BlockSpec(memory_space=pl.ANY),
                  pl.BlockSpec((TM_CMB, D_MODEL), row),
                  pl.BlockSpec((TM_CMB, TOP_K), row),
                  pl.BlockSpec((1, N_MOD, D_MODEL), lambda i, d: (_group_of_block(i, TM_CMB), 0, 0)),
                  pl.BlockSpec((1, D_MODEL), lambda i, d: (0, 0))],
        out_specs=pl.BlockSpec((TM_CMB, D_MODEL), row),
        scratch_shapes=[pltpu.VMEM((2, TOP_K, TM_CMB, D_MODEL), F32), pltpu.SemaphoreType.DMA((2,))],
    )
    return pl.pallas_call(
        functools.partial(_combine_kernel, final=final),
        out_shape=jax.ShapeDtypeStruct((T_ALL, D_MODEL), F32),
        grid_spec=grid_spec,
        compiler_params=_cparams(("arbitrary",)),
        name="moe_combine_final" if final else "moe_combine",
    )(dest, ys, x1, gates, mods_l, final_g)


def _routing_tables(idx, rank, counts):
    counts = counts.reshape(N_EXPERTS)
    padded = (counts + TM_EXP - 1) // TM_EXP * TM_EXP
    pad_end = jnp.cumsum(padded)
    pad_start = pad_end - padded
    dest = (pad_start[idx] + rank).reshape(-1).astype(jnp.int32)
    tok = jnp.arange(T_ALL * TOP_K, dtype=jnp.int32) // TOP_K
    slot_tok = jnp.zeros((N_SLOTS,), jnp.int32).at[dest].set(tok)
    n_used = (pad_end[-1] // TM_EXP).astype(jnp.int32)
    blk = jnp.minimum(jnp.arange(N_SLOT_BLOCKS, dtype=jnp.int32), n_used - 1)
    block_e = jnp.sum((pad_end[None, :] <= (blk * TM_EXP)[:, None]).astype(jnp.int32), axis=1)
    block_e = jnp.minimum(block_e, N_EXPERTS - 1)
    return dest, slot_tok, block_e, n_used.reshape(1)


def kernel(x_prompt, x_sample, cache_na_k, cache_na_v, cache_gqa_k, cache_gqa_v, state_rwkv, c, c_ctx,
           w_mod, b_mod, norm1_g, norm2_g, w_in, rw_shift, rw_w0, rw_w_up, rw_a0, rw_a_up, rw_g_up,
           rw_k_k, rw_k_a, rw_r_k, rw_ln_g, rw_ln_b, na_rpb, q_norm, k_norm, w_out, router_w, router_b,
           moe_w_gu, moe_b_gu, moe_w_down, moe_b_down, final_norm_g):
    x = jnp.concatenate([x_prompt.reshape(T_CTX, D_MODEL), x_sample.reshape(T_LAT, D_MODEL)], axis=0)
    cvecs = jnp.concatenate([c_ctx[None, :], c, jnp.zeros((SUBLANES - N_GROUPS, D_MODEL), F32)], axis=0)
    mods = _mods_call(cvecs, w_mod, b_mod)
    mods = mods[:, :N_GROUPS].reshape(DEPTH, N_GROUPS, N_MOD, D_MODEL)
    rope = _rope_tables()
    w_in_bf = w_in.astype(BF16)
    w_out_bf = w_out.astype(BF16)
    rw_g_up_bf = rw_g_up.astype(BF16)
    final_g = final_norm_g.reshape(1, D_MODEL)

    na_k_l, na_v_l, g_k_l, g_v_l, st_l = [], [], [], [], []
    for l in range(DEPTH):
        lp = {
            "rw_shift": rw_shift[l], "rw_k_k": rw_k_k[l].reshape(1, RW_WIDTH),
            "rw_k_a": rw_k_a[l].reshape(1, RW_WIDTH), "rw_r_k": rw_r_k[l].reshape(1, RW_WIDTH),
            "rw_w0": rw_w0[l], "rw_w_up": rw_w_up[l], "rw_a0": rw_a0[l], "rw_a_up": rw_a_up[l],
            "rw_g_up": rw_g_up_bf[l], "rw_ln_g": rw_ln_g[l].reshape(1, RW_WIDTH),
            "rw_ln_b": rw_ln_b[l].reshape(1, RW_WIDTH), "w_out": w_out_bf[l],
            "norm2_g": norm2_g[l].reshape(1, D_MODEL), "router_w": router_w[l],
            "router_b": router_b[l].reshape(1, N_EXPERTS),
        }
        qk_g = jnp.concatenate([jnp.tile(q_norm[l], GQA_Q_HEADS), jnp.tile(k_norm[l], GQA_KV_HEADS)])
        zrw, zna, zgq, zgkv = _in_proj_call(x, mods[l], norm1_g[l].reshape(1, D_MODEL), w_in_bf[l],
                                            qk_g.reshape(1, GQ_WIDTH + GKV_WIDTH), rope)

        s0_lat = state_rwkv[:, l]
        rw_ctx = _rwkv_call(zrw, s0_lat, lp, (), n_seq=BATCH, seq_len=SEQ, row_base=0,
                            has_init=False, emit_state=True)
        yf, yb, bonus, gate = _rwkv_call(zrw, s0_lat, lp, rw_ctx[:4], n_seq=DEC_BATCH, seq_len=DEC_SEQ,
                                         row_base=T_CTX, has_init=True, emit_state=False)
        st_l.append(rw_ctx[4])

        att, nk, nv, gk, gv = _ctx_attn_call(zna, zgq, zgkv)
        na_k_l.append(nk)
        na_v_l.append(nv)
        g_k_l.append(gk)
        g_v_l.append(gv)
        att = _lat_na_call(zna, cache_na_k[:, l].reshape(DEC_BATCH, PAST_LEN, NA_WIDTH),
                           cache_na_v[:, l].reshape(DEC_BATCH, PAST_LEN, NA_WIDTH),
                           _na_bias_table(na_rpb[l]), att)
        att = _lat_gqa_call(zgq, zgkv, cache_gqa_k[:, l].reshape(DEC_BATCH, PAST_LEN, GKV_WIDTH),
                            cache_gqa_v[:, l].reshape(DEC_BATCH, PAST_LEN, GKV_WIDTH), att)

        x1, h2, idx, rank, gates, counts = _out_proj_call(x, yf, yb, bonus, gate, att, mods[l], lp)
        dest, slot_tok, block_e, n_used = _routing_tables(idx, rank, counts)
        ys = _expert_call(block_e, n_used, slot_tok, h2, moe_w_gu, moe_b_gu, moe_w_down, moe_b_down, l)
        x = _combine_call(dest, ys, x1, gates, mods[l], final_g, l == DEPTH - 1)

    y_prompt = x[:T_CTX].reshape(BATCH, SEQ, D_MODEL)
    y_sample = x[T_CTX:].reshape(DEC_BATCH, DEC_SEQ, D_MODEL)
    heads = lambda ts, n: jnp.stack(ts, axis=1).reshape(BATCH, DEPTH, SEQ, n, HEAD_DIM)
    return (y_prompt, y_sample, heads(na_k_l, NA_HEADS), heads(na_v_l, NA_HEADS),
            heads(g_k_l, GQA_KV_HEADS), heads(g_v_l, GQA_KV_HEADS), jnp.stack(st_l, axis=1))
```

```python
import functools

import numpy as np
import jax
import jax.numpy as jnp
from jax import lax
from jax.experimental import pallas as pl
from jax.experimental.pallas import tpu as pltpu

F32 = jnp.float32
BF16 = jnp.bfloat16

D_MODEL = 1024
BATCH = 32
SEQ = 256
DEPTH = 4
DEC_BATCH = 2
DEC_SEQ = 2048
PAST_LEN = 512
GRID_W = 64
GRID_H = DEC_SEQ // GRID_W
HEAD_DIM = 64
RW_HEADS = 4
RW_WIDTH = RW_HEADS * HEAD_DIM
DECAY_LORA = 64
ICLR_LORA = 64
GATE_LORA = 128
NA_HEADS = 4
NA_WIDTH = NA_HEADS * HEAD_DIM
NA_WIN_ROWS = 8
NA_WIN_COLS = 16
GQA_Q_HEADS = 8
GQA_KV_HEADS = 2
GQA_GROUP = GQA_Q_HEADS // GQA_KV_HEADS
GQ_WIDTH = GQA_Q_HEADS * HEAD_DIM
GKV_WIDTH = GQA_KV_HEADS * HEAD_DIM
RW_PROJ = 3 * RW_WIDTH + DECAY_LORA + ICLR_LORA + GATE_LORA
NA_PROJ = 3 * NA_WIDTH
GQA_PROJ = GQ_WIDTH + 2 * GKV_WIDTH
IN_PROJ = RW_PROJ + NA_PROJ + GQA_PROJ
ATT_WIDTH = GQ_WIDTH + NA_WIDTH
N_EXPERTS = 32
TOP_K = 4
D_FF = D_MODEL
SWIGLU_LIMIT = 7.0
SWIGLU_ALPHA = 1.702
ROPE_THETA = 10000.0
NORM_EPS = 1e-6
LNX_EPS = 64e-5
N_MOD = 6
ATT_SCALE = HEAD_DIM ** -0.5
NEG_BIG = -1e30

T_CTX = BATCH * SEQ
T_LAT = DEC_BATCH * DEC_SEQ
T_ALL = T_CTX + T_LAT
N_GROUPS = 1 + DEC_BATCH

LANES = 128
SUBLANES = 8
TM_TOK = 512
CHUNK = 64
RW_SEQ_GROUP = 2
TQ_GQA = 256
TM_EXP = 256
SEG_ALIGN = 16
SEG_MAX_BIT = (TM_TOK // SEG_ALIGN).bit_length() - 1
TAIL_MAX_BIT = (TM_EXP // SEG_ALIGN - 1).bit_length() - 1
LOCAL_ROWS = -(-(TOP_K * TM_TOK + N_EXPERTS * (SEG_ALIGN - 1)) // TM_TOK) * TM_TOK
N_SLOT_BLOCKS = (-(-(T_ALL * TOP_K + (T_ALL // TM_TOK) * N_EXPERTS * (SEG_ALIGN - 1)) // TM_EXP)
                 + N_EXPERTS)
N_SLOTS = N_SLOT_BLOCKS * TM_EXP
VMEM_LIMIT = 56 * 1024 * 1024

NT_DIMS = (((1,), (1,)), ((), ()))
TN_DIMS = (((0,), (0,)), ((), ()))


def _bdot(a, b, dims=None):
    a = a.astype(BF16)
    b = b.astype(BF16)
    if dims is None:
        return jnp.dot(a, b, preferred_element_type=F32)
    return lax.dot_general(a, b, dims, preferred_element_type=F32)


def _split(a):
    hi = a.astype(BF16)
    lo = (a - hi.astype(F32)).astype(BF16)
    return hi, lo


def _dot3(a, b, dims=None):
    ah, al = _split(a)
    bh, bl = _split(b)
    return _bdot(ah, bh, dims) + _bdot(ah, bl, dims) + _bdot(al, bh, dims)


def _dot_exact_lhs(a_exact, b):
    h1 = b.astype(BF16)
    r1 = b - h1.astype(F32)
    h2 = r1.astype(BF16)
    h3 = (r1 - h2.astype(F32)).astype(BF16)
    return _bdot(a_exact, h1) + _bdot(a_exact, h2) + _bdot(a_exact, h3)


def _head_ones(n):
    r = lax.broadcasted_iota(jnp.int32, (n, n), 0) // HEAD_DIM
    c = lax.broadcasted_iota(jnp.int32, (n, n), 1) // HEAD_DIM
    return (r == c).astype(BF16)


def _head_sum(x, ones_bd):
    hi, lo = _split(x)
    return (jnp.dot(hi, ones_bd, preferred_element_type=F32)
            + jnp.dot(lo, ones_bd, preferred_element_type=F32))


def _sigmoid(x):
    return 1.0 / (1.0 + jnp.exp(-x))


def _cparams(sem):
    return pltpu.CompilerParams(dimension_semantics=sem, vmem_limit_bytes=VMEM_LIMIT)


def _group_of_block(i, rows_per_block):
    first_lat = T_CTX // rows_per_block
    per_sample = DEC_SEQ // rows_per_block
    return jnp.where(i < first_lat, 0, 1 + (i - first_lat) // per_sample)


def _mods_kernel(c_ref, w_ref, b_ref, o_ref):
    c = c_ref[...]
    s = c * _sigmoid(c)
    o_ref[0] = _dot3(s, w_ref[0]) + b_ref[0]


def _mods_call(cvecs, w_mod, b_mod):
    tn = 1536
    n_rows = cvecs.shape[0]
    return pl.pallas_call(
        _mods_kernel,
        out_shape=jax.ShapeDtypeStruct((DEPTH, n_rows, N_MOD * D_MODEL), F32),
        grid=(DEPTH, N_MOD * D_MODEL // tn),
        in_specs=[
            pl.BlockSpec((n_rows, D_MODEL), lambda l, j: (0, 0)),
            pl.BlockSpec((1, D_MODEL, tn), lambda l, j: (l, 0, j)),
            pl.BlockSpec((1, 1, tn), lambda l, j: (l, 0, j)),
        ],
        out_specs=pl.BlockSpec((1, n_rows, tn), lambda l, j: (l, 0, j)),
        compiler_params=_cparams(("arbitrary", "arbitrary")),
        name="adaln_mods",
    )(cvecs, w_mod, b_mod.reshape(DEPTH, 1, N_MOD * D_MODEL))


def _rms(x, g):
    ms = jnp.mean(x * x, axis=-1, keepdims=True)
    return x * lax.rsqrt(ms + NORM_EPS) * g


def _in_proj_kernel(x_ref, mod_ref, g_ref, w_ref, qkg_ref, rc_ref, rs1_ref, rs2_ref,
                    zrw_ref, zna_ref, zgq_ref, zgkv_ref):
    x = x_ref[...]
    shift1 = mod_ref[0, 0:1, :]
    scale1 = mod_ref[0, 1:2, :]
    h = _rms(x, g_ref[...]) * (1.0 + scale1) + shift1
    z = jnp.dot(h.astype(BF16), w_ref[...], preferred_element_type=F32)
    zrw_ref[...] = z[:, :RW_PROJ]
    zna_ref[...] = z[:, RW_PROJ:RW_PROJ + NA_PROJ]
    qk_w = GQ_WIDTH + GKV_WIDTH
    qk = z[:, RW_PROJ + NA_PROJ:RW_PROJ + NA_PROJ + qk_w]
    ones_bd = _head_ones(LANES)
    sq = qk * qk
    ssq = jnp.concatenate(
        [_head_sum(sq[:, j * LANES:(j + 1) * LANES], ones_bd) for j in range(qk_w // LANES)], axis=1)
    qkn = qk * lax.rsqrt(ssq * (1.0 / HEAD_DIM) + NORM_EPS) * qkg_ref[...]
    reps = qk_w // LANES
    rc = jnp.concatenate([rc_ref[...]] * reps, axis=1)
    rs1 = jnp.concatenate([rs1_ref[...]] * reps, axis=1)
    rs2 = jnp.concatenate([rs2_ref[...]] * reps, axis=1)
    half = HEAD_DIM // 4
    qkr = qkn * rc + pltpu.roll(qkn, half, 1) * rs1 + pltpu.roll(qkn, qk_w - half, 1) * rs2
    zgq_ref[...] = qkr[:, :GQ_WIDTH]
    zgkv_ref[:, :GKV_WIDTH] = qkr[:, GQ_WIDTH:]
    zgkv_ref[:, GKV_WIDTH:] = z[:, RW_PROJ + NA_PROJ + qk_w:]


def _rope_tables():
    t = np.arange(DEC_SEQ)
    pos = np.stack([t // GRID_W, t % GRID_W], axis=1).astype(np.float32)
    axis_dim = HEAD_DIM // 2
    inv = ROPE_THETA ** (-np.arange(0, axis_dim, 2, dtype=np.float32) / axis_dim)
    d = np.arange(LANES) % HEAD_DIM
    part = d // axis_dim
    within = d % axis_dim
    freq = within % (axis_dim // 2)
    second = within // (axis_dim // 2)
    ang = jnp.asarray(pos)[:, part] * jnp.asarray(inv)[freq][None, :]
    cos = jnp.cos(ang)
    sin = jnp.sin(ang)
    s1 = jnp.where(second[None, :] == 1, sin, 0.0)
    s2 = jnp.where(second[None, :] == 0, -sin, 0.0)
    ident = jnp.ones((TM_TOK, LANES), F32)
    zero = jnp.zeros((TM_TOK, LANES), F32)
    return (jnp.concatenate([cos, ident], 0), jnp.concatenate([s1, zero], 0),
            jnp.concatenate([s2, zero], 0))


def _in_proj_call(x, mods_l, g1, w_in_bf, qk_g, rope):
    n_blk = T_ALL // TM_TOK
    lat_blk = DEC_SEQ // TM_TOK
    first_lat = T_CTX // TM_TOK

    def rope_idx(i):
        return (jnp.where(i < first_lat, lat_blk, (i - first_lat) % lat_blk), 0)

    row = lambda i: (i, 0)
    rope_spec = pl.BlockSpec((TM_TOK, LANES), rope_idx)
    return pl.pallas_call(
        _in_proj_kernel,
        out_shape=(jax.ShapeDtypeStruct((T_ALL, RW_PROJ), F32),
                   jax.ShapeDtypeStruct((T_ALL, NA_PROJ), F32),
                   jax.ShapeDtypeStruct((T_ALL, GQ_WIDTH), F32),
                   jax.ShapeDtypeStruct((T_ALL, 2 * GKV_WIDTH), F32)),
        grid=(n_blk,),
        in_specs=[
            pl.BlockSpec((TM_TOK, D_MODEL), row),
            pl.BlockSpec((1, N_MOD, D_MODEL), lambda i: (_group_of_block(i, TM_TOK), 0, 0)),
            pl.BlockSpec((1, D_MODEL), lambda i: (0, 0)),
            pl.BlockSpec((D_MODEL, IN_PROJ), lambda i: (0, 0)),
            pl.BlockSpec((1, GQ_WIDTH + GKV_WIDTH), lambda i: (0, 0)),
            rope_spec, rope_spec, rope_spec,
        ],
        out_specs=(pl.BlockSpec((TM_TOK, RW_PROJ), row), pl.BlockSpec((TM_TOK, NA_PROJ), row),
                   pl.BlockSpec((TM_TOK, GQ_WIDTH), row), pl.BlockSpec((TM_TOK, 2 * GKV_WIDTH), row)),
        compiler_params=_cparams(("arbitrary",)),
        name="in_proj",
    )(x, mods_l, g1, w_in_bf, qk_g, *rope)


def _softplus(x):
    return jnp.maximum(x, 0.0) + jnp.log(1.0 + jnp.exp(-jnp.abs(x)))


def _rw_pre(z, zprev, znext, shift_ref, kk_ref, ka_ref, rk_ref, w0_ref, wup_ref, a0_ref, aup_ref, d,
            ones_bd):
    rows = lax.broadcasted_iota(jnp.int32, z.shape, 0)
    zp = jnp.where(rows == 0, zprev, pltpu.roll(z, 1, 0))
    zn = jnp.where(rows == CHUNK - 1, znext, pltpu.roll(z, CHUNK - 1, 0))
    zs = zp * shift_ref[0:1, :] + z * shift_ref[1:2, :] + zn * shift_ref[2:3, :]
    r = zs[:, 0:RW_WIDTH]
    k = zs[:, RW_WIDTH:2 * RW_WIDTH]
    v = zs[:, 2 * RW_WIDTH:3 * RW_WIDTH]
    o = 3 * RW_WIDTH
    wd = zs[:, o:o + DECAY_LORA]
    ad = zs[:, o + DECAY_LORA:o + DECAY_LORA + ICLR_LORA]
    gd = zs[:, o + DECAY_LORA + ICLR_LORA:]
    kk = k * kk_ref[...]
    kk = kk / jnp.maximum(jnp.sqrt(_head_sum(kk * kk, ones_bd)), 1e-12)
    tw = jnp.tanh(wd)
    wl = w0_ref[d:d + 1, :] + _dot3(tw, wup_ref[d])
    lw = -jnp.exp(-_softplus(-wl) - 0.5)
    a_sig = _sigmoid(a0_ref[d:d + 1, :] + _dot3(ad, aup_ref[d]))
    k_d = k * (1.0 + (a_sig - 1.0) * ka_ref[...])
    bonus = _head_sum(r * k_d * rk_ref[...], ones_bd) * v
    return dict(r=r, k=k_d, v=v, a=-kk, b=kk * a_sig, lw=lw, bonus=bonus, gd=gd, ad=ad, k_raw=k)


def _chunk_masks(rev):
    t = lax.broadcasted_iota(jnp.int32, (CHUNK, CHUNK), 0)
    j = lax.broadcasted_iota(jnp.int32, (CHUNK, CHUNK), 1)
    return ((j >= t), (j > t)) if rev else ((j <= t), (j < t))


def _wkv_scale(p, incl, rev):
    lw = p["lw"]
    cs = _dot_exact_lhs(incl.astype(BF16), lw)
    tot = cs[0:1, :] if rev else cs[CHUNK - 1:CHUNK, :]
    e_inv = jnp.exp(-cs)
    e_rem = jnp.exp(tot - cs)
    bf = lambda x: x.astype(BF16)
    return dict(at=bf(p["a"] * jnp.exp(cs - lw)), rt=bf(p["r"] * jnp.exp(cs)),
                bt=bf(p["b"] * e_inv), kt=bf(p["k"] * e_inv),
                bh=bf(p["b"] * e_rem), kh=bf(p["k"] * e_rem), v=bf(p["v"]), gtot=jnp.exp(tot))


def _wkv_chunks(items):
    bf = lambda x: x.astype(BF16)
    c = CHUNK
    n_sq = int(np.log2(c))
    ar = [jnp.concatenate([it["at"], it["rt"]], 0) for it in items]
    m_b = [_bdot(a, it["bt"], NT_DIMS) for a, it in zip(ar, items)]
    m_k = [_bdot(a, it["kt"], NT_DIMS) for a, it in zip(ar, items)]
    s_bf = [bf(it["s"]) for it in items]
    xs = [bf(jnp.where(it["strict"], m[:c], 0.0)) for m, it in zip(m_b, items)]
    a_rb = [bf(jnp.where(it["incl"], m[c:], 0.0)) for m, it in zip(m_b, items)]
    a_ak = [bf(jnp.where(it["strict"], m[:c], 0.0)) for m, it in zip(m_k, items)]
    a_rk = [bf(jnp.where(it["incl"], m[c:], 0.0)) for m, it in zip(m_k, items)]
    ws = [_bdot(ak, it["v"]) + _bdot(it["at"], s, NT_DIMS) for ak, it, s in zip(a_ak, items, s_bf)]
    for step in range(n_sq):
        wb = [bf(w) for w in ws]
        ws = [w + _bdot(x, b) for w, x, b in zip(ws, xs, wb)]
        if step < n_sq - 1:
            xs = [bf(_bdot(x, x)) for x in xs]
    wb = [bf(w) for w in ws]
    ys = [_bdot(rb, w) + _bdot(rk, it["v"]) + _bdot(it["rt"], s, NT_DIMS)
          for rb, rk, w, it, s in zip(a_rb, a_rk, wb, items, s_bf)]
    s_new = [it["s"] * it["gtot"] + _bdot(w, it["bh"], TN_DIMS) + _bdot(it["v"], it["kh"], TN_DIMS)
             for w, it in zip(wb, items)]
    return ys, s_new


def _rwkv_kernel(zf_ref, zfp_ref, zfn_ref, zb_ref, zbp_ref, zbn_ref, s0_ref,
                 shift_ref, kk_ref, ka_ref, rk_ref, w0_ref, wup_ref, a0_ref, aup_ref, gup_ref,
                 *refs, n_chunks, n_group, has_init, emit_state, n_alias):
    refs = refs[n_alias:]
    if emit_state:
        yf_ref, yb_ref, bonus_ref, gate_ref, st_ref, h_ref = refs
    else:
        yf_ref, yb_ref, bonus_ref, gate_ref, h_ref = refs
        st_ref = None
    i = pl.program_id(1)
    ones_bd = _head_ones(RW_WIDTH)

    @pl.when(i == 0)
    def _():
        if has_init:
            h_ref[...] = s0_ref[...]
        else:
            h_ref[...] = jnp.zeros(h_ref.shape, F32)

    params = (shift_ref, kk_ref, ka_ref, rk_ref, w0_ref, wup_ref, a0_ref, aup_ref)
    first = i == 0
    last = i == n_chunks - 1
    zero_row = jnp.zeros((1, RW_PROJ), F32)
    masks = (_chunk_masks(False), _chunk_masks(True))
    items = []
    for g in range(n_group):
        pf = _rw_pre(zf_ref[g], jnp.where(first, zero_row, zfp_ref[g, 0, SUBLANES - 1:SUBLANES, :]),
                     jnp.where(last, zero_row, zfn_ref[g, 0, 0:1, :]), *params, 0, ones_bd)
        pb = _rw_pre(zb_ref[g], jnp.where(last, zero_row, zbp_ref[g, 0, SUBLANES - 1:SUBLANES, :]),
                     jnp.where(first, zero_row, zbn_ref[g, 0, 0:1, :]), *params, 1, ones_bd)
        a_sig_b = _sigmoid(a0_ref[1:2, :] + _dot3(pf["ad"], aup_ref[1]))
        k_b = pf["k_raw"] * (1.0 + (a_sig_b - 1.0) * ka_ref[...])
        bonus_ref[g] = pf["bonus"] + _head_sum(pf["r"] * k_b * rk_ref[...], ones_bd) * pf["v"]
        gate_ref[g] = _bdot(_sigmoid(pf["gd"]), gup_ref[...])
        for d, p in ((0, pf), (1, pb)):
            incl, strict = masks[d]
            sc = _wkv_scale(p, incl, d == 1)
            for h in range(RW_HEADS):
                sl = slice(h * HEAD_DIM, (h + 1) * HEAD_DIM)
                it = {k: v[:, sl] for k, v in sc.items()}
                it.update(s=h_ref[g, d, h], incl=incl, strict=strict, where=(g, d, h))
                items.append(it)

    ys, s_new = _wkv_chunks(items)
    for it, y, s in zip(items, ys, s_new):
        g, d, h = it["where"]
        y_ref = yb_ref if d else yf_ref
        y_ref[g, :, h * HEAD_DIM:(h + 1) * HEAD_DIM] = y
        h_ref[g, d, h] = s

    if emit_state:
        @pl.when(last)
        def _():
            for it, s in zip(items, s_new):
                g, d, h = it["where"]
                st_ref[g, d, h] = s


def _rwkv_call(zrw, s0, lp, prev_outs, *, n_seq, seq_len, row_base, has_init, emit_state):
    g = RW_SEQ_GROUP
    n_chunks = seq_len // CHUNK
    n_rows8 = seq_len // SUBLANES
    per8 = CHUNK // SUBLANES
    n_view = T_ALL // seq_len
    base_g = row_base // seq_len // g
    z3 = zrw.reshape(n_view, seq_len, RW_PROJ)
    z4 = zrw.reshape(n_view, n_rows8, SUBLANES, RW_PROJ)

    fwd = lambda i: i
    bwd = lambda i: n_chunks - 1 - i
    main = lambda c: pl.BlockSpec((g, CHUNK, RW_PROJ), lambda b, i: (base_g + b, c(i), 0))
    prev8 = lambda c: pl.BlockSpec((g, 1, SUBLANES, RW_PROJ),
                                   lambda b, i: (base_g + b, jnp.maximum(c(i) * per8 - 1, 0), 0, 0))
    next8 = lambda c: pl.BlockSpec((g, 1, SUBLANES, RW_PROJ),
                                   lambda b, i: (base_g + b, jnp.minimum((c(i) + 1) * per8, n_rows8 - 1), 0, 0))
    full = lambda shape: pl.BlockSpec(shape, lambda b, i: (0,) * len(shape))
    out_f = pl.BlockSpec((g, CHUNK, RW_WIDTH), lambda b, i: (base_g + b, i, 0))
    out_b = pl.BlockSpec((g, CHUNK, RW_WIDTH), lambda b, i: (base_g + b, bwd(i), 0))
    tok = jax.ShapeDtypeStruct((n_view, seq_len, RW_WIDTH), F32)
    out_shape = [tok, tok, tok, tok]
    out_specs = [out_f, out_b, out_f, out_f]
    state_block = (g, 2, RW_HEADS, HEAD_DIM, HEAD_DIM)
    if emit_state:
        out_shape.append(jax.ShapeDtypeStruct((n_seq, 2, RW_HEADS, HEAD_DIM, HEAD_DIM), F32))
        out_specs.append(pl.BlockSpec(state_block, lambda b, i: (b, 0, 0, 0, 0)))
    n_alias = len(prev_outs)
    n_in = 16
    aliases = {n_in + j: j for j in range(n_alias)}
    state_spec = pl.BlockSpec(state_block, lambda b, i: (b if has_init else 0, 0, 0, 0, 0))
    kern = functools.partial(_rwkv_kernel, n_chunks=n_chunks, n_group=g, has_init=has_init,
                             emit_state=emit_state, n_alias=n_alias)
    prev_views = [p.reshape(n_view, seq_len, RW_WIDTH) for p in prev_outs]
    outs = pl.pallas_call(
        kern,
        out_shape=tuple(out_shape),
        grid=(n_seq // g, n_chunks),
        in_specs=[main(fwd), prev8(fwd), next8(fwd), main(bwd), prev8(bwd), next8(bwd), state_spec,
                  full((3, RW_PROJ)), full((1, RW_WIDTH)), full((1, RW_WIDTH)), full((1, RW_WIDTH)),
                  full((2, RW_WIDTH)), full((2, DECAY_LORA, RW_WIDTH)),
                  full((2, RW_WIDTH)), full((2, ICLR_LORA, RW_WIDTH)), full((GATE_LORA, RW_WIDTH))]
                 + [pl.BlockSpec(memory_space=pl.ANY)] * n_alias,
        out_specs=tuple(out_specs),
        scratch_shapes=[pltpu.VMEM(state_block, F32)],
        input_output_aliases=aliases,
        compiler_params=_cparams(("arbitrary", "arbitrary")),
        name="rwkv_scan_init" if has_init else "rwkv_scan_zero",
    )(z3, z4, z4, z3, z4, z4, s0, lp["rw_shift"], lp["rw_k_k"], lp["rw_k_a"], lp["rw_r_k"],
      lp["rw_w0"], lp["rw_w_up"], lp["rw_a0"], lp["rw_a_up"], lp["rw_g_up"], *prev_views)
    return tuple(o.reshape(T_ALL, RW_WIDTH) for o in outs[:4]) + tuple(outs[4:])


def _softmax_pv(scores, values):
    m = scores[0].max(axis=-1, keepdims=True)
    for s in scores[1:]:
        m = jnp.maximum(m, s.max(axis=-1, keepdims=True))
    es = [jnp.exp(s - m) for s in scores]
    l = es[0].sum(axis=-1, keepdims=True)
    for e in es[1:]:
        l = l + e.sum(axis=-1, keepdims=True)
    inv = 1.0 / l
    o = _bdot(es[0] * inv, values[0])
    for e, v in zip(es[1:], values[1:]):
        o = o + _bdot(e * inv, v)
    return o


def _head(x, h):
    return x[:, h * HEAD_DIM:(h + 1) * HEAD_DIM]


def _ctx_attn_kernel(zna_ref, zgq_ref, zgkv_ref, att_ref, nk_ref, nv_ref, gk_ref, gv_ref):
    zna = zna_ref[...]
    q, k, v = zna[:, :NA_WIDTH], zna[:, NA_WIDTH:2 * NA_WIDTH], zna[:, 2 * NA_WIDTH:]
    nk_ref[0] = k
    nv_ref[0] = v
    gq = zgq_ref[...]
    gkv = zgkv_ref[...]
    gk, gv = gkv[:, :GKV_WIDTH], gkv[:, GKV_WIDTH:]
    gk_ref[0] = gk
    gv_ref[0] = gv
    for h in range(GQA_Q_HEADS):
        kv = h // GQA_GROUP
        s = _bdot(_head(gq, h), _head(gk, kv), NT_DIMS) * ATT_SCALE
        att_ref[:, h * HEAD_DIM:(h + 1) * HEAD_DIM] = _softmax_pv([s], [_head(gv, kv)]).astype(BF16)
    for h in range(NA_HEADS):
        s = _bdot(_head(q, h), _head(k, h), NT_DIMS) * ATT_SCALE
        o = _softmax_pv([s], [_head(v, h)])
        att_ref[:, GQ_WIDTH + h * HEAD_DIM:GQ_WIDTH + (h + 1) * HEAD_DIM] = o.astype(BF16)


def _ctx_attn_call(zna, zgq, zgkv):
    row = lambda b: (b, 0)
    bat = lambda b: (b, 0, 0)
    return pl.pallas_call(
        _ctx_attn_kernel,
        out_shape=(jax.ShapeDtypeStruct((T_ALL, ATT_WIDTH), BF16),
                   jax.ShapeDtypeStruct((BATCH, SEQ, NA_WIDTH), F32),
                   jax.ShapeDtypeStruct((BATCH, SEQ, NA_WIDTH), F32),
                   jax.ShapeDtypeStruct((BATCH, SEQ, GKV_WIDTH), F32),
                   jax.ShapeDtypeStruct((BATCH, SEQ, GKV_WIDTH), F32)),
        grid=(BATCH,),
        in_specs=[pl.BlockSpec((SEQ, NA_PROJ), row), pl.BlockSpec((SEQ, GQ_WIDTH), row),
                  pl.BlockSpec((SEQ, 2 * GKV_WIDTH), row)],
        out_specs=(pl.BlockSpec((SEQ, ATT_WIDTH), row),
                   pl.BlockSpec((1, SEQ, NA_WIDTH), bat), pl.BlockSpec((1, SEQ, NA_WIDTH), bat),
                   pl.BlockSpec((1, SEQ, GKV_WIDTH), bat), pl.BlockSpec((1, SEQ, GKV_WIDTH), bat)),
        compiler_params=_cparams(("arbitrary",)),
        name="ctx_attention",
    )(zna, zgq, zgkv)


def _lat_na_kernel(zna_ref, ck_ref, cv_ref, tb_ref, att_in_ref, att_ref):
    del att_in_ref
    i = pl.program_id(1)
    start = jnp.clip(i - NA_WIN_ROWS // 2, 0, GRID_H - NA_WIN_ROWS)
    n_loc = NA_WIN_ROWS * GRID_W
    q = zna_ref[pl.ds(pl.multiple_of(i * GRID_W, GRID_W), GRID_W), 0:NA_WIDTH]
    w0 = pl.multiple_of(start * GRID_W, GRID_W)
    kwin = zna_ref[pl.ds(w0, n_loc), NA_WIDTH:2 * NA_WIDTH]
    vwin = zna_ref[pl.ds(w0, n_loc), 2 * NA_WIDTH:3 * NA_WIDTH]
    kc = ck_ref[0]
    vc = cv_ref[0]
    dr0 = start - i + NA_WIN_ROWS - 1
    outs = []
    for h in range(NA_HEADS):
        bias = jnp.concatenate([tb_ref[h, dr0 + r] for r in range(NA_WIN_ROWS)], axis=1)
        qh = _head(q, h)
        s_loc = _bdot(qh, _head(kwin, h), NT_DIMS) * ATT_SCALE + bias
        s_ctx = _bdot(qh, _head(kc, h), NT_DIMS) * ATT_SCALE
        outs.append(_softmax_pv([s_loc, s_ctx], [_head(vwin, h), _head(vc, h)]))
    att_ref[...] = jnp.concatenate(outs, axis=1).astype(BF16)


def _na_bias_table(rpb):
    w = np.arange(GRID_W)[:, None]
    kc = np.arange(GRID_W)[None, :]
    cs = np.clip(w - NA_WIN_COLS // 2, 0, GRID_W - NA_WIN_COLS)
    valid = (kc >= cs) & (kc < cs + NA_WIN_COLS)
    off = np.clip(kc - w + NA_WIN_COLS - 1, 0, 2 * NA_WIN_COLS - 2)
    tb = rpb[:, :, off]
    return jnp.where(jnp.asarray(valid)[None, None], tb, NEG_BIG).astype(F32)


def _lat_na_call(zna, ck, cv, tb, att):
    lat_blk = T_CTX // DEC_SEQ
    first_row = T_CTX // GRID_W
    return pl.pallas_call(
        _lat_na_kernel,
        out_shape=jax.ShapeDtypeStruct((T_ALL, ATT_WIDTH), BF16),
        grid=(DEC_BATCH, GRID_H),
        in_specs=[pl.BlockSpec((DEC_SEQ, NA_PROJ), lambda b, i: (lat_blk + b, 0)),
                  pl.BlockSpec((1, PAST_LEN, NA_WIDTH), lambda b, i: (b, 0, 0)),
                  pl.BlockSpec((1, PAST_LEN, NA_WIDTH), lambda b, i: (b, 0, 0)),
                  pl.BlockSpec((NA_HEADS, 2 * NA_WIN_ROWS - 1, GRID_W, GRID_W), lambda b, i: (0, 0, 0, 0)),
                  pl.BlockSpec(memory_space=pl.ANY)],
        out_specs=pl.BlockSpec((GRID_W, NA_WIDTH),
                               lambda b, i: (first_row + b * GRID_H + i, GQ_WIDTH // NA_WIDTH)),
        input_output_aliases={4: 0},
        compiler_params=_cparams(("arbitrary", "arbitrary")),
        name="latent_neighbourhood_attention",
    )(zna, ck, cv, tb, att)


def _lat_gqa_kernel(zgq_ref, zgkv_ref, ck_ref, cv_ref, att_in_ref, att_ref):
    del att_in_ref
    q = zgq_ref[...]
    kv = zgkv_ref[...]
    kl, vl = kv[:, :GKV_WIDTH], kv[:, GKV_WIDTH:]
    kc = ck_ref[0]
    vc = cv_ref[0]
    for h in range(GQA_Q_HEADS):
        g = h // GQA_GROUP
        qh = _head(q, h)
        s_c = _bdot(qh, _head(kc, g), NT_DIMS) * ATT_SCALE
        s_l = _bdot(qh, _head(kl, g), NT_DIMS) * ATT_SCALE
        o = _softmax_pv([s_c, s_l], [_head(vc, g), _head(vl, g)])
        att_ref[:, h * HEAD_DIM:(h + 1) * HEAD_DIM] = o.astype(BF16)


def _lat_gqa_call(zgq, zgkv, ck, cv, att):
    n_q = DEC_SEQ // TQ_GQA
    first_q = T_CTX // TQ_GQA
    lat_blk = T_CTX // DEC_SEQ
    return pl.pallas_call(
        _lat_gqa_kernel,
        out_shape=jax.ShapeDtypeStruct((T_ALL, ATT_WIDTH), BF16),
        grid=(DEC_BATCH, n_q),
        in_specs=[pl.BlockSpec((TQ_GQA, GQ_WIDTH), lambda b, j: (first_q + b * n_q + j, 0)),
                  pl.BlockSpec((DEC_SEQ, 2 * GKV_WIDTH), lambda b, j: (lat_blk + b, 0)),
                  pl.BlockSpec((1, PAST_LEN, GKV_WIDTH), lambda b, j: (b, 0, 0)),
                  pl.BlockSpec((1, PAST_LEN, GKV_WIDTH), lambda b, j: (b, 0, 0)),
                  pl.BlockSpec(memory_space=pl.ANY)],
        out_specs=pl.BlockSpec((TQ_GQA, GQ_WIDTH), lambda b, j: (first_q + b * n_q + j, 0)),
        input_output_aliases={4: 0},
        compiler_params=_cparams(("arbitrary", "arbitrary")),
        name="latent_gqa_attention",
    )(zgq, zgkv, ck, cv, att)


def _out_proj_kernel(x_ref, yf_ref, yb_ref, bonus_ref, gate_ref, att_ref, mod_ref, lng_ref, lnb_ref,
                     wout_ref, g2_ref, rw_ref, rb_ref,
                     x1_ref, h2_ref, slot_ref, gates_ref, cnt_ref):
    ones_bd = _head_ones(RW_WIDTH)
    o = yf_ref[...] + yb_ref[...]
    mu = _head_sum(o, ones_bd) * (1.0 / HEAD_DIM)
    dlt = o - mu
    var = _head_sum(dlt * dlt, ones_bd) * (1.0 / HEAD_DIM)
    ln = dlt * lax.rsqrt(var + LNX_EPS) * lng_ref[...] + lnb_ref[...]
    rw = ((ln + bonus_ref[...]) * gate_ref[...]).astype(BF16)
    att = att_ref[...]
    mix = (jnp.dot(rw, wout_ref[0:RW_WIDTH, :], preferred_element_type=F32)
           + jnp.dot(att[:, :GQ_WIDTH], wout_ref[RW_WIDTH + NA_WIDTH:, :], preferred_element_type=F32)
           + jnp.dot(att[:, GQ_WIDTH:], wout_ref[RW_WIDTH:RW_WIDTH + NA_WIDTH, :],
                     preferred_element_type=F32))
    gate1 = mod_ref[0, 2:3, :]
    shift2 = mod_ref[0, 3:4, :]
    scale2 = mod_ref[0, 4:5, :]
    x1 = x_ref[...] + gate1 * mix
    x1_ref[...] = x1
    h2 = _rms(x1, g2_ref[...]) * (1.0 + scale2) + shift2
    h2_ref[...] = h2.astype(BF16)

    logits = _dot3(h2, rw_ref[...]) + rb_ref[...]
    tm = logits.shape[0]
    col = lax.broadcasted_iota(jnp.int32, (tm, N_EXPERTS), 1)
    lane4 = lax.broadcasted_iota(jnp.int32, (tm, TOP_K), 1)
    work = logits
    sels, vals = [], []
    for k in range(TOP_K):
        m = work.max(axis=-1, keepdims=True)
        idx = jnp.min(jnp.where(work == m, col, N_EXPERTS), axis=-1, keepdims=True)
        sel = col == idx
        sels.append(sel)
        vals.append(m)
        work = jnp.where(sel, -jnp.inf, work)
    es = [jnp.exp(v - vals[0]) for v in vals]
    inv = 1.0 / (es[0] + es[1] + es[2] + es[3])
    gates = jnp.zeros((tm, TOP_K), F32)
    for k in range(TOP_K):
        gates = jnp.where(lane4 == k, es[k] * inv, gates)
    assign = jnp.zeros((tm, N_EXPERTS), F32)
    for sel in sels:
        assign = assign + sel.astype(F32)
    r_i = lax.broadcasted_iota(jnp.int32, (tm, tm), 0)
    c_i = lax.broadcasted_iota(jnp.int32, (tm, tm), 1)
    before = jnp.dot((c_i < r_i).astype(BF16), assign.astype(BF16), preferred_element_type=F32)
    cnt = jnp.sum(assign, axis=0, keepdims=True)
    seg_units = jnp.floor((cnt + (SEG_ALIGN - 1)) * (1.0 / SEG_ALIGN))
    e_r = lax.broadcasted_iota(jnp.int32, (N_EXPERTS, N_EXPERTS), 0)
    e_c = lax.broadcasted_iota(jnp.int32, (N_EXPERTS, N_EXPERTS), 1)
    loc = SEG_ALIGN * jnp.dot(seg_units.astype(BF16), (e_r < e_c).astype(BF16),
                              preferred_element_type=F32)
    pos = before + loc
    slot = jnp.zeros((tm, TOP_K), F32)
    for k in range(TOP_K):
        sk = jnp.sum(jnp.where(sels[k], pos, 0.0), axis=-1, keepdims=True)
        slot = jnp.where(lane4 == k, sk, slot)
    slot_ref[...] = slot.astype(jnp.int32)
    gates_ref[...] = gates
    cnt_ref[0] = cnt.astype(jnp.int32)


def _out_proj_call(x, yf, yb, bonus, gate, att, mods_l, lp):
    n_blk = T_ALL // TM_TOK
    row = lambda i: (i, 0)
    full2 = lambda r, c: pl.BlockSpec((r, c), lambda i: (0, 0))
    tokw = lambda w: pl.BlockSpec((TM_TOK, w), row)
    return pl.pallas_call(
        _out_proj_kernel,
        out_shape=(jax.ShapeDtypeStruct((T_ALL, D_MODEL), F32),
                   jax.ShapeDtypeStruct((T_ALL, D_MODEL), BF16),
                   jax.ShapeDtypeStruct((T_ALL, TOP_K), jnp.int32),
                   jax.ShapeDtypeStruct((T_ALL, TOP_K), F32),
                   jax.ShapeDtypeStruct((n_blk, 1, N_EXPERTS), jnp.int32)),
        grid=(n_blk,),
        in_specs=[tokw(D_MODEL), tokw(RW_WIDTH), tokw(RW_WIDTH), tokw(RW_WIDTH), tokw(RW_WIDTH),
                  tokw(ATT_WIDTH),
                  pl.BlockSpec((1, N_MOD, D_MODEL), lambda i: (_group_of_block(i, TM_TOK), 0, 0)),
                  full2(1, RW_WIDTH), full2(1, RW_WIDTH), full2(D_MODEL, D_MODEL), full2(1, D_MODEL),
                  full2(D_MODEL, N_EXPERTS), full2(1, N_EXPERTS)],
        out_specs=(tokw(D_MODEL), tokw(D_MODEL), tokw(TOP_K), tokw(TOP_K),
                   pl.BlockSpec((1, 1, N_EXPERTS), lambda i: (i, 0, 0))),
        compiler_params=_cparams(("arbitrary",)),
        name="out_proj_router",
    )(x, yf, yb, bonus, gate, att, mods_l, lp["rw_ln_g"], lp["rw_ln_b"], lp["w_out"], lp["norm2_g"],
      lp["router_w"], lp["router_b"])


def _for_each_piece(n_units, max_bit, fn):
    for b in range(max_bit, -1, -1):
        @pl.when(((n_units >> b) & 1) == 1)
        def _(b=b):
            off = ((n_units >> (b + 1)) << (b + 1)) * SEG_ALIGN
            fn(pl.multiple_of(off, SEG_ALIGN), SEG_ALIGN << b)


def _segment_copies(local_ref, sorted_hbm, sem, blk, seg_ref, loc_ref, dst_ref, to_sorted, wait):
    def body(e, carry):
        t = blk * N_EXPERTS + e
        loc = loc_ref[t]
        dst = dst_ref[t]

        def piece(off, size):
            a = local_ref.at[pl.ds(pl.multiple_of(loc + off, SEG_ALIGN), size), :]
            b = sorted_hbm.at[pl.ds(pl.multiple_of(dst + off, SEG_ALIGN), size), :]
            cp = pltpu.make_async_copy(a, b, sem) if to_sorted else pltpu.make_async_copy(b, a, sem)
            if wait:
                cp.wait()
            else:
                cp.start()

        _for_each_piece(seg_ref[t], SEG_MAX_BIT, piece)
        return carry
    lax.fori_loop(0, N_EXPERTS, body, 0)


def _dispatch_kernel(seg_ref, loc_ref, dst_ref, tail_ref, tail_dst_ref, h2_ref, slot_ref, xs_hbm,
                     xs_local, zero_buf, sem, zsem):
    j = pl.program_id(0)
    n = pl.num_programs(0)
    buf = j % 2

    @pl.when(j == 0)
    def _():
        zero_buf[...] = jnp.zeros(zero_buf.shape, BF16)

        def tails(wait):
            def body(e, carry):
                def piece(off, size):
                    cp = pltpu.make_async_copy(
                        zero_buf.at[pl.ds(0, size), :],
                        xs_hbm.at[pl.ds(pl.multiple_of(tail_dst_ref[e] + off, SEG_ALIGN), size), :], zsem)
                    if wait:
                        cp.wait()
                    else:
                        cp.start()
                _for_each_piece(tail_ref[e], TAIL_MAX_BIT, piece)
                return carry
            lax.fori_loop(0, N_EXPERTS, body, 0)
        tails(False)
        tails(True)

    lane = lax.broadcasted_iota(jnp.int32, (TM_TOK, LANES), 1)
    slots = slot_ref[...].astype(F32)
    wide = jnp.zeros((TM_TOK, LANES), F32)
    for k in range(TOP_K):
        wide = jnp.where(lane == k, slots[:, k:k + 1], wide)
    slot_rows = wide.T
    h2 = h2_ref[...]
    for c in range(LOCAL_ROWS // TM_TOK):
        row = lax.broadcasted_iota(jnp.int32, (TM_TOK, TM_TOK), 0).astype(F32) + float(c * TM_TOK)
        onehot = jnp.zeros((TM_TOK, TM_TOK), F32)
        for k in range(TOP_K):
            onehot = jnp.where(row == slot_rows[k:k + 1, :], 1.0, onehot)
        onehot = onehot.astype(BF16)
        xs_local[buf, c * TM_TOK:(c + 1) * TM_TOK, :] = jnp.dot(
            onehot, h2, preferred_element_type=F32).astype(BF16)

    @pl.when(j > 0)
    def _():
        _segment_copies(xs_local.at[1 - buf], xs_hbm, sem.at[1 - buf], j - 1, seg_ref, loc_ref, dst_ref,
                        True, True)

    _segment_copies(xs_local.at[buf], xs_hbm, sem.at[buf], j, seg_ref, loc_ref, dst_ref, True, False)

    @pl.when(j == n - 1)
    def _():
        _segment_copies(xs_local.at[buf], xs_hbm, sem.at[buf], j, seg_ref, loc_ref, dst_ref, True, True)


def _dispatch_call(tabs, h2, slot):
    n_blk = T_ALL // TM_TOK
    row = lambda j, *_: (j, 0)
    grid_spec = pltpu.PrefetchScalarGridSpec(
        num_scalar_prefetch=5,
        grid=(n_blk,),
        in_specs=[pl.BlockSpec((TM_TOK, D_MODEL), row), pl.BlockSpec((TM_TOK, TOP_K), row)],
        out_specs=pl.BlockSpec(memory_space=pl.ANY),
        scratch_shapes=[pltpu.VMEM((2, LOCAL_ROWS, D_MODEL), BF16), pltpu.VMEM((TM_EXP, D_MODEL), BF16),
                        pltpu.SemaphoreType.DMA((2,)), pltpu.SemaphoreType.DMA],
    )
    return pl.pallas_call(
        _dispatch_kernel,
        out_shape=jax.ShapeDtypeStruct((N_SLOTS, D_MODEL), BF16),
        grid_spec=grid_spec,
        compiler_params=_cparams(("arbitrary",)),
        name="moe_dispatch",
    )(tabs["seg"], tabs["loc"], tabs["dst"], tabs["tail"], tabs["tail_dst"], h2, slot)


def _expert_kernel(be_ref, first_ref, par_ref, nexte_ref, meta_ref, x_ref, wgu_hbm, bgu_ref, wd_hbm, bd_ref,
                   y_ref, wgu_f, wd_f, wgu_bf, wd_bf, sem, *, layer):
    i = pl.program_id(0)
    n_used = meta_ref[0]

    def weight_copies(e, buf):
        return (pltpu.make_async_copy(wgu_hbm.at[layer, e], wgu_f.at[buf], sem.at[buf]),
                pltpu.make_async_copy(wd_hbm.at[layer, e], wd_f.at[buf], sem.at[buf]))

    @pl.when(i < n_used)
    def _():
        e = be_ref[i]
        buf = par_ref[i]

        @pl.when(first_ref[i] == 1)
        def _():
            @pl.when(i == 0)
            def _():
                for cp in weight_copies(e, buf):
                    cp.start()
            for cp in weight_copies(e, buf):
                cp.wait()
            wgu_bf[...] = wgu_f[buf].astype(BF16)
            wd_bf[...] = wd_f[buf].astype(BF16)
            nxt = nexte_ref[i]

            @pl.when(nxt >= 0)
            def _():
                for cp in weight_copies(nxt, 1 - buf):
                    cp.start()

        gu = jnp.dot(x_ref[...], wgu_bf[...], preferred_element_type=F32) + bgu_ref[0, 0]
        glu = jnp.minimum(gu[:, :D_FF], SWIGLU_LIMIT)
        lin = jnp.clip(gu[:, D_FF:], -SWIGLU_LIMIT, SWIGLU_LIMIT)
        act = glu * _sigmoid(SWIGLU_ALPHA * glu) * (lin + 1.0)
        y = jnp.dot(act.astype(BF16), wd_bf[...], preferred_element_type=F32) + bd_ref[0, 0]
        y_ref[...] = y.astype(BF16)


def _expert_call(tabs, xs, w_gu, b_gu, w_down, b_down, layer):
    def bmap(i, be, first, par, nxt, meta_):
        return (layer, be[i], 0, 0)

    def rmap(i, be, first, par, nxt, meta_):
        return (jnp.minimum(i, meta_[0] - 1), 0)

    grid_spec = pltpu.PrefetchScalarGridSpec(
        num_scalar_prefetch=5,
        grid=(N_SLOT_BLOCKS,),
        in_specs=[pl.BlockSpec((TM_EXP, D_MODEL), rmap),
                  pl.BlockSpec(memory_space=pl.ANY),
                  pl.BlockSpec((1, 1, 1, 2 * D_FF), bmap),
                  pl.BlockSpec(memory_space=pl.ANY),
                  pl.BlockSpec((1, 1, 1, D_MODEL), bmap)],
        out_specs=pl.BlockSpec((TM_EXP, D_MODEL), rmap),
        scratch_shapes=[pltpu.VMEM((2, D_MODEL, 2 * D_FF), F32), pltpu.VMEM((2, D_FF, D_MODEL), F32),
                        pltpu.VMEM((D_MODEL, 2 * D_FF), BF16), pltpu.VMEM((D_FF, D_MODEL), BF16),
                        pltpu.SemaphoreType.DMA((2,))],
    )
    return pl.pallas_call(
        functools.partial(_expert_kernel, layer=layer),
        out_shape=jax.ShapeDtypeStruct((N_SLOTS, D_MODEL), BF16),
        grid_spec=grid_spec,
        compiler_params=_cparams(("arbitrary",)),
        name="moe_experts",
    )(tabs["block_e"], tabs["first"], tabs["parity"], tabs["next_e"], tabs["n_used"], xs, w_gu,
      b_gu.reshape(DEPTH, N_EXPERTS, 1, 2 * D_FF), w_down, b_down.reshape(DEPTH, N_EXPERTS, 1, D_MODEL))


def _combine_kernel(seg_ref, loc_ref, dst_ref, total_ref, ys_hbm, x1_ref, slot_ref, gates_ref, mod_ref,
                    fg_ref, o_ref, ybuf, sem, *, final):
    j = pl.program_id(0)
    n = pl.num_programs(0)
    buf = j % 2

    @pl.when(j == 0)
    def _():
        _segment_copies(ybuf.at[0], ys_hbm, sem.at[0], 0, seg_ref, loc_ref, dst_ref, False, False)

    @pl.when(j + 1 < n)
    def _():
        _segment_copies(ybuf.at[1 - buf], ys_hbm, sem.at[1 - buf], j + 1, seg_ref, loc_ref, dst_ref,
                        False, False)

    _segment_copies(ybuf.at[buf], ys_hbm, sem.at[buf], j, seg_ref, loc_ref, dst_ref, False, True)

    slots = slot_ref[...]
    gates = gates_ref[...]
    total = total_ref[j]
    ff = jnp.zeros((TM_TOK, D_MODEL), F32)
    for c in range(LOCAL_ROWS // TM_TOK):
        col = lax.broadcasted_iota(jnp.int32, (TM_TOK, TM_TOK), 1) + c * TM_TOK
        q = jnp.zeros((TM_TOK, TM_TOK), F32)
        for k in range(TOP_K):
            q = jnp.where(col == slots[:, k:k + 1], gates[:, k:k + 1], q)
        rows = lax.broadcasted_iota(jnp.int32, (TM_TOK, D_MODEL), 0) + c * TM_TOK
        y = jnp.where(rows < total, ybuf[buf, c * TM_TOK:(c + 1) * TM_TOK, :], jnp.zeros((), BF16))
        ff = ff + jnp.dot(q.astype(BF16), y, preferred_element_type=F32)
    x = x1_ref[...] + mod_ref[0, 5:6, :] * ff
    if final:
        x = _rms(x, fg_ref[...])
    o_ref[...] = x


def _combine_call(tabs, ys, x1, slot, gates, mods_l, final_g, final):
    n_blk = T_ALL // TM_TOK
    row = lambda j, *_: (j, 0)
    grid_spec = pltpu.PrefetchScalarGridSpec(
        num_scalar_prefetch=4,
        grid=(n_blk,),
        in_specs=[pl.BlockSpec(memory_space=pl.ANY),
                  pl.BlockSpec((TM_TOK, D_MODEL), row),
                  pl.BlockSpec((TM_TOK, TOP_K), row),
                  pl.BlockSpec((TM_TOK, TOP_K), row),
                  pl.BlockSpec((1, N_MOD, D_MODEL), lambda j, *_: (_group_of_block(j, TM_TOK), 0, 0)),
                  pl.BlockSpec((1, D_MODEL), lambda j, *_: (0, 0))],
        out_specs=pl.BlockSpec((TM_TOK, D_MODEL), row),
        scratch_shapes=[pltpu.VMEM((2, LOCAL_ROWS, D_MODEL), BF16), pltpu.SemaphoreType.DMA((2,))],
    )
    return pl.pallas_call(
        functools.partial(_combine_kernel, final=final),
        out_shape=jax.ShapeDtypeStruct((T_ALL, D_MODEL), F32),
        grid_spec=grid_spec,
        compiler_params=_cparams(("arbitrary",)),
        name="moe_combine_final" if final else "moe_combine",
    )(tabs["seg"], tabs["loc"], tabs["dst"], tabs["total"], ys, x1, slot, gates, mods_l, final_g)


def _routing_tables(cnt):
    i32 = jnp.int32
    cnt = cnt.reshape(T_ALL // TM_TOK, N_EXPERTS)
    seg = (cnt + SEG_ALIGN - 1) // SEG_ALIGN * SEG_ALIGN
    loc = jnp.cumsum(seg, axis=1) - seg
    total = jnp.sum(seg, axis=1)
    rows_e = jnp.sum(seg, axis=0)
    region = (rows_e + TM_EXP - 1) // TM_EXP * TM_EXP
    region_end = jnp.cumsum(region)
    base = region_end - region
    dst = base[None, :] + jnp.cumsum(seg, axis=0) - seg
    n_used = jnp.maximum(region_end[-1] // TM_EXP, 1).astype(i32)
    blk = jnp.minimum(jnp.arange(N_SLOT_BLOCKS, dtype=i32), n_used - 1)
    block_e = jnp.sum((region_end[None, :] <= (blk * TM_EXP)[:, None]).astype(i32), axis=1)
    block_e = jnp.minimum(block_e, N_EXPERTS - 1).astype(i32)
    first = jnp.concatenate([jnp.ones((1,), i32), (block_e[1:] != block_e[:-1]).astype(i32)])
    parity = (jnp.cumsum(first) - 1) % 2
    e_ids = jnp.arange(N_EXPERTS, dtype=i32)
    later_used = (e_ids[None, :] > e_ids[:, None]) & (region[None, :] > 0)
    next_used = jnp.min(jnp.where(later_used, e_ids[None, :], N_EXPERTS), axis=1)
    next_e = next_used[block_e]
    next_e = jnp.where(next_e < N_EXPERTS, next_e, -1)
    flat = lambda a: a.reshape(-1).astype(i32)
    return dict(seg=flat(seg // SEG_ALIGN), loc=flat(loc), dst=flat(dst), total=flat(total),
                tail=flat((region - rows_e) // SEG_ALIGN), tail_dst=flat(base + rows_e),
                block_e=block_e, first=first, parity=flat(parity), next_e=flat(next_e),
                n_used=n_used.reshape(1))


def kernel(x_prompt, x_sample, cache_na_k, cache_na_v, cache_gqa_k, cache_gqa_v, state_rwkv, c, c_ctx,
           w_mod, b_mod, norm1_g, norm2_g, w_in, rw_shift, rw_w0, rw_w_up, rw_a0, rw_a_up, rw_g_up,
           rw_k_k, rw_k_a, rw_r_k, rw_ln_g, rw_ln_b, na_rpb, q_norm, k_norm, w_out, router_w, router_b,
           moe_w_gu, moe_b_gu, moe_w_down, moe_b_down, final_norm_g):
    x = jnp.concatenate([x_prompt.reshape(T_CTX, D_MODEL), x_sample.reshape(T_LAT, D_MODEL)], axis=0)
    cvecs = jnp.concatenate([c_ctx[None, :], c, jnp.zeros((SUBLANES - N_GROUPS, D_MODEL), F32)], axis=0)
    mods = _mods_call(cvecs, w_mod, b_mod)
    mods = mods[:, :N_GROUPS].reshape(DEPTH, N_GROUPS, N_MOD, D_MODEL)
    rope = _rope_tables()
    w_in_bf = w_in.astype(BF16)
    w_out_bf = w_out.astype(BF16)
    rw_g_up_bf = rw_g_up.astype(BF16)
    final_g = final_norm_g.reshape(1, D_MODEL)

    na_k_l, na_v_l, g_k_l, g_v_l, st_l = [], [], [], [], []
    for l in range(DEPTH):
        lp = {
            "rw_shift": rw_shift[l], "rw_k_k": rw_k_k[l].reshape(1, RW_WIDTH),
            "rw_k_a": rw_k_a[l].reshape(1, RW_WIDTH), "rw_r_k": rw_r_k[l].reshape(1, RW_WIDTH),
            "rw_w0": rw_w0[l], "rw_w_up": rw_w_up[l], "rw_a0": rw_a0[l], "rw_a_up": rw_a_up[l],
            "rw_g_up": rw_g_up_bf[l], "rw_ln_g": rw_ln_g[l].reshape(1, RW_WIDTH),
            "rw_ln_b": rw_ln_b[l].reshape(1, RW_WIDTH), "w_out": w_out_bf[l],
            "norm2_g": norm2_g[l].reshape(1, D_MODEL), "router_w": router_w[l],
            "router_b": router_b[l].reshape(1, N_EXPERTS),
        }
        qk_g = jnp.concatenate([jnp.tile(q_norm[l], GQA_Q_HEADS), jnp.tile(k_norm[l], GQA_KV_HEADS)])
        zrw, zna, zgq, zgkv = _in_proj_call(x, mods[l], norm1_g[l].reshape(1, D_MODEL), w_in_bf[l],
                                            qk_g.reshape(1, GQ_WIDTH + GKV_WIDTH), rope)

        s0_lat = state_rwkv[:, l]
        rw_ctx = _rwkv_call(zrw, s0_lat, lp, (), n_seq=BATCH, seq_len=SEQ, row_base=0,
                            has_init=False, emit_state=True)
        yf, yb, bonus, gate = _rwkv_call(zrw, s0_lat, lp, rw_ctx[:4], n_seq=DEC_BATCH, seq_len=DEC_SEQ,
                                         row_base=T_CTX, has_init=True, emit_state=False)
        st_l.append(rw_ctx[4])

        att, nk, nv, gk, gv = _ctx_attn_call(zna, zgq, zgkv)
        na_k_l.append(nk)
        na_v_l.append(nv)
        g_k_l.append(gk)
        g_v_l.append(gv)
        att = _lat_na_call(zna, cache_na_k[:, l].reshape(DEC_BATCH, PAST_LEN, NA_WIDTH),
                           cache_na_v[:, l].reshape(DEC_BATCH, PAST_LEN, NA_WIDTH),
                           _na_bias_table(na_rpb[l]), att)
        att = _lat_gqa_call(zgq, zgkv, cache_gqa_k[:, l].reshape(DEC_BATCH, PAST_LEN, GKV_WIDTH),
                            cache_gqa_v[:, l].reshape(DEC_BATCH, PAST_LEN, GKV_WIDTH), att)

        x1, h2, slot, gates, counts = _out_proj_call(x, yf, yb, bonus, gate, att, mods[l], lp)
        tabs = _routing_tables(counts)
        xs = _dispatch_call(tabs, h2, slot)
        ys = _expert_call(tabs, xs, moe_w_gu, moe_b_gu, moe_w_down, moe_b_down, l)
        x = _combine_call(tabs, ys, x1, slot, gates, mods[l], final_g, l == DEPTH - 1)

    y_prompt = x[:T_CTX].reshape(BATCH, SEQ, D_MODEL)
    y_sample = x[T_CTX:].reshape(DEC_BATCH, DEC_SEQ, D_MODEL)
    heads = lambda ts, n: jnp.stack(ts, axis=1).reshape(BATCH, DEPTH, SEQ, n, HEAD_DIM)
    return (y_prompt, y_sample, heads(na_k_l, NA_HEADS), heads(na_v_l, NA_HEADS),
            heads(g_k_l, GQA_KV_HEADS), heads(g_v_l, GQA_KV_HEADS), jnp.stack(st_l, axis=1))
```

```python
import functools

import numpy as np
import jax
import jax.numpy as jnp
from jax import lax
from jax.experimental import pallas as pl
from jax.experimental.pallas import tpu as pltpu

F32 = jnp.float32
BF16 = jnp.bfloat16

D_MODEL = 1024
BATCH = 32
SEQ = 256
DEPTH = 4
DEC_BATCH = 2
DEC_SEQ = 2048
PAST_LEN = 512
GRID_W = 64
GRID_H = DEC_SEQ // GRID_W
HEAD_DIM = 64
RW_HEADS = 4
RW_WIDTH = RW_HEADS * HEAD_DIM
DECAY_LORA = 64
ICLR_LORA = 64
GATE_LORA = 128
NA_HEADS = 4
NA_WIDTH = NA_HEADS * HEAD_DIM
NA_WIN_ROWS = 8
NA_WIN_COLS = 16
GQA_Q_HEADS = 8
GQA_KV_HEADS = 2
GQA_GROUP = GQA_Q_HEADS // GQA_KV_HEADS
GQ_WIDTH = GQA_Q_HEADS * HEAD_DIM
GKV_WIDTH = GQA_KV_HEADS * HEAD_DIM
RW_PROJ = 3 * RW_WIDTH + DECAY_LORA + ICLR_LORA + GATE_LORA
NA_PROJ = 3 * NA_WIDTH
GQA_PROJ = GQ_WIDTH + 2 * GKV_WIDTH
IN_PROJ = RW_PROJ + NA_PROJ + GQA_PROJ
ATT_WIDTH = GQ_WIDTH + NA_WIDTH
N_EXPERTS = 32
TOP_K = 4
D_FF = D_MODEL
SWIGLU_LIMIT = 7.0
SWIGLU_ALPHA = 1.702
ROPE_THETA = 10000.0
NORM_EPS = 1e-6
LNX_EPS = 64e-5
N_MOD = 6
ATT_SCALE = HEAD_DIM ** -0.5
NEG_BIG = -1e30

T_CTX = BATCH * SEQ
T_LAT = DEC_BATCH * DEC_SEQ
T_ALL = T_CTX + T_LAT
N_GROUPS = 1 + DEC_BATCH

LANES = 128
SUBLANES = 8
TM_TOK = 512
CHUNK = 64
RW_SEQ_GROUP = 2
TQ_GQA = 256
TM_EXP = 256
SEG_ALIGN = 16
SEG_MAX_BIT = (TM_TOK // SEG_ALIGN).bit_length() - 1
TAIL_MAX_BIT = (TM_EXP // SEG_ALIGN - 1).bit_length() - 1
TOTAL_MAX_BIT = (-(-(TOP_K * TM_TOK + N_EXPERTS * (SEG_ALIGN - 1)) // SEG_ALIGN)).bit_length() - 1
LOCAL_ROWS = -(-(TOP_K * TM_TOK + N_EXPERTS * (SEG_ALIGN - 1)) // TM_TOK) * TM_TOK
N_SLOT_BLOCKS = (-(-(T_ALL * TOP_K + (T_ALL // TM_TOK) * N_EXPERTS * (SEG_ALIGN - 1)) // TM_EXP)
                 + N_EXPERTS)
N_SLOTS = N_SLOT_BLOCKS * TM_EXP
VMEM_LIMIT = 56 * 1024 * 1024

NT_DIMS = (((1,), (1,)), ((), ()))
TN_DIMS = (((0,), (0,)), ((), ()))


def _bdot(a, b, dims=None):
    a = a.astype(BF16)
    b = b.astype(BF16)
    if dims is None:
        return jnp.dot(a, b, preferred_element_type=F32)
    return lax.dot_general(a, b, dims, preferred_element_type=F32)


def _split(a):
    hi = a.astype(BF16)
    lo = (a - hi.astype(F32)).astype(BF16)
    return hi, lo


def _dot3(a, b, dims=None):
    ah, al = _split(a)
    bh, bl = _split(b)
    return _bdot(ah, bh, dims) + _bdot(ah, bl, dims) + _bdot(al, bh, dims)


def _dot_exact_lhs(a_exact, b):
    h1 = b.astype(BF16)
    r1 = b - h1.astype(F32)
    h2 = r1.astype(BF16)
    h3 = (r1 - h2.astype(F32)).astype(BF16)
    return _bdot(a_exact, h1) + _bdot(a_exact, h2) + _bdot(a_exact, h3)


def _head_ones(n):
    r = lax.broadcasted_iota(jnp.int32, (n, n), 0) // HEAD_DIM
    c = lax.broadcasted_iota(jnp.int32, (n, n), 1) // HEAD_DIM
    return (r == c).astype(BF16)


def _head_sum(x, ones_bd):
    hi, lo = _split(x)
    return (jnp.dot(hi, ones_bd, preferred_element_type=F32)
            + jnp.dot(lo, ones_bd, preferred_element_type=F32))


def _sigmoid(x):
    return 1.0 / (1.0 + jnp.exp(-x))


def _cparams(sem):
    return pltpu.CompilerParams(dimension_semantics=sem, vmem_limit_bytes=VMEM_LIMIT)


def _group_of_block(i, rows_per_block):
    first_lat = T_CTX // rows_per_block
    per_sample = DEC_SEQ // rows_per_block
    return jnp.where(i < first_lat, 0, 1 + (i - first_lat) // per_sample)


def _mods_kernel(c_ref, w_ref, b_ref, o_ref):
    c = c_ref[...]
    s = c * _sigmoid(c)
    o_ref[0] = _dot3(s, w_ref[0]) + b_ref[0]


def _mods_call(cvecs, w_mod, b_mod):
    tn = 1536
    n_rows = cvecs.shape[0]
    return pl.pallas_call(
        _mods_kernel,
        out_shape=jax.ShapeDtypeStruct((DEPTH, n_rows, N_MOD * D_MODEL), F32),
        grid=(DEPTH, N_MOD * D_MODEL // tn),
        in_specs=[
            pl.BlockSpec((n_rows, D_MODEL), lambda l, j: (0, 0)),
            pl.BlockSpec((1, D_MODEL, tn), lambda l, j: (l, 0, j)),
            pl.BlockSpec((1, 1, tn), lambda l, j: (l, 0, j)),
        ],
        out_specs=pl.BlockSpec((1, n_rows, tn), lambda l, j: (l, 0, j)),
        compiler_params=_cparams(("arbitrary", "arbitrary")),
        name="adaln_mods",
    )(cvecs, w_mod, b_mod.reshape(DEPTH, 1, N_MOD * D_MODEL))


def _rms(x, g):
    ms = jnp.mean(x * x, axis=-1, keepdims=True)
    return x * lax.rsqrt(ms + NORM_EPS) * g


def _in_proj_kernel(x_ref, mod_ref, g_ref, w_ref, qkg_ref, rc_ref, rs1_ref, rs2_ref,
                    zrw_ref, zna_ref, zgq_ref, zgkv_ref):
    x = x_ref[...]
    shift1 = mod_ref[0, 0:1, :]
    scale1 = mod_ref[0, 1:2, :]
    h = _rms(x, g_ref[...]) * (1.0 + scale1) + shift1
    z = jnp.dot(h.astype(BF16), w_ref[...], preferred_element_type=F32)
    zrw_ref[...] = z[:, :RW_PROJ]
    zna_ref[...] = z[:, RW_PROJ:RW_PROJ + NA_PROJ]
    qk_w = GQ_WIDTH + GKV_WIDTH
    qk = z[:, RW_PROJ + NA_PROJ:RW_PROJ + NA_PROJ + qk_w]
    ones_bd = _head_ones(LANES)
    sq = qk * qk
    ssq = jnp.concatenate(
        [_head_sum(sq[:, j * LANES:(j + 1) * LANES], ones_bd) for j in range(qk_w // LANES)], axis=1)
    qkn = qk * lax.rsqrt(ssq * (1.0 / HEAD_DIM) + NORM_EPS) * qkg_ref[...]
    reps = qk_w // LANES
    rc = jnp.concatenate([rc_ref[...]] * reps, axis=1)
    rs1 = jnp.concatenate([rs1_ref[...]] * reps, axis=1)
    rs2 = jnp.concatenate([rs2_ref[...]] * reps, axis=1)
    half = HEAD_DIM // 4
    qkr = qkn * rc + pltpu.roll(qkn, half, 1) * rs1 + pltpu.roll(qkn, qk_w - half, 1) * rs2
    zgq_ref[...] = qkr[:, :GQ_WIDTH]
    zgkv_ref[:, :GKV_WIDTH] = qkr[:, GQ_WIDTH:]
    zgkv_ref[:, GKV_WIDTH:] = z[:, RW_PROJ + NA_PROJ + qk_w:]


def _rope_tables():
    t = np.arange(DEC_SEQ)
    pos = np.stack([t // GRID_W, t % GRID_W], axis=1).astype(np.float32)
    axis_dim = HEAD_DIM // 2
    inv = ROPE_THETA ** (-np.arange(0, axis_dim, 2, dtype=np.float32) / axis_dim)
    d = np.arange(LANES) % HEAD_DIM
    part = d // axis_dim
    within = d % axis_dim
    freq = within % (axis_dim // 2)
    second = within // (axis_dim // 2)
    ang = jnp.asarray(pos)[:, part] * jnp.asarray(inv)[freq][None, :]
    cos = jnp.cos(ang)
    sin = jnp.sin(ang)
    s1 = jnp.where(second[None, :] == 1, sin, 0.0)
    s2 = jnp.where(second[None, :] == 0, -sin, 0.0)
    ident = jnp.ones((TM_TOK, LANES), F32)
    zero = jnp.zeros((TM_TOK, LANES), F32)
    return (jnp.concatenate([cos, ident], 0), jnp.concatenate([s1, zero], 0),
            jnp.concatenate([s2, zero], 0))


def _in_proj_call(x, mods_l, g1, w_in_bf, qk_g, rope):
    n_blk = T_ALL // TM_TOK
    lat_blk = DEC_SEQ // TM_TOK
    first_lat = T_CTX // TM_TOK

    def rope_idx(i):
        return (jnp.where(i < first_lat, lat_blk, (i - first_lat) % lat_blk), 0)

    row = lambda i: (i, 0)
    rope_spec = pl.BlockSpec((TM_TOK, LANES), rope_idx)
    return pl.pallas_call(
        _in_proj_kernel,
        out_shape=(jax.ShapeDtypeStruct((T_ALL, RW_PROJ), F32),
                   jax.ShapeDtypeStruct((T_ALL, NA_PROJ), F32),
                   jax.ShapeDtypeStruct((T_ALL, GQ_WIDTH), F32),
                   jax.ShapeDtypeStruct((T_ALL, 2 * GKV_WIDTH), F32)),
        grid=(n_blk,),
        in_specs=[
            pl.BlockSpec((TM_TOK, D_MODEL), row),
            pl.BlockSpec((1, N_MOD, D_MODEL), lambda i: (_group_of_block(i, TM_TOK), 0, 0)),
            pl.BlockSpec((1, D_MODEL), lambda i: (0, 0)),
            pl.BlockSpec((D_MODEL, IN_PROJ), lambda i: (0, 0)),
            pl.BlockSpec((1, GQ_WIDTH + GKV_WIDTH), lambda i: (0, 0)),
            rope_spec, rope_spec, rope_spec,
        ],
        out_specs=(pl.BlockSpec((TM_TOK, RW_PROJ), row), pl.BlockSpec((TM_TOK, NA_PROJ), row),
                   pl.BlockSpec((TM_TOK, GQ_WIDTH), row), pl.BlockSpec((TM_TOK, 2 * GKV_WIDTH), row)),
        compiler_params=_cparams(("arbitrary",)),
        name="in_proj",
    )(x, mods_l, g1, w_in_bf, qk_g, *rope)


def _softplus(x):
    return jnp.maximum(x, 0.0) + jnp.log(1.0 + jnp.exp(-jnp.abs(x)))


def _rw_pre(z, zprev, znext, shift_ref, kk_ref, ka_ref, rk_ref, w0_ref, wup_ref, a0_ref, aup_ref, d,
            ones_bd):
    rows = lax.broadcasted_iota(jnp.int32, z.shape, 0)
    zp = jnp.where(rows == 0, zprev, pltpu.roll(z, 1, 0))
    zn = jnp.where(rows == CHUNK - 1, znext, pltpu.roll(z, CHUNK - 1, 0))
    zs = zp * shift_ref[0:1, :] + z * shift_ref[1:2, :] + zn * shift_ref[2:3, :]
    r = zs[:, 0:RW_WIDTH]
    k = zs[:, RW_WIDTH:2 * RW_WIDTH]
    v = zs[:, 2 * RW_WIDTH:3 * RW_WIDTH]
    o = 3 * RW_WIDTH
    wd = zs[:, o:o + DECAY_LORA]
    ad = zs[:, o + DECAY_LORA:o + DECAY_LORA + ICLR_LORA]
    gd = zs[:, o + DECAY_LORA + ICLR_LORA:]
    kk = k * kk_ref[...]
    kk = kk / jnp.maximum(jnp.sqrt(_head_sum(kk * kk, ones_bd)), 1e-12)
    tw = jnp.tanh(wd)
    wl = w0_ref[d:d + 1, :] + _dot3(tw, wup_ref[d])
    lw = -jnp.exp(-_softplus(-wl) - 0.5)
    a_sig = _sigmoid(a0_ref[d:d + 1, :] + _dot3(ad, aup_ref[d]))
    k_d = k * (1.0 + (a_sig - 1.0) * ka_ref[...])
    bonus = _head_sum(r * k_d * rk_ref[...], ones_bd) * v
    return dict(r=r, k=k_d, v=v, a=-kk, b=kk * a_sig, lw=lw, bonus=bonus, gd=gd, ad=ad, k_raw=k)


def _chunk_masks(rev):
    t = lax.broadcasted_iota(jnp.int32, (CHUNK, CHUNK), 0)
    j = lax.broadcasted_iota(jnp.int32, (CHUNK, CHUNK), 1)
    return ((j >= t), (j > t)) if rev else ((j <= t), (j < t))


def _wkv_scale(p, incl, rev):
    lw = p["lw"]
    cs = _dot_exact_lhs(incl.astype(BF16), lw)
    tot = cs[0:1, :] if rev else cs[CHUNK - 1:CHUNK, :]
    e_inv = jnp.exp(-cs)
    e_rem = jnp.exp(tot - cs)
    bf = lambda x: x.astype(BF16)
    return dict(at=bf(p["a"] * jnp.exp(cs - lw)), rt=bf(p["r"] * jnp.exp(cs)),
                bt=bf(p["b"] * e_inv), kt=bf(p["k"] * e_inv),
                bh=bf(p["b"] * e_rem), kh=bf(p["k"] * e_rem), v=bf(p["v"]), gtot=jnp.exp(tot))


def _wkv_chunks(items):
    bf = lambda x: x.astype(BF16)
    c = CHUNK
    n_sq = int(np.log2(c))
    ar = [jnp.concatenate([it["at"], it["rt"]], 0) for it in items]
    m_b = [_bdot(a, it["bt"], NT_DIMS) for a, it in zip(ar, items)]
    m_k = [_bdot(a, it["kt"], NT_DIMS) for a, it in zip(ar, items)]
    s_bf = [bf(it["s"]) for it in items]
    xs = [bf(jnp.where(it["strict"], m[:c], 0.0)) for m, it in zip(m_b, items)]
    a_rb = [bf(jnp.where(it["incl"], m[c:], 0.0)) for m, it in zip(m_b, items)]
    a_ak = [bf(jnp.where(it["strict"], m[:c], 0.0)) for m, it in zip(m_k, items)]
    a_rk = [bf(jnp.where(it["incl"], m[c:], 0.0)) for m, it in zip(m_k, items)]
    ws = [_bdot(ak, it["v"]) + _bdot(it["at"], s, NT_DIMS) for ak, it, s in zip(a_ak, items, s_bf)]
    for step in range(n_sq):
        wb = [bf(w) for w in ws]
        ws = [w + _bdot(x, b) for w, x, b in zip(ws, xs, wb)]
        if step < n_sq - 1:
            xs = [bf(_bdot(x, x)) for x in xs]
    wb = [bf(w) for w in ws]
    ys = [_bdot(rb, w) + _bdot(rk, it["v"]) + _bdot(it["rt"], s, NT_DIMS)
          for rb, rk, w, it, s in zip(a_rb, a_rk, wb, items, s_bf)]
    s_new = [it["s"] * it["gtot"] + _bdot(w, it["bh"], TN_DIMS) + _bdot(it["v"], it["kh"], TN_DIMS)
             for w, it in zip(wb, items)]
    return ys, s_new


def _rwkv_kernel(zf_ref, zfp_ref, zfn_ref, zb_ref, zbp_ref, zbn_ref, s0_ref,
                 shift_ref, kk_ref, ka_ref, rk_ref, w0_ref, wup_ref, a0_ref, aup_ref, gup_ref,
                 *refs, n_chunks, n_group, has_init, emit_state, n_alias):
    refs = refs[n_alias:]
    if emit_state:
        yf_ref, yb_ref, bonus_ref, gate_ref, st_ref, h_ref = refs
    else:
        yf_ref, yb_ref, bonus_ref, gate_ref, h_ref = refs
        st_ref = None
    i = pl.program_id(1)
    ones_bd = _head_ones(RW_WIDTH)

    @pl.when(i == 0)
    def _():
        if has_init:
            h_ref[...] = s0_ref[...]
        else:
            h_ref[...] = jnp.zeros(h_ref.shape, F32)

    params = (shift_ref, kk_ref, ka_ref, rk_ref, w0_ref, wup_ref, a0_ref, aup_ref)
    first = i == 0
    last = i == n_chunks - 1
    zero_row = jnp.zeros((1, RW_PROJ), F32)
    masks = (_chunk_masks(False), _chunk_masks(True))
    items = []
    for g in range(n_group):
        pf = _rw_pre(zf_ref[g], jnp.where(first, zero_row, zfp_ref[g, 0, SUBLANES - 1:SUBLANES, :]),
                     jnp.where(last, zero_row, zfn_ref[g, 0, 0:1, :]), *params, 0, ones_bd)
        pb = _rw_pre(zb_ref[g], jnp.where(last, zero_row, zbp_ref[g, 0, SUBLANES - 1:SUBLANES, :]),
                     jnp.where(first, zero_row, zbn_ref[g, 0, 0:1, :]), *params, 1, ones_bd)
        a_sig_b = _sigmoid(a0_ref[1:2, :] + _dot3(pf["ad"], aup_ref[1]))
        k_b = pf["k_raw"] * (1.0 + (a_sig_b - 1.0) * ka_ref[...])
        bonus_ref[g] = pf["bonus"] + _head_sum(pf["r"] * k_b * rk_ref[...], ones_bd) * pf["v"]
        gate_ref[g] = _bdot(_sigmoid(pf["gd"]), gup_ref[...])
        for d, p in ((0, pf), (1, pb)):
            incl, strict = masks[d]
            sc = _wkv_scale(p, incl, d == 1)
            for h in range(RW_HEADS):
                sl = slice(h * HEAD_DIM, (h + 1) * HEAD_DIM)
                it = {k: v[:, sl] for k, v in sc.items()}
                it.update(s=h_ref[g, d, h], incl=incl, strict=strict, where=(g, d, h))
                items.append(it)

    ys, s_new = _wkv_chunks(items)
    for it, y, s in zip(items, ys, s_new):
        g, d, h = it["where"]
        y_ref = yb_ref if d else yf_ref
        y_ref[g, :, h * HEAD_DIM:(h + 1) * HEAD_DIM] = y
        h_ref[g, d, h] = s

    if emit_state:
        @pl.when(last)
        def _():
            for it, s in zip(items, s_new):
                g, d, h = it["where"]
                st_ref[g, d, h] = s


def _rwkv_call(zrw, s0, lp, prev_outs, *, n_seq, seq_len, row_base, has_init, emit_state):
    g = RW_SEQ_GROUP
    n_chunks = seq_len // CHUNK
    n_rows8 = seq_len // SUBLANES
    per8 = CHUNK // SUBLANES
    n_view = T_ALL // seq_len
    base_g = row_base // seq_len // g
    z3 = zrw.reshape(n_view, seq_len, RW_PROJ)
    z4 = zrw.reshape(n_view, n_rows8, SUBLANES, RW_PROJ)

    fwd = lambda i: i
    bwd = lambda i: n_chunks - 1 - i
    main = lambda c: pl.BlockSpec((g, CHUNK, RW_PROJ), lambda b, i: (base_g + b, c(i), 0))
    prev8 = lambda c: pl.BlockSpec((g, 1, SUBLANES, RW_PROJ),
                                   lambda b, i: (base_g + b, jnp.maximum(c(i) * per8 - 1, 0), 0, 0))
    next8 = lambda c: pl.BlockSpec((g, 1, SUBLANES, RW_PROJ),
                                   lambda b, i: (base_g + b, jnp.minimum((c(i) + 1) * per8, n_rows8 - 1), 0, 0))
    full = lambda shape: pl.BlockSpec(shape, lambda b, i: (0,) * len(shape))
    out_f = pl.BlockSpec((g, CHUNK, RW_WIDTH), lambda b, i: (base_g + b, i, 0))
    out_b = pl.BlockSpec((g, CHUNK, RW_WIDTH), lambda b, i: (base_g + b, bwd(i), 0))
    tok = jax.ShapeDtypeStruct((n_view, seq_len, RW_WIDTH), F32)
    out_shape = [tok, tok, tok, tok]
    out_specs = [out_f, out_b, out_f, out_f]
    state_block = (g, 2, RW_HEADS, HEAD_DIM, HEAD_DIM)
    if emit_state:
        out_shape.append(jax.ShapeDtypeStruct((n_seq, 2, RW_HEADS, HEAD_DIM, HEAD_DIM), F32))
        out_specs.append(pl.BlockSpec(state_block, lambda b, i: (b, 0, 0, 0, 0)))
    n_alias = len(prev_outs)
    n_in = 16
    aliases = {n_in + j: j for j in range(n_alias)}
    state_spec = pl.BlockSpec(state_block, lambda b, i: (b if has_init else 0, 0, 0, 0, 0))
    kern = functools.partial(_rwkv_kernel, n_chunks=n_chunks, n_group=g, has_init=has_init,
                             emit_state=emit_state, n_alias=n_alias)
    prev_views = [p.reshape(n_view, seq_len, RW_WIDTH) for p in prev_outs]
    outs = pl.pallas_call(
        kern,
        out_shape=tuple(out_shape),
        grid=(n_seq // g, n_chunks),
        in_specs=[main(fwd), prev8(fwd), next8(fwd), main(bwd), prev8(bwd), next8(bwd), state_spec,
                  full((3, RW_PROJ)), full((1, RW_WIDTH)), full((1, RW_WIDTH)), full((1, RW_WIDTH)),
                  full((2, RW_WIDTH)), full((2, DECAY_LORA, RW_WIDTH)),
                  full((2, RW_WIDTH)), full((2, ICLR_LORA, RW_WIDTH)), full((GATE_LORA, RW_WIDTH))]
                 + [pl.BlockSpec(memory_space=pl.ANY)] * n_alias,
        out_specs=tuple(out_specs),
        scratch_shapes=[pltpu.VMEM(state_block, F32)],
        input_output_aliases=aliases,
        compiler_params=_cparams(("arbitrary", "arbitrary")),
        name="rwkv_scan_init" if has_init else "rwkv_scan_zero",
    )(z3, z4, z4, z3, z4, z4, s0, lp["rw_shift"], lp["rw_k_k"], lp["rw_k_a"], lp["rw_r_k"],
      lp["rw_w0"], lp["rw_w_up"], lp["rw_a0"], lp["rw_a_up"], lp["rw_g_up"], *prev_views)
    return tuple(o.reshape(T_ALL, RW_WIDTH) for o in outs[:4]) + tuple(outs[4:])


def _softmax_pv(scores, values):
    m = scores[0].max(axis=-1, keepdims=True)
    for s in scores[1:]:
        m = jnp.maximum(m, s.max(axis=-1, keepdims=True))
    es = [jnp.exp(s - m) for s in scores]
    l = es[0].sum(axis=-1, keepdims=True)
    for e in es[1:]:
        l = l + e.sum(axis=-1, keepdims=True)
    o = _bdot(es[0], values[0])
    for e, v in zip(es[1:], values[1:]):
        o = o + _bdot(e, v)
    return o * (1.0 / l)


def _head(x, h):
    return x[:, h * HEAD_DIM:(h + 1) * HEAD_DIM]


def _ctx_attn_kernel(zna_ref, zgq_ref, zgkv_ref, att_ref, nk_ref, nv_ref, gk_ref, gv_ref):
    zna = zna_ref[...]
    q, k, v = zna[:, :NA_WIDTH], zna[:, NA_WIDTH:2 * NA_WIDTH], zna[:, 2 * NA_WIDTH:]
    nk_ref[0] = k
    nv_ref[0] = v
    gq = zgq_ref[...]
    gkv = zgkv_ref[...]
    gk, gv = gkv[:, :GKV_WIDTH], gkv[:, GKV_WIDTH:]
    gk_ref[0] = gk
    gv_ref[0] = gv
    gq, gk, gv = (gq * ATT_SCALE).astype(BF16), gk.astype(BF16), gv.astype(BF16)
    q, k, v = (q * ATT_SCALE).astype(BF16), k.astype(BF16), v.astype(BF16)
    for h in range(GQA_Q_HEADS):
        kv = h // GQA_GROUP
        s = _bdot(_head(gq, h), _head(gk, kv), NT_DIMS)
        att_ref[:, h * HEAD_DIM:(h + 1) * HEAD_DIM] = _softmax_pv([s], [_head(gv, kv)]).astype(BF16)
    for h in range(NA_HEADS):
        s = _bdot(_head(q, h), _head(k, h), NT_DIMS)
        o = _softmax_pv([s], [_head(v, h)])
        att_ref[:, GQ_WIDTH + h * HEAD_DIM:GQ_WIDTH + (h + 1) * HEAD_DIM] = o.astype(BF16)


def _ctx_attn_call(zna, zgq, zgkv):
    row = lambda b: (b, 0)
    bat = lambda b: (b, 0, 0)
    return pl.pallas_call(
        _ctx_attn_kernel,
        out_shape=(jax.ShapeDtypeStruct((T_ALL, ATT_WIDTH), BF16),
                   jax.ShapeDtypeStruct((BATCH, SEQ, NA_WIDTH), F32),
                   jax.ShapeDtypeStruct((BATCH, SEQ, NA_WIDTH), F32),
                   jax.ShapeDtypeStruct((BATCH, SEQ, GKV_WIDTH), F32),
                   jax.ShapeDtypeStruct((BATCH, SEQ, GKV_WIDTH), F32)),
        grid=(BATCH,),
        in_specs=[pl.BlockSpec((SEQ, NA_PROJ), row), pl.BlockSpec((SEQ, GQ_WIDTH), row),
                  pl.BlockSpec((SEQ, 2 * GKV_WIDTH), row)],
        out_specs=(pl.BlockSpec((SEQ, ATT_WIDTH), row),
                   pl.BlockSpec((1, SEQ, NA_WIDTH), bat), pl.BlockSpec((1, SEQ, NA_WIDTH), bat),
                   pl.BlockSpec((1, SEQ, GKV_WIDTH), bat), pl.BlockSpec((1, SEQ, GKV_WIDTH), bat)),
        compiler_params=_cparams(("arbitrary",)),
        name="ctx_attention",
    )(zna, zgq, zgkv)


def _lat_na_kernel(zna_ref, ck_ref, cv_ref, tb_ref, att_in_ref, att_ref):
    del att_in_ref
    i = pl.program_id(1)
    start = jnp.clip(i - NA_WIN_ROWS // 2, 0, GRID_H - NA_WIN_ROWS)
    n_loc = NA_WIN_ROWS * GRID_W
    q = zna_ref[pl.ds(pl.multiple_of(i * GRID_W, GRID_W), GRID_W), 0:NA_WIDTH]
    q = (q * ATT_SCALE).astype(BF16)
    w0 = pl.multiple_of(start * GRID_W, GRID_W)
    kwin = zna_ref[pl.ds(w0, n_loc), NA_WIDTH:2 * NA_WIDTH].astype(BF16)
    vwin = zna_ref[pl.ds(w0, n_loc), 2 * NA_WIDTH:3 * NA_WIDTH].astype(BF16)
    kc = ck_ref[0].astype(BF16)
    vc = cv_ref[0].astype(BF16)
    dr0 = start - i + NA_WIN_ROWS - 1
    outs = []
    for h in range(NA_HEADS):
        bias = jnp.concatenate([tb_ref[h, dr0 + r] for r in range(NA_WIN_ROWS)], axis=1)
        qh = _head(q, h)
        s_loc = _bdot(qh, _head(kwin, h), NT_DIMS) + bias
        s_ctx = _bdot(qh, _head(kc, h), NT_DIMS)
        outs.append(_softmax_pv([s_loc, s_ctx], [_head(vwin, h), _head(vc, h)]))
    att_ref[...] = jnp.concatenate(outs, axis=1).astype(BF16)


def _na_bias_table(rpb):
    w = np.arange(GRID_W)[:, None]
    kc = np.arange(GRID_W)[None, :]
    cs = np.clip(w - NA_WIN_COLS // 2, 0, GRID_W - NA_WIN_COLS)
    valid = (kc >= cs) & (kc < cs + NA_WIN_COLS)
    off = np.clip(kc - w + NA_WIN_COLS - 1, 0, 2 * NA_WIN_COLS - 2)
    tb = rpb[:, :, off]
    return jnp.where(jnp.asarray(valid)[None, None], tb, NEG_BIG).astype(F32)


def _lat_na_call(zna, ck, cv, tb, att):
    lat_blk = T_CTX // DEC_SEQ
    first_row = T_CTX // GRID_W
    return pl.pallas_call(
        _lat_na_kernel,
        out_shape=jax.ShapeDtypeStruct((T_ALL, ATT_WIDTH), BF16),
        grid=(DEC_BATCH, GRID_H),
        in_specs=[pl.BlockSpec((DEC_SEQ, NA_PROJ), lambda b, i: (lat_blk + b, 0)),
                  pl.BlockSpec((1, PAST_LEN, NA_WIDTH), lambda b, i: (b, 0, 0)),
                  pl.BlockSpec((1, PAST_LEN, NA_WIDTH), lambda b, i: (b, 0, 0)),
                  pl.BlockSpec((NA_HEADS, 2 * NA_WIN_ROWS - 1, GRID_W, GRID_W), lambda b, i: (0, 0, 0, 0)),
                  pl.BlockSpec(memory_space=pl.ANY)],
        out_specs=pl.BlockSpec((GRID_W, NA_WIDTH),
                               lambda b, i: (first_row + b * GRID_H + i, GQ_WIDTH // NA_WIDTH)),
        input_output_aliases={4: 0},
        compiler_params=_cparams(("arbitrary", "arbitrary")),
        name="latent_neighbourhood_attention",
    )(zna, ck, cv, tb, att)


def _lat_gqa_kernel(zgq_ref, zgkv_ref, ck_ref, cv_ref, att_in_ref, att_ref):
    del att_in_ref
    q = (zgq_ref[...] * ATT_SCALE).astype(BF16)
    kv = zgkv_ref[...].astype(BF16)
    kl, vl = kv[:, :GKV_WIDTH], kv[:, GKV_WIDTH:]
    kc = ck_ref[0].astype(BF16)
    vc = cv_ref[0].astype(BF16)
    for h in range(GQA_Q_HEADS):
        g = h // GQA_GROUP
        qh = _head(q, h)
        s_c = _bdot(qh, _head(kc, g), NT_DIMS)
        s_l = _bdot(qh, _head(kl, g), NT_DIMS)
        o = _softmax_pv([s_c, s_l], [_head(vc, g), _head(vl, g)])
        att_ref[:, h * HEAD_DIM:(h + 1) * HEAD_DIM] = o.astype(BF16)


def _lat_gqa_call(zgq, zgkv, ck, cv, att):
    n_q = DEC_SEQ // TQ_GQA
    first_q = T_CTX // TQ_GQA
    lat_blk = T_CTX // DEC_SEQ
    return pl.pallas_call(
        _lat_gqa_kernel,
        out_shape=jax.ShapeDtypeStruct((T_ALL, ATT_WIDTH), BF16),
        grid=(DEC_BATCH, n_q),
        in_specs=[pl.BlockSpec((TQ_GQA, GQ_WIDTH), lambda b, j: (first_q + b * n_q + j, 0)),
                  pl.BlockSpec((DEC_SEQ, 2 * GKV_WIDTH), lambda b, j: (lat_blk + b, 0)),
                  pl.BlockSpec((1, PAST_LEN, GKV_WIDTH), lambda b, j: (b, 0, 0)),
                  pl.BlockSpec((1, PAST_LEN, GKV_WIDTH), lambda b, j: (b, 0, 0)),
                  pl.BlockSpec(memory_space=pl.ANY)],
        out_specs=pl.BlockSpec((TQ_GQA, GQ_WIDTH), lambda b, j: (first_q + b * n_q + j, 0)),
        input_output_aliases={4: 0},
        compiler_params=_cparams(("arbitrary", "arbitrary")),
        name="latent_gqa_attention",
    )(zgq, zgkv, ck, cv, att)


def _out_proj_kernel(x_ref, yf_ref, yb_ref, bonus_ref, gate_ref, att_ref, mod_ref, lng_ref, lnb_ref,
                     wout_ref, g2_ref, rw_ref, rb_ref,
                     x1_ref, h2_ref, slot_ref, gates_ref, cnt_ref):
    ones_bd = _head_ones(RW_WIDTH)
    o = yf_ref[...] + yb_ref[...]
    mu = _head_sum(o, ones_bd) * (1.0 / HEAD_DIM)
    dlt = o - mu
    var = _head_sum(dlt * dlt, ones_bd) * (1.0 / HEAD_DIM)
    ln = dlt * lax.rsqrt(var + LNX_EPS) * lng_ref[...] + lnb_ref[...]
    rw = ((ln + bonus_ref[...]) * gate_ref[...]).astype(BF16)
    att = att_ref[...]
    mix = (jnp.dot(rw, wout_ref[0:RW_WIDTH, :], preferred_element_type=F32)
           + jnp.dot(att[:, :GQ_WIDTH], wout_ref[RW_WIDTH + NA_WIDTH:, :], preferred_element_type=F32)
           + jnp.dot(att[:, GQ_WIDTH:], wout_ref[RW_WIDTH:RW_WIDTH + NA_WIDTH, :],
                     preferred_element_type=F32))
    gate1 = mod_ref[0, 2:3, :]
    shift2 = mod_ref[0, 3:4, :]
    scale2 = mod_ref[0, 4:5, :]
    x1 = x_ref[...] + gate1 * mix
    x1_ref[...] = x1
    h2 = _rms(x1, g2_ref[...]) * (1.0 + scale2) + shift2
    h2_ref[...] = h2.astype(BF16)

    logits = _dot3(h2, rw_ref[...]) + rb_ref[...]
    tm = logits.shape[0]
    col = lax.broadcasted_iota(jnp.int32, (tm, N_EXPERTS), 1)
    lane4 = lax.broadcasted_iota(jnp.int32, (tm, TOP_K), 1)
    work = logits
    sels, vals = [], []
    for k in range(TOP_K):
        m = work.max(axis=-1, keepdims=True)
        idx = jnp.min(jnp.where(work == m, col, N_EXPERTS), axis=-1, keepdims=True)
        sel = col == idx
        sels.append(sel)
        vals.append(m)
        work = jnp.where(sel, -jnp.inf, work)
    es = [jnp.exp(v - vals[0]) for v in vals]
    inv = 1.0 / (es[0] + es[1] + es[2] + es[3])
    gates = jnp.zeros((tm, TOP_K), F32)
    for k in range(TOP_K):
        gates = jnp.where(lane4 == k, es[k] * inv, gates)
    assign = jnp.zeros((tm, N_EXPERTS), F32)
    for sel in sels:
        assign = assign + sel.astype(F32)
    r_i = lax.broadcasted_iota(jnp.int32, (tm, tm), 0)
    c_i = lax.broadcasted_iota(jnp.int32, (tm, tm), 1)
    before = jnp.dot((c_i < r_i).astype(BF16), assign.astype(BF16), preferred_element_type=F32)
    cnt = jnp.sum(assign, axis=0, keepdims=True)
    seg_units = jnp.floor((cnt + (SEG_ALIGN - 1)) * (1.0 / SEG_ALIGN))
    e_r = lax.broadcasted_iota(jnp.int32, (N_EXPERTS, N_EXPERTS), 0)
    e_c = lax.broadcasted_iota(jnp.int32, (N_EXPERTS, N_EXPERTS), 1)
    loc = SEG_ALIGN * jnp.dot(seg_units.astype(BF16), (e_r < e_c).astype(BF16),
                              preferred_element_type=F32)
    pos = before + loc
    slot = jnp.zeros((tm, TOP_K), F32)
    for k in range(TOP_K):
        sk = jnp.sum(jnp.where(sels[k], pos, 0.0), axis=-1, keepdims=True)
        slot = jnp.where(lane4 == k, sk, slot)
    slot_ref[...] = slot.astype(jnp.int32)
    gates_ref[...] = gates
    cnt_ref[0] = cnt.astype(jnp.int32)


def _out_proj_call(x, yf, yb, bonus, gate, att, mods_l, lp):
    n_blk = T_ALL // TM_TOK
    row = lambda i: (i, 0)
    full2 = lambda r, c: pl.BlockSpec((r, c), lambda i: (0, 0))
    tokw = lambda w: pl.BlockSpec((TM_TOK, w), row)
    return pl.pallas_call(
        _out_proj_kernel,
        out_shape=(jax.ShapeDtypeStruct((T_ALL, D_MODEL), F32),
                   jax.ShapeDtypeStruct((T_ALL, D_MODEL), BF16),
                   jax.ShapeDtypeStruct((T_ALL, TOP_K), jnp.int32),
                   jax.ShapeDtypeStruct((T_ALL, TOP_K), F32),
                   jax.ShapeDtypeStruct((n_blk, 1, N_EXPERTS), jnp.int32)),
        grid=(n_blk,),
        in_specs=[tokw(D_MODEL), tokw(RW_WIDTH), tokw(RW_WIDTH), tokw(RW_WIDTH), tokw(RW_WIDTH),
                  tokw(ATT_WIDTH),
                  pl.BlockSpec((1, N_MOD, D_MODEL), lambda i: (_group_of_block(i, TM_TOK), 0, 0)),
                  full2(1, RW_WIDTH), full2(1, RW_WIDTH), full2(D_MODEL, D_MODEL), full2(1, D_MODEL),
                  full2(D_MODEL, N_EXPERTS), full2(1, N_EXPERTS)],
        out_specs=(tokw(D_MODEL), tokw(D_MODEL), tokw(TOP_K), tokw(TOP_K),
                   pl.BlockSpec((1, 1, N_EXPERTS), lambda i: (i, 0, 0))),
        compiler_params=_cparams(("arbitrary",)),
        name="out_proj_router",
    )(x, yf, yb, bonus, gate, att, mods_l, lp["rw_ln_g"], lp["rw_ln_b"], lp["w_out"], lp["norm2_g"],
      lp["router_w"], lp["router_b"])


def _for_each_piece(n_units, max_bit, fn):
    for b in range(max_bit, -1, -1):
        @pl.when(((n_units >> b) & 1) == 1)
        def _(b=b):
            off = ((n_units >> (b + 1)) << (b + 1)) * SEG_ALIGN
            fn(pl.multiple_of(off, SEG_ALIGN), SEG_ALIGN << b)


def _segment_copies(local_ref, sorted_hbm, sem, blk, seg_ref, loc_ref, dst_ref, total_ref, to_sorted, wait):
    def copy(loc, dst, size):
        a = local_ref.at[pl.ds(pl.multiple_of(loc, SEG_ALIGN), size), :]
        b = sorted_hbm.at[pl.ds(pl.multiple_of(dst, SEG_ALIGN), size), :]
        return pltpu.make_async_copy(a, b, sem) if to_sorted else pltpu.make_async_copy(b, a, sem)

    if wait:
        _for_each_piece(total_ref[blk], TOTAL_MAX_BIT, lambda off, size: copy(0, 0, size).wait())
        return

    def body(e, carry):
        t = blk * N_EXPERTS + e
        loc = loc_ref[t]
        dst = dst_ref[t]
        _for_each_piece(seg_ref[t], SEG_MAX_BIT, lambda off, size: copy(loc + off, dst + off, size).start())
        return carry
    lax.fori_loop(0, N_EXPERTS, body, 0)


def _dispatch_kernel(seg_ref, loc_ref, dst_ref, total_ref, tail_ref, tail_dst_ref, h2_ref, slot_ref, xs_hbm,
                     xs_local, zero_buf, sem, zsem):
    j = pl.program_id(0)
    n = pl.num_programs(0)
    buf = j % 2

    @pl.when(j == 0)
    def _():
        zero_buf[...] = jnp.zeros(zero_buf.shape, BF16)

        def tails(wait):
            def body(e, carry):
                def piece(off, size):
                    cp = pltpu.make_async_copy(
                        zero_buf.at[pl.ds(0, size), :],
                        xs_hbm.at[pl.ds(pl.multiple_of(tail_dst_ref[e] + off, SEG_ALIGN), size), :], zsem)
                    if wait:
                        cp.wait()
                    else:
                        cp.start()
                _for_each_piece(tail_ref[e], TAIL_MAX_BIT, piece)
                return carry
            lax.fori_loop(0, N_EXPERTS, body, 0)
        tails(False)
        tails(True)

    lane = lax.broadcasted_iota(jnp.int32, (TM_TOK, LANES), 1)
    slots = slot_ref[...].astype(F32)
    wide = jnp.zeros((TM_TOK, LANES), F32)
    for k in range(TOP_K):
        wide = jnp.where(lane == k, slots[:, k:k + 1], wide)
    slot_rows = wide.T
    h2 = h2_ref[...]
    for c in range(LOCAL_ROWS // TM_TOK):
        row = lax.broadcasted_iota(jnp.int32, (TM_TOK, TM_TOK), 0).astype(F32) + float(c * TM_TOK)
        onehot = jnp.zeros((TM_TOK, TM_TOK), F32)
        for k in range(TOP_K):
            onehot = jnp.where(row == slot_rows[k:k + 1, :], 1.0, onehot)
        onehot = onehot.astype(BF16)
        xs_local[buf, c * TM_TOK:(c + 1) * TM_TOK, :] = jnp.dot(
            onehot, h2, preferred_element_type=F32).astype(BF16)

    tabs = (seg_ref, loc_ref, dst_ref, total_ref)

    @pl.when(j > 0)
    def _():
        _segment_copies(xs_local.at[1 - buf], xs_hbm, sem.at[1 - buf], j - 1, *tabs, True, True)

    _segment_copies(xs_local.at[buf], xs_hbm, sem.at[buf], j, *tabs, True, False)

    @pl.when(j == n - 1)
    def _():
        _segment_copies(xs_local.at[buf], xs_hbm, sem.at[buf], j, *tabs, True, True)


def _dispatch_call(tabs, h2, slot):
    n_blk = T_ALL // TM_TOK
    row = lambda j, *_: (j, 0)
    grid_spec = pltpu.PrefetchScalarGridSpec(
        num_scalar_prefetch=6,
        grid=(n_blk,),
        in_specs=[pl.BlockSpec((TM_TOK, D_MODEL), row), pl.BlockSpec((TM_TOK, TOP_K), row)],
        out_specs=pl.BlockSpec(memory_space=pl.ANY),
        scratch_shapes=[pltpu.VMEM((2, LOCAL_ROWS, D_MODEL), BF16), pltpu.VMEM((TM_EXP, D_MODEL), BF16),
                        pltpu.SemaphoreType.DMA((2,)), pltpu.SemaphoreType.DMA],
    )
    return pl.pallas_call(
        _dispatch_kernel,
        out_shape=jax.ShapeDtypeStruct((N_SLOTS, D_MODEL), BF16),
        grid_spec=grid_spec,
        compiler_params=_cparams(("arbitrary",)),
        name="moe_dispatch",
    )(tabs["seg"], tabs["loc"], tabs["dst"], tabs["total"], tabs["tail"], tabs["tail_dst"], h2, slot)


def _expert_kernel(be_ref, first_ref, par_ref, nexte_ref, meta_ref, x_ref, wgu_hbm, bgu_ref, wd_hbm, bd_ref,
                   y_ref, wgu_f, wd_f, wgu_bf, wd_bf, sem, *, layer):
    i = pl.program_id(0)
    n_used = meta_ref[0]

    def weight_copies(e, buf):
        return (pltpu.make_async_copy(wgu_hbm.at[layer, e], wgu_f.at[buf], sem.at[buf]),
                pltpu.make_async_copy(wd_hbm.at[layer, e], wd_f.at[buf], sem.at[buf]))

    @pl.when(i < n_used)
    def _():
        e = be_ref[i]
        buf = par_ref[i]

        @pl.when(first_ref[i] == 1)
        def _():
            @pl.when(i == 0)
            def _():
                for cp in weight_copies(e, buf):
                    cp.start()
            for cp in weight_copies(e, buf):
                cp.wait()
            wgu_bf[...] = wgu_f[buf].astype(BF16)
            wd_bf[...] = wd_f[buf].astype(BF16)
            nxt = nexte_ref[i]

            @pl.when(nxt >= 0)
            def _():
                for cp in weight_copies(nxt, 1 - buf):
                    cp.start(priority=1)

        gu = jnp.dot(x_ref[...], wgu_bf[...], preferred_element_type=F32) + bgu_ref[0, 0]
        glu = jnp.minimum(gu[:, :D_FF], SWIGLU_LIMIT)
        lin = jnp.clip(gu[:, D_FF:], -SWIGLU_LIMIT, SWIGLU_LIMIT)
        act = glu * _sigmoid(SWIGLU_ALPHA * glu) * (lin + 1.0)
        y = jnp.dot(act.astype(BF16), wd_bf[...], preferred_element_type=F32) + bd_ref[0, 0]
        y_ref[...] = y.astype(BF16)


def _expert_call(tabs, xs, w_gu, b_gu, w_down, b_down, layer):
    def bmap(i, be, first, par, nxt, meta_):
        return (layer, be[i], 0, 0)

    def rmap(i, be, first, par, nxt, meta_):
        return (jnp.minimum(i, meta_[0] - 1), 0)

    grid_spec = pltpu.PrefetchScalarGridSpec(
        num_scalar_prefetch=5,
        grid=(N_SLOT_BLOCKS,),
        in_specs=[pl.BlockSpec((TM_EXP, D_MODEL), rmap),
                  pl.BlockSpec(memory_space=pl.ANY),
                  pl.BlockSpec((1, 1, 1, 2 * D_FF), bmap),
                  pl.BlockSpec(memory_space=pl.ANY),
                  pl.BlockSpec((1, 1, 1, D_MODEL), bmap)],
        out_specs=pl.BlockSpec((TM_EXP, D_MODEL), rmap),
        scratch_shapes=[pltpu.VMEM((2, D_MODEL, 2 * D_FF), F32), pltpu.VMEM((2, D_FF, D_MODEL), F32),
                        pltpu.VMEM((D_MODEL, 2 * D_FF), BF16), pltpu.VMEM((D_FF, D_MODEL), BF16),
                        pltpu.SemaphoreType.DMA((2,))],
    )
    return pl.pallas_call(
        functools.partial(_expert_kernel, layer=layer),
        out_shape=jax.ShapeDtypeStruct((N_SLOTS, D_MODEL), BF16),
        grid_spec=grid_spec,
        compiler_params=_cparams(("arbitrary",)),
        name="moe_experts",
    )(tabs["block_e"], tabs["first"], tabs["parity"], tabs["next_e"], tabs["n_used"], xs, w_gu,
      b_gu.reshape(DEPTH, N_EXPERTS, 1, 2 * D_FF), w_down, b_down.reshape(DEPTH, N_EXPERTS, 1, D_MODEL))


def _combine_kernel(seg_ref, loc_ref, dst_ref, total_ref, ys_hbm, x1_ref, slot_ref, gates_ref, mod_ref,
                    fg_ref, o_ref, ybuf, sem, *, final):
    j = pl.program_id(0)
    n = pl.num_programs(0)
    buf = j % 2

    tabs = (seg_ref, loc_ref, dst_ref, total_ref)

    @pl.when(j == 0)
    def _():
        _segment_copies(ybuf.at[0], ys_hbm, sem.at[0], 0, *tabs, False, False)

    @pl.when(j + 1 < n)
    def _():
        _segment_copies(ybuf.at[1 - buf], ys_hbm, sem.at[1 - buf], j + 1, *tabs, False, False)

    _segment_copies(ybuf.at[buf], ys_hbm, sem.at[buf], j, *tabs, False, True)

    slots = slot_ref[...]
    gates = gates_ref[...]
    total = total_ref[j] * SEG_ALIGN
    ff = jnp.zeros((TM_TOK, D_MODEL), F32)
    for c in range(LOCAL_ROWS // TM_TOK):
        col = lax.broadcasted_iota(jnp.int32, (TM_TOK, TM_TOK), 1) + c * TM_TOK
        q = jnp.zeros((TM_TOK, TM_TOK), F32)
        for k in range(TOP_K):
            q = jnp.where(col == slots[:, k:k + 1], gates[:, k:k + 1], q)
        rows = lax.broadcasted_iota(jnp.int32, (TM_TOK, D_MODEL), 0) + c * TM_TOK
        y = jnp.where(rows < total, ybuf[buf, c * TM_TOK:(c + 1) * TM_TOK, :], jnp.zeros((), BF16))
        ff = ff + jnp.dot(q.astype(BF16), y, preferred_element_type=F32)
    x = x1_ref[...] + mod_ref[0, 5:6, :] * ff
    if final:
        x = _rms(x, fg_ref[...])
    o_ref[...] = x


def _combine_call(tabs, ys, x1, slot, gates, mods_l, final_g, final):
    n_blk = T_ALL // TM_TOK
    row = lambda j, *_: (j, 0)
    grid_spec = pltpu.PrefetchScalarGridSpec(
        num_scalar_prefetch=4,
        grid=(n_blk,),
        in_specs=[pl.BlockSpec(memory_space=pl.ANY),
                  pl.BlockSpec((TM_TOK, D_MODEL), row),
                  pl.BlockSpec((TM_TOK, TOP_K), row),
                  pl.BlockSpec((TM_TOK, TOP_K), row),
                  pl.BlockSpec((1, N_MOD, D_MODEL), lambda j, *_: (_group_of_block(j, TM_TOK), 0, 0)),
                  pl.BlockSpec((1, D_MODEL), lambda j, *_: (0, 0))],
        out_specs=pl.BlockSpec((TM_TOK, D_MODEL), row),
        scratch_shapes=[pltpu.VMEM((2, LOCAL_ROWS, D_MODEL), BF16), pltpu.SemaphoreType.DMA((2,))],
    )
    return pl.pallas_call(
        functools.partial(_combine_kernel, final=final),
        out_shape=jax.ShapeDtypeStruct((T_ALL, D_MODEL), F32),
        grid_spec=grid_spec,
        compiler_params=_cparams(("arbitrary",)),
        name="moe_combine_final" if final else "moe_combine",
    )(tabs["seg"], tabs["loc"], tabs["dst"], tabs["total"], ys, x1, slot, gates, mods_l, final_g)


def _routing_tables(cnt):
    i32 = jnp.int32
    cnt = cnt.reshape(T_ALL // TM_TOK, N_EXPERTS)
    seg = (cnt + SEG_ALIGN - 1) // SEG_ALIGN * SEG_ALIGN
    loc = jnp.cumsum(seg, axis=1) - seg
    total = jnp.sum(seg, axis=1)
    rows_e = jnp.sum(seg, axis=0)
    region = (rows_e + TM_EXP - 1) // TM_EXP * TM_EXP
    region_end = jnp.cumsum(region)
    base = region_end - region
    dst = base[None, :] + jnp.cumsum(seg, axis=0) - seg
    n_used = jnp.maximum(region_end[-1] // TM_EXP, 1).astype(i32)
    blk = jnp.minimum(jnp.arange(N_SLOT_BLOCKS, dtype=i32), n_used - 1)
    block_e = jnp.sum((region_end[None, :] <= (blk * TM_EXP)[:, None]).astype(i32), axis=1)
    block_e = jnp.minimum(block_e, N_EXPERTS - 1).astype(i32)
    first = jnp.concatenate([jnp.ones((1,), i32), (block_e[1:] != block_e[:-1]).astype(i32)])
    parity = (jnp.cumsum(first) - 1) % 2
    e_ids = jnp.arange(N_EXPERTS, dtype=i32)
    later_used = (e_ids[None, :] > e_ids[:, None]) & (region[None, :] > 0)
    next_used = jnp.min(jnp.where(later_used, e_ids[None, :], N_EXPERTS), axis=1)
    next_e = next_used[block_e]
    next_e = jnp.where(next_e < N_EXPERTS, next_e, -1)
    flat = lambda a: a.reshape(-1).astype(i32)
    return dict(seg=flat(seg // SEG_ALIGN), loc=flat(loc), dst=flat(dst), total=flat(total // SEG_ALIGN),
                tail=flat((region - rows_e) // SEG_ALIGN), tail_dst=flat(base + rows_e),
                block_e=block_e, first=first, parity=flat(parity), next_e=flat(next_e),
                n_used=n_used.reshape(1))


def kernel(x_prompt, x_sample, cache_na_k, cache_na_v, cache_gqa_k, cache_gqa_v, state_rwkv, c, c_ctx,
           w_mod, b_mod, norm1_g, norm2_g, w_in, rw_shift, rw_w0, rw_w_up, rw_a0, rw_a_up, rw_g_up,
           rw_k_k, rw_k_a, rw_r_k, rw_ln_g, rw_ln_b, na_rpb, q_norm, k_norm, w_out, router_w, router_b,
           moe_w_gu, moe_b_gu, moe_w_down, moe_b_down, final_norm_g):
    x = jnp.concatenate([x_prompt.reshape(T_CTX, D_MODEL), x_sample.reshape(T_LAT, D_MODEL)], axis=0)
    cvecs = jnp.concatenate([c_ctx[None, :], c, jnp.zeros((SUBLANES - N_GROUPS, D_MODEL), F32)], axis=0)
    mods = _mods_call(cvecs, w_mod, b_mod)
    mods = mods[:, :N_GROUPS].reshape(DEPTH, N_GROUPS, N_MOD, D_MODEL)
    rope = _rope_tables()
    w_in_bf = w_in.astype(BF16)
    w_out_bf = w_out.astype(BF16)
    rw_g_up_bf = rw_g_up.astype(BF16)
    final_g = final_norm_g.reshape(1, D_MODEL)

    na_k_l, na_v_l, g_k_l, g_v_l, st_l = [], [], [], [], []
    for l in range(DEPTH):
        lp = {
            "rw_shift": rw_shift[l], "rw_k_k": rw_k_k[l].reshape(1, RW_WIDTH),
            "rw_k_a": rw_k_a[l].reshape(1, RW_WIDTH), "rw_r_k": rw_r_k[l].reshape(1, RW_WIDTH),
            "rw_w0": rw_w0[l], "rw_w_up": rw_w_up[l], "rw_a0": rw_a0[l], "rw_a_up": rw_a_up[l],
            "rw_g_up": rw_g_up_bf[l], "rw_ln_g": rw_ln_g[l].reshape(1, RW_WIDTH),
            "rw_ln_b": rw_ln_b[l].reshape(1, RW_WIDTH), "w_out": w_out_bf[l],
            "norm2_g": norm2_g[l].reshape(1, D_MODEL), "router_w": router_w[l],
            "router_b": router_b[l].reshape(1, N_EXPERTS),
        }
        qk_g = jnp.concatenate([jnp.tile(q_norm[l], GQA_Q_HEADS), jnp.tile(k_norm[l], GQA_KV_HEADS)])
        zrw, zna, zgq, zgkv = _in_proj_call(x, mods[l], norm1_g[l].reshape(1, D_MODEL), w_in_bf[l],
                                            qk_g.reshape(1, GQ_WIDTH + GKV_WIDTH), rope)

        s0_lat = state_rwkv[:, l]
        rw_ctx = _rwkv_call(zrw, s0_lat, lp, (), n_seq=BATCH, seq_len=SEQ, row_base=0,
                            has_init=False, emit_state=True)
        yf, yb, bonus, gate = _rwkv_call(zrw, s0_lat, lp, rw_ctx[:4], n_seq=DEC_BATCH, seq_len=DEC_SEQ,
                                         row_base=T_CTX, has_init=True, emit_state=False)
        st_l.append(rw_ctx[4])

        att, nk, nv, gk, gv = _ctx_attn_call(zna, zgq, zgkv)
        na_k_l.append(nk)
        na_v_l.append(nv)
        g_k_l.append(gk)
        g_v_l.append(gv)
        att = _lat_na_call(zna, cache_na_k[:, l].reshape(DEC_BATCH, PAST_LEN, NA_WIDTH),
                           cache_na_v[:, l].reshape(DEC_BATCH, PAST_LEN, NA_WIDTH),
                           _na_bias_table(na_rpb[l]), att)
        att = _lat_gqa_call(zgq, zgkv, cache_gqa_k[:, l].reshape(DEC_BATCH, PAST_LEN, GKV_WIDTH),
                            cache_gqa_v[:, l].reshape(DEC_BATCH, PAST_LEN, GKV_WIDTH), att)

        x1, h2, slot, gates, counts = _out_proj_call(x, yf, yb, bonus, gate, att, mods[l], lp)
        tabs = _routing_tables(counts)
        xs = _dispatch_call(tabs, h2, slot)
        ys = _expert_call(tabs, xs, moe_w_gu, moe_b_gu, moe_w_down, moe_b_down, l)
        x = _combine_call(tabs, ys, x1, slot, gates, mods[l], final_g, l == DEPTH - 1)

    y_prompt = x[:T_CTX].reshape(BATCH, SEQ, D_MODEL)
    y_sample = x[T_CTX:].reshape(DEC_BATCH, DEC_SEQ, D_MODEL)
    heads = lambda ts, n: jnp.stack(ts, axis=1).reshape(BATCH, DEPTH, SEQ, n, HEAD_DIM)
    return (y_prompt, y_sample, heads(na_k_l, NA_HEADS), heads(na_v_l, NA_HEADS),
            heads(g_k_l, GQA_KV_HEADS), heads(g_v_l, GQA_KV_HEADS), jnp.stack(st_l, axis=1))
```

```python
import functools

import numpy as np
import jax
import jax.numpy as jnp
from jax import lax
from jax.experimental import pallas as pl
from jax.experimental.pallas import tpu as pltpu

F32 = jnp.float32
BF16 = jnp.bfloat16

D_MODEL = 1024
BATCH = 32
SEQ = 256
DEPTH = 4
DEC_BATCH = 2
DEC_SEQ = 2048
PAST_LEN = 512
GRID_W = 64
GRID_H = DEC_SEQ // GRID_W
HEAD_DIM = 64
RW_HEADS = 4
RW_WIDTH = RW_HEADS * HEAD_DIM
DECAY_LORA = 64
ICLR_LORA = 64
GATE_LORA = 128
NA_HEADS = 4
NA_WIDTH = NA_HEADS * HEAD_DIM
NA_WIN_ROWS = 8
NA_WIN_COLS = 16
GQA_Q_HEADS = 8
GQA_KV_HEADS = 2
GQA_GROUP = GQA_Q_HEADS // GQA_KV_HEADS
GQ_WIDTH = GQA_Q_HEADS * HEAD_DIM
GKV_WIDTH = GQA_KV_HEADS * HEAD_DIM
RW_PROJ = 3 * RW_WIDTH + DECAY_LORA + ICLR_LORA + GATE_LORA
NA_PROJ = 3 * NA_WIDTH
GQA_PROJ = GQ_WIDTH + 2 * GKV_WIDTH
IN_PROJ = RW_PROJ + NA_PROJ + GQA_PROJ
ATT_WIDTH = GQ_WIDTH + NA_WIDTH
N_EXPERTS = 32
TOP_K = 4
D_FF = D_MODEL
SWIGLU_LIMIT = 7.0
SWIGLU_ALPHA = 1.702
ROPE_THETA = 10000.0
NORM_EPS = 1e-6
LNX_EPS = 64e-5
N_MOD = 6
ATT_SCALE = HEAD_DIM ** -0.5
NEG_BIG = -1e30

T_CTX = BATCH * SEQ
T_LAT = DEC_BATCH * DEC_SEQ
T_ALL = T_CTX + T_LAT
N_GROUPS = 1 + DEC_BATCH

LANES = 128
SUBLANES = 8
TM_TOK = 512
CHUNK = 64
RW_SEQ_GROUP = 2
TQ_GQA = 256
TM_EXP = 256
WEIGHT_DMA_PARTS = 4
SEG_ALIGN = 16
SEG_MAX_BIT = (TM_TOK // SEG_ALIGN).bit_length() - 1
TAIL_MAX_BIT = (TM_EXP // SEG_ALIGN - 1).bit_length() - 1
TOTAL_MAX_BIT = (-(-(TOP_K * TM_TOK + N_EXPERTS * (SEG_ALIGN - 1)) // SEG_ALIGN)).bit_length() - 1
LOCAL_ROWS = -(-(TOP_K * TM_TOK + N_EXPERTS * (SEG_ALIGN - 1)) // TM_TOK) * TM_TOK
N_SLOT_BLOCKS = (-(-(T_ALL * TOP_K + (T_ALL // TM_TOK) * N_EXPERTS * (SEG_ALIGN - 1)) // TM_EXP)
                 + N_EXPERTS)
N_SLOTS = N_SLOT_BLOCKS * TM_EXP
VMEM_LIMIT = 56 * 1024 * 1024

NT_DIMS = (((1,), (1,)), ((), ()))
TN_DIMS = (((0,), (0,)), ((), ()))


def _bdot(a, b, dims=None):
    a = a.astype(BF16)
    b = b.astype(BF16)
    if dims is None:
        return jnp.dot(a, b, preferred_element_type=F32)
    return lax.dot_general(a, b, dims, preferred_element_type=F32)


def _split(a):
    hi = a.astype(BF16)
    lo = (a - hi.astype(F32)).astype(BF16)
    return hi, lo


def _dot3(a, b, dims=None):
    ah, al = _split(a)
    bh, bl = _split(b)
    return _bdot(ah, bh, dims) + _bdot(ah, bl, dims) + _bdot(al, bh, dims)


def _dot_exact_lhs(a_exact, b):
    h1 = b.astype(BF16)
    r1 = b - h1.astype(F32)
    h2 = r1.astype(BF16)
    h3 = (r1 - h2.astype(F32)).astype(BF16)
    return _bdot(a_exact, h1) + _bdot(a_exact, h2) + _bdot(a_exact, h3)


def _head_ones(n):
    r = lax.broadcasted_iota(jnp.int32, (n, n), 0) // HEAD_DIM
    c = lax.broadcasted_iota(jnp.int32, (n, n), 1) // HEAD_DIM
    return (r == c).astype(BF16)


def _head_sum(x, ones_bd):
    hi, lo = _split(x)
    return (jnp.dot(hi, ones_bd, preferred_element_type=F32)
            + jnp.dot(lo, ones_bd, preferred_element_type=F32))


def _sigmoid(x):
    return 1.0 / (1.0 + jnp.exp(-x))


def _cparams(sem):
    return pltpu.CompilerParams(dimension_semantics=sem, vmem_limit_bytes=VMEM_LIMIT)


def _group_of_block(i, rows_per_block):
    first_lat = T_CTX // rows_per_block
    per_sample = DEC_SEQ // rows_per_block
    return jnp.where(i < first_lat, 0, 1 + (i - first_lat) // per_sample)


def _mods_kernel(c_ref, w_ref, b_ref, o_ref):
    c = c_ref[...]
    s = c * _sigmoid(c)
    o_ref[0] = _dot3(s, w_ref[0]) + b_ref[0]


def _mods_call(cvecs, w_mod, b_mod):
    tn = 1536
    n_rows = cvecs.shape[0]
    return pl.pallas_call(
        _mods_kernel,
        out_shape=jax.ShapeDtypeStruct((DEPTH, n_rows, N_MOD * D_MODEL), F32),
        grid=(DEPTH, N_MOD * D_MODEL // tn),
        in_specs=[
            pl.BlockSpec((n_rows, D_MODEL), lambda l, j: (0, 0)),
            pl.BlockSpec((1, D_MODEL, tn), lambda l, j: (l, 0, j)),
            pl.BlockSpec((1, 1, tn), lambda l, j: (l, 0, j)),
        ],
        out_specs=pl.BlockSpec((1, n_rows, tn), lambda l, j: (l, 0, j)),
        compiler_params=_cparams(("arbitrary", "arbitrary")),
        name="adaln_mods",
    )(cvecs, w_mod, b_mod.reshape(DEPTH, 1, N_MOD * D_MODEL))


def _rms(x, g):
    ms = jnp.mean(x * x, axis=-1, keepdims=True)
    return x * lax.rsqrt(ms + NORM_EPS) * g


def _in_proj_kernel(x_ref, mod_ref, g_ref, w_ref, qkg_ref, rc_ref, rs1_ref, rs2_ref,
                    zrw_ref, zna_ref, zgq_ref, zgkv_ref):
    x = x_ref[...]
    shift1 = mod_ref[0, 0:1, :]
    scale1 = mod_ref[0, 1:2, :]
    h = _rms(x, g_ref[...]) * (1.0 + scale1) + shift1
    z = jnp.dot(h.astype(BF16), w_ref[...], preferred_element_type=F32)
    zrw_ref[...] = z[:, :RW_PROJ]
    zna_ref[...] = z[:, RW_PROJ:RW_PROJ + NA_PROJ]
    qk_w = GQ_WIDTH + GKV_WIDTH
    qk = z[:, RW_PROJ + NA_PROJ:RW_PROJ + NA_PROJ + qk_w]
    ones_bd = _head_ones(LANES)
    sq = qk * qk
    ssq = jnp.concatenate(
        [_head_sum(sq[:, j * LANES:(j + 1) * LANES], ones_bd) for j in range(qk_w // LANES)], axis=1)
    qkn = qk * lax.rsqrt(ssq * (1.0 / HEAD_DIM) + NORM_EPS) * qkg_ref[...]
    reps = qk_w // LANES
    rc = jnp.concatenate([rc_ref[...]] * reps, axis=1)
    rs1 = jnp.concatenate([rs1_ref[...]] * reps, axis=1)
    rs2 = jnp.concatenate([rs2_ref[...]] * reps, axis=1)
    half = HEAD_DIM // 4
    qkr = qkn * rc + pltpu.roll(qkn, half, 1) * rs1 + pltpu.roll(qkn, qk_w - half, 1) * rs2
    zgq_ref[...] = qkr[:, :GQ_WIDTH]
    zgkv_ref[:, :GKV_WIDTH] = qkr[:, GQ_WIDTH:]
    zgkv_ref[:, GKV_WIDTH:] = z[:, RW_PROJ + NA_PROJ + qk_w:]


def _rope_tables():
    t = np.arange(DEC_SEQ)
    pos = np.stack([t // GRID_W, t % GRID_W], axis=1).astype(np.float32)
    axis_dim = HEAD_DIM // 2
    inv = ROPE_THETA ** (-np.arange(0, axis_dim, 2, dtype=np.float32) / axis_dim)
    d = np.arange(LANES) % HEAD_DIM
    part = d // axis_dim
    within = d % axis_dim
    freq = within % (axis_dim // 2)
    second = within // (axis_dim // 2)
    ang = jnp.asarray(pos)[:, part] * jnp.asarray(inv)[freq][None, :]
    cos = jnp.cos(ang)
    sin = jnp.sin(ang)
    s1 = jnp.where(second[None, :] == 1, sin, 0.0)
    s2 = jnp.where(second[None, :] == 0, -sin, 0.0)
    ident = jnp.ones((TM_TOK, LANES), F32)
    zero = jnp.zeros((TM_TOK, LANES), F32)
    return (jnp.concatenate([cos, ident], 0), jnp.concatenate([s1, zero], 0),
            jnp.concatenate([s2, zero], 0))


def _in_proj_call(x, mods_l, g1, w_in_bf, qk_g, rope):
    n_blk = T_ALL // TM_TOK
    lat_blk = DEC_SEQ // TM_TOK
    first_lat = T_CTX // TM_TOK

    def rope_idx(i):
        return (jnp.where(i < first_lat, lat_blk, (i - first_lat) % lat_blk), 0)

    row = lambda i: (i, 0)
    rope_spec = pl.BlockSpec((TM_TOK, LANES), rope_idx)
    return pl.pallas_call(
        _in_proj_kernel,
        out_shape=(jax.ShapeDtypeStruct((T_ALL, RW_PROJ), F32),
                   jax.ShapeDtypeStruct((T_ALL, NA_PROJ), F32),
                   jax.ShapeDtypeStruct((T_ALL, GQ_WIDTH), F32),
                   jax.ShapeDtypeStruct((T_ALL, 2 * GKV_WIDTH), F32)),
        grid=(n_blk,),
        in_specs=[
            pl.BlockSpec((TM_TOK, D_MODEL), row),
            pl.BlockSpec((1, N_MOD, D_MODEL), lambda i: (_group_of_block(i, TM_TOK), 0, 0)),
            pl.BlockSpec((1, D_MODEL), lambda i: (0, 0)),
            pl.BlockSpec((D_MODEL, IN_PROJ), lambda i: (0, 0)),
            pl.BlockSpec((1, GQ_WIDTH + GKV_WIDTH), lambda i: (0, 0)),
            rope_spec, rope_spec, rope_spec,
        ],
        out_specs=(pl.BlockSpec((TM_TOK, RW_PROJ), row), pl.BlockSpec((TM_TOK, NA_PROJ), row),
                   pl.BlockSpec((TM_TOK, GQ_WIDTH), row), pl.BlockSpec((TM_TOK, 2 * GKV_WIDTH), row)),
        compiler_params=_cparams(("arbitrary",)),
        name="in_proj",
    )(x, mods_l, g1, w_in_bf, qk_g, *rope)


def _softplus(x):
    return jnp.maximum(x, 0.0) + jnp.log(1.0 + jnp.exp(-jnp.abs(x)))


def _rw_pre(z, zprev, znext, shift_ref, kk_ref, ka_ref, rk_ref, w0_ref, wup_ref, a0_ref, aup_ref, d,
            ones_bd):
    rows = lax.broadcasted_iota(jnp.int32, z.shape, 0)
    zp = jnp.where(rows == 0, zprev, pltpu.roll(z, 1, 0))
    zn = jnp.where(rows == CHUNK - 1, znext, pltpu.roll(z, CHUNK - 1, 0))
    zs = zp * shift_ref[0:1, :] + z * shift_ref[1:2, :] + zn * shift_ref[2:3, :]
    r = zs[:, 0:RW_WIDTH]
    k = zs[:, RW_WIDTH:2 * RW_WIDTH]
    v = zs[:, 2 * RW_WIDTH:3 * RW_WIDTH]
    o = 3 * RW_WIDTH
    wd = zs[:, o:o + DECAY_LORA]
    ad = zs[:, o + DECAY_LORA:o + DECAY_LORA + ICLR_LORA]
    gd = zs[:, o + DECAY_LORA + ICLR_LORA:]
    kk = k * kk_ref[...]
    kk = kk / jnp.maximum(jnp.sqrt(_head_sum(kk * kk, ones_bd)), 1e-12)
    tw = jnp.tanh(wd)
    wl = w0_ref[d:d + 1, :] + _dot3(tw, wup_ref[d])
    lw = -jnp.exp(-_softplus(-wl) - 0.5)
    a_sig = _sigmoid(a0_ref[d:d + 1, :] + _dot3(ad, aup_ref[d]))
    k_d = k * (1.0 + (a_sig - 1.0) * ka_ref[...])
    bonus = _head_sum(r * k_d * rk_ref[...], ones_bd) * v
    return dict(r=r, k=k_d, v=v, a=-kk, b=kk * a_sig, lw=lw, bonus=bonus, gd=gd, ad=ad, k_raw=k)


def _chunk_masks(rev):
    t = lax.broadcasted_iota(jnp.int32, (CHUNK, CHUNK), 0)
    j = lax.broadcasted_iota(jnp.int32, (CHUNK, CHUNK), 1)
    return ((j >= t), (j > t)) if rev else ((j <= t), (j < t))


def _wkv_scale(p, incl, rev):
    lw = p["lw"]
    cs = _dot_exact_lhs(incl.astype(BF16), lw)
    tot = cs[0:1, :] if rev else cs[CHUNK - 1:CHUNK, :]
    e_inv = jnp.exp(-cs)
    e_rem = jnp.exp(tot - cs)
    bf = lambda x: x.astype(BF16)
    return dict(at=bf(p["a"] * jnp.exp(cs - lw)), rt=bf(p["r"] * jnp.exp(cs)),
                bt=bf(p["b"] * e_inv), kt=bf(p["k"] * e_inv),
                bh=bf(p["b"] * e_rem), kh=bf(p["k"] * e_rem), v=bf(p["v"]), gtot=jnp.exp(tot))


def _wkv_chunks(items):
    bf = lambda x: x.astype(BF16)
    c = CHUNK
    n_sq = int(np.log2(c))
    ar = [jnp.concatenate([it["at"], it["rt"]], 0) for it in items]
    m_b = [_bdot(a, it["bt"], NT_DIMS) for a, it in zip(ar, items)]
    m_k = [_bdot(a, it["kt"], NT_DIMS) for a, it in zip(ar, items)]
    s_bf = [bf(it["s"]) for it in items]
    xs = [bf(jnp.where(it["strict"], m[:c], 0.0)) for m, it in zip(m_b, items)]
    a_rb = [bf(jnp.where(it["incl"], m[c:], 0.0)) for m, it in zip(m_b, items)]
    a_ak = [bf(jnp.where(it["strict"], m[:c], 0.0)) for m, it in zip(m_k, items)]
    a_rk = [bf(jnp.where(it["incl"], m[c:], 0.0)) for m, it in zip(m_k, items)]
    ws = [_bdot(ak, it["v"]) + _bdot(it["at"], s, NT_DIMS) for ak, it, s in zip(a_ak, items, s_bf)]
    for step in range(n_sq):
        wb = [bf(w) for w in ws]
        ws = [w + _bdot(x, b) for w, x, b in zip(ws, xs, wb)]
        if step < n_sq - 1:
            xs = [bf(_bdot(x, x)) for x in xs]
    wb = [bf(w) for w in ws]
    ys = [_bdot(rb, w) + _bdot(rk, it["v"]) + _bdot(it["rt"], s, NT_DIMS)
          for rb, rk, w, it, s in zip(a_rb, a_rk, wb, items, s_bf)]
    s_new = [it["s"] * it["gtot"] + _bdot(w, it["bh"], TN_DIMS) + _bdot(it["v"], it["kh"], TN_DIMS)
             for w, it in zip(wb, items)]
    return ys, s_new


def _rwkv_kernel(zf_ref, zfp_ref, zfn_ref, zb_ref, zbp_ref, zbn_ref, s0_ref,
                 shift_ref, kk_ref, ka_ref, rk_ref, w0_ref, wup_ref, a0_ref, aup_ref, gup_ref,
                 *refs, n_chunks, n_group, has_init, emit_state, n_alias):
    refs = refs[n_alias:]
    if emit_state:
        yf_ref, yb_ref, bonus_ref, gate_ref, st_ref, h_ref = refs
    else:
        yf_ref, yb_ref, bonus_ref, gate_ref, h_ref = refs
        st_ref = None
    i = pl.program_id(1)
    ones_bd = _head_ones(RW_WIDTH)

    @pl.when(i == 0)
    def _():
        if has_init:
            h_ref[...] = s0_ref[...]
        else:
            h_ref[...] = jnp.zeros(h_ref.shape, F32)

    params = (shift_ref, kk_ref, ka_ref, rk_ref, w0_ref, wup_ref, a0_ref, aup_ref)
    first = i == 0
    last = i == n_chunks - 1
    zero_row = jnp.zeros((1, RW_PROJ), F32)
    masks = (_chunk_masks(False), _chunk_masks(True))
    items = []
    for g in range(n_group):
        pf = _rw_pre(zf_ref[g], jnp.where(first, zero_row, zfp_ref[g, 0, SUBLANES - 1:SUBLANES, :]),
                     jnp.where(last, zero_row, zfn_ref[g, 0, 0:1, :]), *params, 0, ones_bd)
        pb = _rw_pre(zb_ref[g], jnp.where(last, zero_row, zbp_ref[g, 0, SUBLANES - 1:SUBLANES, :]),
                     jnp.where(first, zero_row, zbn_ref[g, 0, 0:1, :]), *params, 1, ones_bd)
        a_sig_b = _sigmoid(a0_ref[1:2, :] + _dot3(pf["ad"], aup_ref[1]))
        k_b = pf["k_raw"] * (1.0 + (a_sig_b - 1.0) * ka_ref[...])
        bonus_ref[g] = pf["bonus"] + _head_sum(pf["r"] * k_b * rk_ref[...], ones_bd) * pf["v"]
        gate_ref[g] = _bdot(_sigmoid(pf["gd"]), gup_ref[...])
        for d, p in ((0, pf), (1, pb)):
            incl, strict = masks[d]
            sc = _wkv_scale(p, incl, d == 1)
            for h in range(RW_HEADS):
                sl = slice(h * HEAD_DIM, (h + 1) * HEAD_DIM)
                it = {k: v[:, sl] for k, v in sc.items()}
                it.update(s=h_ref[g, d, h], incl=incl, strict=strict, where=(g, d, h))
                items.append(it)

    ys, s_new = _wkv_chunks(items)
    for it, y, s in zip(items, ys, s_new):
        g, d, h = it["where"]
        y_ref = yb_ref if d else yf_ref
        y_ref[g, :, h * HEAD_DIM:(h + 1) * HEAD_DIM] = y
        h_ref[g, d, h] = s

    if emit_state:
        @pl.when(last)
        def _():
            for it, s in zip(items, s_new):
                g, d, h = it["where"]
                st_ref[g, 0, d, h] = s


def _rwkv_call(zrw, s0, lp, prev_outs, *, n_seq, seq_len, row_base, has_init, emit_state,
               state_prev=None, layer=0):
    g = RW_SEQ_GROUP
    n_chunks = seq_len // CHUNK
    n_rows8 = seq_len // SUBLANES
    per8 = CHUNK // SUBLANES
    n_view = T_ALL // seq_len
    base_g = row_base // seq_len // g
    z3 = zrw.reshape(n_view, seq_len, RW_PROJ)
    z4 = zrw.reshape(n_view, n_rows8, SUBLANES, RW_PROJ)

    fwd = lambda i: i
    bwd = lambda i: n_chunks - 1 - i
    main = lambda c: pl.BlockSpec((g, CHUNK, RW_PROJ), lambda b, i: (base_g + b, c(i), 0))
    prev8 = lambda c: pl.BlockSpec((g, 1, SUBLANES, RW_PROJ),
                                   lambda b, i: (base_g + b, jnp.maximum(c(i) * per8 - 1, 0), 0, 0))
    next8 = lambda c: pl.BlockSpec((g, 1, SUBLANES, RW_PROJ),
                                   lambda b, i: (base_g + b, jnp.minimum((c(i) + 1) * per8, n_rows8 - 1), 0, 0))
    full = lambda shape: pl.BlockSpec(shape, lambda b, i: (0,) * len(shape))
    out_f = pl.BlockSpec((g, CHUNK, RW_WIDTH), lambda b, i: (base_g + b, i, 0))
    out_b = pl.BlockSpec((g, CHUNK, RW_WIDTH), lambda b, i: (base_g + b, bwd(i), 0))
    tok = jax.ShapeDtypeStruct((n_view, seq_len, RW_WIDTH), F32)
    out_shape = [tok, tok, tok, tok]
    out_specs = [out_f, out_b, out_f, out_f]
    state_block = (g, 2, RW_HEADS, HEAD_DIM, HEAD_DIM)
    prev_views = [p.reshape(n_view, seq_len, RW_WIDTH) for p in prev_outs]
    n_in = 16
    aliases = {n_in + j: j for j in range(len(prev_views))}
    if emit_state:
        out_shape.append(jax.ShapeDtypeStruct((n_seq, DEPTH, 2, RW_HEADS, HEAD_DIM, HEAD_DIM), F32))
        out_specs.append(pl.BlockSpec((g, 1, 2, RW_HEADS, HEAD_DIM, HEAD_DIM),
                                      lambda b, i: (b, layer, 0, 0, 0, 0)))
        if state_prev is not None:
            aliases[n_in + len(prev_views)] = 4
            prev_views.append(state_prev)
    n_alias = len(prev_views)
    state_spec = pl.BlockSpec(state_block, lambda b, i: (b if has_init else 0, 0, 0, 0, 0))
    kern = functools.partial(_rwkv_kernel, n_chunks=n_chunks, n_group=g, has_init=has_init,
                             emit_state=emit_state, n_alias=n_alias)
    outs = pl.pallas_call(
        kern,
        out_shape=tuple(out_shape),
        grid=(n_seq // g, n_chunks),
        in_specs=[main(fwd), prev8(fwd), next8(fwd), main(bwd), prev8(bwd), next8(bwd), state_spec,
                  full((3, RW_PROJ)), full((1, RW_WIDTH)), full((1, RW_WIDTH)), full((1, RW_WIDTH)),
                  full((2, RW_WIDTH)), full((2, DECAY_LORA, RW_WIDTH)),
                  full((2, RW_WIDTH)), full((2, ICLR_LORA, RW_WIDTH)), full((GATE_LORA, RW_WIDTH))]
                 + [pl.BlockSpec(memory_space=pl.ANY)] * n_alias,
        out_specs=tuple(out_specs),
        scratch_shapes=[pltpu.VMEM(state_block, F32)],
        input_output_aliases=aliases,
        compiler_params=_cparams(("arbitrary", "arbitrary")),
        name="rwkv_scan_init" if has_init else "rwkv_scan_zero",
    )(z3, z4, z4, z3, z4, z4, s0, lp["rw_shift"], lp["rw_k_k"], lp["rw_k_a"], lp["rw_r_k"],
      lp["rw_w0"], lp["rw_w_up"], lp["rw_a0"], lp["rw_a_up"], lp["rw_g_up"], *prev_views)
    return tuple(o.reshape(T_ALL, RW_WIDTH) for o in outs[:4]) + tuple(outs[4:])


def _softmax_pv(scores, values):
    m = scores[0].max(axis=-1, keepdims=True)
    for s in scores[1:]:
        m = jnp.maximum(m, s.max(axis=-1, keepdims=True))
    es = [jnp.exp(s - m) for s in scores]
    l = es[0].sum(axis=-1, keepdims=True)
    for e in es[1:]:
        l = l + e.sum(axis=-1, keepdims=True)
    o = _bdot(es[0], values[0])
    for e, v in zip(es[1:], values[1:]):
        o = o + _bdot(e, v)
    return o * (1.0 / l)


def _head(x, h):
    return x[:, h * HEAD_DIM:(h + 1) * HEAD_DIM]


def _ctx_attn_kernel(zna_ref, zgq_ref, zgkv_ref, *refs):
    att_ref, nk_ref, nv_ref, gk_ref, gv_ref = refs[-5:]
    zna = zna_ref[...]
    q, k, v = zna[:, :NA_WIDTH], zna[:, NA_WIDTH:2 * NA_WIDTH], zna[:, 2 * NA_WIDTH:]
    nk_ref[0, 0] = k
    nv_ref[0, 0] = v
    gq = zgq_ref[...]
    gkv = zgkv_ref[...]
    gk, gv = gkv[:, :GKV_WIDTH], gkv[:, GKV_WIDTH:]
    gk_ref[0, 0] = gk
    gv_ref[0, 0] = gv
    gq, gk, gv = (gq * ATT_SCALE).astype(BF16), gk.astype(BF16), gv.astype(BF16)
    q, k, v = (q * ATT_SCALE).astype(BF16), k.astype(BF16), v.astype(BF16)
    for h in range(GQA_Q_HEADS):
        kv = h // GQA_GROUP
        s = _bdot(_head(gq, h), _head(gk, kv), NT_DIMS)
        att_ref[:, h * HEAD_DIM:(h + 1) * HEAD_DIM] = _softmax_pv([s], [_head(gv, kv)]).astype(BF16)
    for h in range(NA_HEADS):
        s = _bdot(_head(q, h), _head(k, h), NT_DIMS)
        o = _softmax_pv([s], [_head(v, h)])
        att_ref[:, GQ_WIDTH + h * HEAD_DIM:GQ_WIDTH + (h + 1) * HEAD_DIM] = o.astype(BF16)


def _ctx_attn_call(zna, zgq, zgkv, caches, layer):
    row = lambda b: (b, 0)
    bat = lambda b: (b, layer, 0, 0)
    cache = lambda w: jax.ShapeDtypeStruct((BATCH, DEPTH, SEQ, w), F32)
    n_alias = 0 if caches is None else len(caches)
    return pl.pallas_call(
        _ctx_attn_kernel,
        out_shape=(jax.ShapeDtypeStruct((T_ALL, ATT_WIDTH), BF16),
                   cache(NA_WIDTH), cache(NA_WIDTH), cache(GKV_WIDTH), cache(GKV_WIDTH)),
        grid=(BATCH,),
        in_specs=[pl.BlockSpec((SEQ, NA_PROJ), row), pl.BlockSpec((SEQ, GQ_WIDTH), row),
                  pl.BlockSpec((SEQ, 2 * GKV_WIDTH), row)] + [pl.BlockSpec(memory_space=pl.ANY)] * n_alias,
        out_specs=(pl.BlockSpec((SEQ, ATT_WIDTH), row),
                   pl.BlockSpec((1, 1, SEQ, NA_WIDTH), bat), pl.BlockSpec((1, 1, SEQ, NA_WIDTH), bat),
                   pl.BlockSpec((1, 1, SEQ, GKV_WIDTH), bat), pl.BlockSpec((1, 1, SEQ, GKV_WIDTH), bat)),
        input_output_aliases={3 + j: 1 + j for j in range(n_alias)},
        compiler_params=_cparams(("arbitrary",)),
        name="ctx_attention",
    )(zna, zgq, zgkv, *(caches or ()))


def _lat_na_kernel(zna_ref, ck_ref, cv_ref, tb_ref, att_in_ref, att_ref):
    del att_in_ref
    i = pl.program_id(1)
    start = jnp.clip(i - NA_WIN_ROWS // 2, 0, GRID_H - NA_WIN_ROWS)
    n_loc = NA_WIN_ROWS * GRID_W
    q = zna_ref[pl.ds(pl.multiple_of(i * GRID_W, GRID_W), GRID_W), 0:NA_WIDTH]
    q = (q * ATT_SCALE).astype(BF16)
    w0 = pl.multiple_of(start * GRID_W, GRID_W)
    kwin = zna_ref[pl.ds(w0, n_loc), NA_WIDTH:2 * NA_WIDTH].astype(BF16)
    vwin = zna_ref[pl.ds(w0, n_loc), 2 * NA_WIDTH:3 * NA_WIDTH].astype(BF16)
    kc = ck_ref[0].astype(BF16)
    vc = cv_ref[0].astype(BF16)
    dr0 = start - i + NA_WIN_ROWS - 1
    outs = []
    for h in range(NA_HEADS):
        bias = jnp.concatenate([tb_ref[h, dr0 + r] for r in range(NA_WIN_ROWS)], axis=1)
        qh = _head(q, h)
        s_loc = _bdot(qh, _head(kwin, h), NT_DIMS) + bias
        s_ctx = _bdot(qh, _head(kc, h), NT_DIMS)
        outs.append(_softmax_pv([s_loc, s_ctx], [_head(vwin, h), _head(vc, h)]))
    att_ref[...] = jnp.concatenate(outs, axis=1).astype(BF16)


def _na_bias_tables(rpb):
    w = np.arange(GRID_W)[:, None]
    kc = np.arange(GRID_W)[None, :]
    cs = np.clip(w - NA_WIN_COLS // 2, 0, GRID_W - NA_WIN_COLS)
    valid = (kc >= cs) & (kc < cs + NA_WIN_COLS)
    off = np.clip(kc - w + NA_WIN_COLS - 1, 0, 2 * NA_WIN_COLS - 2)
    onehot = (off[:, :, None] == np.arange(2 * NA_WIN_COLS - 1)).astype(np.float32)
    tb = jnp.einsum("lhrd,wkd->lhrwk", rpb, jnp.asarray(onehot), precision=lax.Precision.HIGHEST)
    return jnp.where(jnp.asarray(valid), tb, NEG_BIG).astype(F32)


def _lat_na_call(zna, ck, cv, tb, att):
    lat_blk = T_CTX // DEC_SEQ
    first_row = T_CTX // GRID_W
    return pl.pallas_call(
        _lat_na_kernel,
        out_shape=jax.ShapeDtypeStruct((T_ALL, ATT_WIDTH), BF16),
        grid=(DEC_BATCH, GRID_H),
        in_specs=[pl.BlockSpec((DEC_SEQ, NA_PROJ), lambda b, i: (lat_blk + b, 0)),
                  pl.BlockSpec((1, PAST_LEN, NA_WIDTH), lambda b, i: (b, 0, 0)),
                  pl.BlockSpec((1, PAST_LEN, NA_WIDTH), lambda b, i: (b, 0, 0)),
                  pl.BlockSpec((NA_HEADS, 2 * NA_WIN_ROWS - 1, GRID_W, GRID_W), lambda b, i: (0, 0, 0, 0)),
                  pl.BlockSpec(memory_space=pl.ANY)],
        out_specs=pl.BlockSpec((GRID_W, NA_WIDTH),
                               lambda b, i: (first_row + b * GRID_H + i, GQ_WIDTH // NA_WIDTH)),
        input_output_aliases={4: 0},
        compiler_params=_cparams(("arbitrary", "arbitrary")),
        name="latent_neighbourhood_attention",
    )(zna, ck, cv, tb, att)


def _lat_gqa_kernel(zgq_ref, zgkv_ref, ck_ref, cv_ref, att_in_ref, att_ref):
    del att_in_ref
    q = (zgq_ref[...] * ATT_SCALE).astype(BF16)
    kv = zgkv_ref[...].astype(BF16)
    kl, vl = kv[:, :GKV_WIDTH], kv[:, GKV_WIDTH:]
    kc = ck_ref[0].astype(BF16)
    vc = cv_ref[0].astype(BF16)
    for h in range(GQA_Q_HEADS):
        g = h // GQA_GROUP
        qh = _head(q, h)
        s_c = _bdot(qh, _head(kc, g), NT_DIMS)
        s_l = _bdot(qh, _head(kl, g), NT_DIMS)
        o = _softmax_pv([s_c, s_l], [_head(vc, g), _head(vl, g)])
        att_ref[:, h * HEAD_DIM:(h + 1) * HEAD_DIM] = o.astype(BF16)


def _lat_gqa_call(zgq, zgkv, ck, cv, att):
    n_q = DEC_SEQ // TQ_GQA
    first_q = T_CTX // TQ_GQA
    lat_blk = T_CTX // DEC_SEQ
    return pl.pallas_call(
        _lat_gqa_kernel,
        out_shape=jax.ShapeDtypeStruct((T_ALL, ATT_WIDTH), BF16),
        grid=(DEC_BATCH, n_q),
        in_specs=[pl.BlockSpec((TQ_GQA, GQ_WIDTH), lambda b, j: (first_q + b * n_q + j, 0)),
                  pl.BlockSpec((DEC_SEQ, 2 * GKV_WIDTH), lambda b, j: (lat_blk + b, 0)),
                  pl.BlockSpec((1, PAST_LEN, GKV_WIDTH), lambda b, j: (b, 0, 0)),
                  pl.BlockSpec((1, PAST_LEN, GKV_WIDTH), lambda b, j: (b, 0, 0)),
                  pl.BlockSpec(memory_space=pl.ANY)],
        out_specs=pl.BlockSpec((TQ_GQA, GQ_WIDTH), lambda b, j: (first_q + b * n_q + j, 0)),
        input_output_aliases={4: 0},
        compiler_params=_cparams(("arbitrary", "arbitrary")),
        name="latent_gqa_attention",
    )(zgq, zgkv, ck, cv, att)


def _out_proj_kernel(x_ref, yf_ref, yb_ref, bonus_ref, gate_ref, att_ref, mod_ref, lng_ref, lnb_ref,
                     wout_ref, g2_ref, rw_ref, rb_ref,
                     x1_ref, h2_ref, slot_ref, gates_ref, cnt_ref):
    ones_bd = _head_ones(RW_WIDTH)
    o = yf_ref[...] + yb_ref[...]
    mu = _head_sum(o, ones_bd) * (1.0 / HEAD_DIM)
    dlt = o - mu
    var = _head_sum(dlt * dlt, ones_bd) * (1.0 / HEAD_DIM)
    ln = dlt * lax.rsqrt(var + LNX_EPS) * lng_ref[...] + lnb_ref[...]
    rw = ((ln + bonus_ref[...]) * gate_ref[...]).astype(BF16)
    att = att_ref[...]
    mix = (jnp.dot(rw, wout_ref[0:RW_WIDTH, :], preferred_element_type=F32)
           + jnp.dot(att[:, :GQ_WIDTH], wout_ref[RW_WIDTH + NA_WIDTH:, :], preferred_element_type=F32)
           + jnp.dot(att[:, GQ_WIDTH:], wout_ref[RW_WIDTH:RW_WIDTH + NA_WIDTH, :],
                     preferred_element_type=F32))
    gate1 = mod_ref[0, 2:3, :]
    shift2 = mod_ref[0, 3:4, :]
    scale2 = mod_ref[0, 4:5, :]
    x1 = x_ref[...] + gate1 * mix
    x1_ref[...] = x1
    h2 = _rms(x1, g2_ref[...]) * (1.0 + scale2) + shift2
    h2_ref[...] = h2.astype(BF16)

    logits = _dot3(h2, rw_ref[...]) + rb_ref[...]
    tm = logits.shape[0]
    col = lax.broadcasted_iota(jnp.int32, (tm, N_EXPERTS), 1)
    lane4 = lax.broadcasted_iota(jnp.int32, (tm, TOP_K), 1)
    work = logits
    sels, vals = [], []
    for k in range(TOP_K):
        m = work.max(axis=-1, keepdims=True)
        idx = jnp.min(jnp.where(work == m, col, N_EXPERTS), axis=-1, keepdims=True)
        sel = col == idx
        sels.append(sel)
        vals.append(m)
        work = jnp.where(sel, -jnp.inf, work)
    es = [jnp.exp(v - vals[0]) for v in vals]
    inv = 1.0 / (es[0] + es[1] + es[2] + es[3])
    gates = jnp.zeros((tm, TOP_K), F32)
    for k in range(TOP_K):
        gates = jnp.where(lane4 == k, es[k] * inv, gates)
    assign = jnp.zeros((tm, N_EXPERTS), F32)
    for sel in sels:
        assign = assign + sel.astype(F32)
    r_i = lax.broadcasted_iota(jnp.int32, (tm, tm), 0)
    c_i = lax.broadcasted_iota(jnp.int32, (tm, tm), 1)
    before = jnp.dot((c_i < r_i).astype(BF16), assign.astype(BF16), preferred_element_type=F32)
    cnt = jnp.sum(assign, axis=0, keepdims=True)
    seg_units = jnp.floor((cnt + (SEG_ALIGN - 1)) * (1.0 / SEG_ALIGN))
    e_r = lax.broadcasted_iota(jnp.int32, (N_EXPERTS, N_EXPERTS), 0)
    e_c = lax.broadcasted_iota(jnp.int32, (N_EXPERTS, N_EXPERTS), 1)
    loc = SEG_ALIGN * jnp.dot(seg_units.astype(BF16), (e_r < e_c).astype(BF16),
                              preferred_element_type=F32)
    pos = before + loc
    slot = jnp.zeros((tm, TOP_K), F32)
    for k in range(TOP_K):
        sk = jnp.sum(jnp.where(sels[k], pos, 0.0), axis=-1, keepdims=True)
        slot = jnp.where(lane4 == k, sk, slot)
    slot_ref[...] = slot.astype(jnp.int32)
    gates_ref[...] = gates
    cnt_ref[0] = cnt.astype(jnp.int32)


def _out_proj_call(x, yf, yb, bonus, gate, att, mods_l, lp):
    n_blk = T_ALL // TM_TOK
    row = lambda i: (i, 0)
    full2 = lambda r, c: pl.BlockSpec((r, c), lambda i: (0, 0))
    tokw = lambda w: pl.BlockSpec((TM_TOK, w), row)
    return pl.pallas_call(
        _out_proj_kernel,
        out_shape=(jax.ShapeDtypeStruct((T_ALL, D_MODEL), F32),
                   jax.ShapeDtypeStruct((T_ALL, D_MODEL), BF16),
                   jax.ShapeDtypeStruct((T_ALL, TOP_K), jnp.int32),
                   jax.ShapeDtypeStruct((T_ALL, TOP_K), F32),
                   jax.ShapeDtypeStruct((n_blk, 1, N_EXPERTS), jnp.int32)),
        grid=(n_blk,),
        in_specs=[tokw(D_MODEL), tokw(RW_WIDTH), tokw(RW_WIDTH), tokw(RW_WIDTH), tokw(RW_WIDTH),
                  tokw(ATT_WIDTH),
                  pl.BlockSpec((1, N_MOD, D_MODEL), lambda i: (_group_of_block(i, TM_TOK), 0, 0)),
                  full2(1, RW_WIDTH), full2(1, RW_WIDTH), full2(D_MODEL, D_MODEL), full2(1, D_MODEL),
                  full2(D_MODEL, N_EXPERTS), full2(1, N_EXPERTS)],
        out_specs=(tokw(D_MODEL), tokw(D_MODEL), tokw(TOP_K), tokw(TOP_K),
                   pl.BlockSpec((1, 1, N_EXPERTS), lambda i: (i, 0, 0))),
        compiler_params=_cparams(("arbitrary",)),
        name="out_proj_router",
    )(x, yf, yb, bonus, gate, att, mods_l, lp["rw_ln_g"], lp["rw_ln_b"], lp["w_out"], lp["norm2_g"],
      lp["router_w"], lp["router_b"])


def _for_each_piece(n_units, max_bit, fn):
    for b in range(max_bit, -1, -1):
        @pl.when(((n_units >> b) & 1) == 1)
        def _(b=b):
            off = ((n_units >> (b + 1)) << (b + 1)) * SEG_ALIGN
            fn(pl.multiple_of(off, SEG_ALIGN), SEG_ALIGN << b)


def _segment_copies(local_ref, sorted_hbm, sem, blk, seg_ref, loc_ref, dst_ref, total_ref, to_sorted, wait):
    def copy(loc, dst, size):
        a = local_ref.at[pl.ds(pl.multiple_of(loc, SEG_ALIGN), size), :]
        b = sorted_hbm.at[pl.ds(pl.multiple_of(dst, SEG_ALIGN), size), :]
        return pltpu.make_async_copy(a, b, sem) if to_sorted else pltpu.make_async_copy(b, a, sem)

    if wait:
        _for_each_piece(total_ref[blk], TOTAL_MAX_BIT, lambda off, size: copy(0, 0, size).wait())
        return

    def body(e, carry):
        t = blk * N_EXPERTS + e
        loc = loc_ref[t]
        dst = dst_ref[t]
        _for_each_piece(seg_ref[t], SEG_MAX_BIT, lambda off, size: copy(loc + off, dst + off, size).start())
        return carry
    lax.fori_loop(0, N_EXPERTS, body, 0)


def _dispatch_kernel(seg_ref, loc_ref, dst_ref, total_ref, tail_ref, tail_dst_ref, h2_ref, slot_ref, xs_hbm,
                     xs_local, zero_buf, sem, zsem):
    j = pl.program_id(0)
    n = pl.num_programs(0)
    buf = j % 2

    @pl.when(j == 0)
    def _():
        zero_buf[...] = jnp.zeros(zero_buf.shape, BF16)

        def tails(wait):
            def body(e, carry):
                def piece(off, size):
                    cp = pltpu.make_async_copy(
                        zero_buf.at[pl.ds(0, size), :],
                        xs_hbm.at[pl.ds(pl.multiple_of(tail_dst_ref[e] + off, SEG_ALIGN), size), :], zsem)
                    if wait:
                        cp.wait()
                    else:
                        cp.start()
                _for_each_piece(tail_ref[e], TAIL_MAX_BIT, piece)
                return carry
            lax.fori_loop(0, N_EXPERTS, body, 0)
        tails(False)
        tails(True)

    lane = lax.broadcasted_iota(jnp.int32, (TM_TOK, LANES), 1)
    slots = slot_ref[...].astype(F32)
    wide = jnp.zeros((TM_TOK, LANES), F32)
    for k in range(TOP_K):
        wide = jnp.where(lane == k, slots[:, k:k + 1], wide)
    slot_rows = wide.T
    h2 = h2_ref[...]
    for c in range(LOCAL_ROWS // TM_TOK):
        row = lax.broadcasted_iota(jnp.int32, (TM_TOK, TM_TOK), 0).astype(F32) + float(c * TM_TOK)
        onehot = jnp.zeros((TM_TOK, TM_TOK), F32)
        for k in range(TOP_K):
            onehot = jnp.where(row == slot_rows[k:k + 1, :], 1.0, onehot)
        onehot = onehot.astype(BF16)
        xs_local[buf, c * TM_TOK:(c + 1) * TM_TOK, :] = jnp.dot(
            onehot, h2, preferred_element_type=F32).astype(BF16)

    tabs = (seg_ref, loc_ref, dst_ref, total_ref)

    @pl.when(j > 0)
    def _():
        _segment_copies(xs_local.at[1 - buf], xs_hbm, sem.at[1 - buf], j - 1, *tabs, True, True)

    _segment_copies(xs_local.at[buf], xs_hbm, sem.at[buf], j, *tabs, True, False)

    @pl.when(j == n - 1)
    def _():
        _segment_copies(xs_local.at[buf], xs_hbm, sem.at[buf], j, *tabs, True, True)


def _dispatch_call(tabs, h2, slot):
    n_blk = T_ALL // TM_TOK
    row = lambda j, *_: (j, 0)
    grid_spec = pltpu.PrefetchScalarGridSpec(
        num_scalar_prefetch=6,
        grid=(n_blk,),
        in_specs=[pl.BlockSpec((TM_TOK, D_MODEL), row), pl.BlockSpec((TM_TOK, TOP_K), row)],
        out_specs=pl.BlockSpec(memory_space=pl.ANY),
        scratch_shapes=[pltpu.VMEM((2, LOCAL_ROWS, D_MODEL), BF16), pltpu.VMEM((TM_EXP, D_MODEL), BF16),
                        pltpu.SemaphoreType.DMA((2,)), pltpu.SemaphoreType.DMA],
    )
    return pl.pallas_call(
        _dispatch_kernel,
        out_shape=jax.ShapeDtypeStruct((N_SLOTS, D_MODEL), BF16),
        grid_spec=grid_spec,
        compiler_params=_cparams(("arbitrary",)),
        name="moe_dispatch",
    )(tabs["seg"], tabs["loc"], tabs["dst"], tabs["total"], tabs["tail"], tabs["tail_dst"], h2, slot)


def _expert_kernel(be_ref, first_ref, par_ref, nexte_ref, meta_ref, x_ref, wgu_hbm, bgu_ref, wd_hbm, bd_ref,
                   y_ref, wgu_f, wd_f, wgu_bf, wd_bf, sem, *, layer):
    i = pl.program_id(0)
    n_used = meta_ref[0]

    def weight_copies(e, buf):
        rows = D_MODEL // WEIGHT_DMA_PARTS
        parts = []
        for p in range(WEIGHT_DMA_PARTS):
            sl = pl.ds(p * rows, rows)
            parts.append(pltpu.make_async_copy(wgu_hbm.at[layer, e, sl], wgu_f.at[buf, sl], sem.at[buf]))
            parts.append(pltpu.make_async_copy(wd_hbm.at[layer, e, sl], wd_f.at[buf, sl], sem.at[buf]))
        return parts

    @pl.when(i < n_used)
    def _():
        e = be_ref[i]
        buf = par_ref[i]
        is_first = first_ref[i] == 1

        @pl.when(is_first)
        def _():
            @pl.when(i == 0)
            def _():
                for cp in weight_copies(e, buf):
                    cp.start()
            for cp in weight_copies(e, buf):
                cp.wait()
            wgu_bf[...] = wgu_f[buf].astype(BF16)
            wd_bf[...] = wd_f[buf].astype(BF16)

        gu = jnp.dot(x_ref[...], wgu_bf[...], preferred_element_type=F32) + bgu_ref[0, 0]
        glu = jnp.minimum(gu[:, :D_FF], SWIGLU_LIMIT)
        lin = jnp.clip(gu[:, D_FF:], -SWIGLU_LIMIT, SWIGLU_LIMIT)
        act = glu * _sigmoid(SWIGLU_ALPHA * glu) * (lin + 1.0)
        y = jnp.dot(act.astype(BF16), wd_bf[...], preferred_element_type=F32) + bd_ref[0, 0]
        y_ref[...] = y.astype(BF16)

        nxt = nexte_ref[i]

        @pl.when(is_first & (nxt >= 0))
        def _():
            for cp in weight_copies(nxt, 1 - buf):
                cp.start(priority=1)


def _expert_call(tabs, xs, w_gu, b_gu, w_down, b_down, layer):
    def bmap(i, be, first, par, nxt, meta_):
        return (layer, be[i], 0, 0)

    def rmap(i, be, first, par, nxt, meta_):
        return (jnp.minimum(i, meta_[0] - 1), 0)

    grid_spec = pltpu.PrefetchScalarGridSpec(
        num_scalar_prefetch=5,
        grid=(N_SLOT_BLOCKS,),
        in_specs=[pl.BlockSpec((TM_EXP, D_MODEL), rmap),
                  pl.BlockSpec(memory_space=pl.ANY),
                  pl.BlockSpec((1, 1, 1, 2 * D_FF), bmap),
                  pl.BlockSpec(memory_space=pl.ANY),
                  pl.BlockSpec((1, 1, 1, D_MODEL), bmap)],
        out_specs=pl.BlockSpec((TM_EXP, D_MODEL), rmap),
        scratch_shapes=[pltpu.VMEM((2, D_MODEL, 2 * D_FF), F32), pltpu.VMEM((2, D_FF, D_MODEL), F32),
                        pltpu.VMEM((D_MODEL, 2 * D_FF), BF16), pltpu.VMEM((D_FF, D_MODEL), BF16),
                        pltpu.SemaphoreType.DMA((2,))],
    )
    return pl.pallas_call(
        functools.partial(_expert_kernel, layer=layer),
        out_shape=jax.ShapeDtypeStruct((N_SLOTS, D_MODEL), BF16),
        grid_spec=grid_spec,
        compiler_params=_cparams(("arbitrary",)),
        name="moe_experts",
    )(tabs["block_e"], tabs["first"], tabs["parity"], tabs["next_e"], tabs["n_used"], xs, w_gu,
      b_gu.reshape(DEPTH, N_EXPERTS, 1, 2 * D_FF), w_down, b_down.reshape(DEPTH, N_EXPERTS, 1, D_MODEL))


def _combine_kernel(seg_ref, loc_ref, dst_ref, total_ref, ys_hbm, x1_ref, slot_ref, gates_ref, mod_ref,
                    fg_ref, *rest, final):
    *o_refs, ybuf, sem = rest
    j = pl.program_id(0)
    n = pl.num_programs(0)
    buf = j % 2

    tabs = (seg_ref, loc_ref, dst_ref, total_ref)

    @pl.when(j == 0)
    def _():
        _segment_copies(ybuf.at[0], ys_hbm, sem.at[0], 0, *tabs, False, False)

    @pl.when(j + 1 < n)
    def _():
        _segment_copies(ybuf.at[1 - buf], ys_hbm, sem.at[1 - buf], j + 1, *tabs, False, False)

    _segment_copies(ybuf.at[buf], ys_hbm, sem.at[buf], j, *tabs, False, True)

    slots = slot_ref[...]
    gates = gates_ref[...]
    total = total_ref[j] * SEG_ALIGN
    ff = jnp.zeros((TM_TOK, D_MODEL), F32)
    for c in range(LOCAL_ROWS // TM_TOK):
        col = lax.broadcasted_iota(jnp.int32, (TM_TOK, TM_TOK), 1) + c * TM_TOK
        q = jnp.zeros((TM_TOK, TM_TOK), F32)
        for k in range(TOP_K):
            q = jnp.where(col == slots[:, k:k + 1], gates[:, k:k + 1], q)
        rows = lax.broadcasted_iota(jnp.int32, (TM_TOK, D_MODEL), 0) + c * TM_TOK
        y = jnp.where(rows < total, ybuf[buf, c * TM_TOK:(c + 1) * TM_TOK, :], jnp.zeros((), BF16))
        ff = ff + jnp.dot(q.astype(BF16), y, preferred_element_type=F32)
    x = x1_ref[...] + mod_ref[0, 5:6, :] * ff
    if final:
        x = _rms(x, fg_ref[...])
        ctx_ref, lat_ref = o_refs
        first_lat = T_CTX // TM_TOK

        @pl.when(j < first_lat)
        def _():
            ctx_ref[...] = x

        @pl.when(j >= first_lat)
        def _():
            lat_ref[...] = x
    else:
        o_refs[0][...] = x


def _combine_call(tabs, ys, x1, slot, gates, mods_l, final_g, final):
    n_blk = T_ALL // TM_TOK
    row = lambda j, *_: (j, 0)
    first_lat = T_CTX // TM_TOK
    if final:
        out_shape = (jax.ShapeDtypeStruct((T_CTX, D_MODEL), F32), jax.ShapeDtypeStruct((T_LAT, D_MODEL), F32))
        out_specs = (pl.BlockSpec((TM_TOK, D_MODEL), lambda j, *_: (jnp.minimum(j, first_lat - 1), 0)),
                     pl.BlockSpec((TM_TOK, D_MODEL), lambda j, *_: (jnp.maximum(j - first_lat, 0), 0)))
    else:
        out_shape = jax.ShapeDtypeStruct((T_ALL, D_MODEL), F32)
        out_specs = pl.BlockSpec((TM_TOK, D_MODEL), row)
    grid_spec = pltpu.PrefetchScalarGridSpec(
        num_scalar_prefetch=4,
        grid=(n_blk,),
        in_specs=[pl.BlockSpec(memory_space=pl.ANY),
                  pl.BlockSpec((TM_TOK, D_MODEL), row),
                  pl.BlockSpec((TM_TOK, TOP_K), row),
                  pl.BlockSpec((TM_TOK, TOP_K), row),
                  pl.BlockSpec((1, N_MOD, D_MODEL), lambda j, *_: (_group_of_block(j, TM_TOK), 0, 0)),
                  pl.BlockSpec((1, D_MODEL), lambda j, *_: (0, 0))],
        out_specs=out_specs,
        scratch_shapes=[pltpu.VMEM((2, LOCAL_ROWS, D_MODEL), BF16), pltpu.SemaphoreType.DMA((2,))],
    )
    return pl.pallas_call(
        functools.partial(_combine_kernel, final=final),
        out_shape=out_shape,
        grid_spec=grid_spec,
        compiler_params=_cparams(("arbitrary",)),
        name="moe_combine_final" if final else "moe_combine",
    )(tabs["seg"], tabs["loc"], tabs["dst"], tabs["total"], ys, x1, slot, gates, mods_l, final_g)


def _routing_tables(cnt):
    i32 = jnp.int32
    cnt = cnt.reshape(T_ALL // TM_TOK, N_EXPERTS)
    seg = (cnt + SEG_ALIGN - 1) // SEG_ALIGN * SEG_ALIGN
    loc = jnp.cumsum(seg, axis=1) - seg
    total = jnp.sum(seg, axis=1)
    rows_e = jnp.sum(seg, axis=0)
    region = (rows_e + TM_EXP - 1) // TM_EXP * TM_EXP
    region_end = jnp.cumsum(region)
    base = region_end - region
    dst = base[None, :] + jnp.cumsum(seg, axis=0) - seg
    n_used = jnp.maximum(region_end[-1] // TM_EXP, 1).astype(i32)
    blk = jnp.minimum(jnp.arange(N_SLOT_BLOCKS, dtype=i32), n_used - 1)
    block_e = jnp.sum((region_end[None, :] <= (blk * TM_EXP)[:, None]).astype(i32), axis=1)
    block_e = jnp.minimum(block_e, N_EXPERTS - 1).astype(i32)
    first = jnp.concatenate([jnp.ones((1,), i32), (block_e[1:] != block_e[:-1]).astype(i32)])
    parity = (jnp.cumsum(first) - 1) % 2
    e_ids = jnp.arange(N_EXPERTS, dtype=i32)
    later_used = (e_ids[None, :] > e_ids[:, None]) & (region[None, :] > 0)
    next_used = jnp.min(jnp.where(later_used, e_ids[None, :], N_EXPERTS), axis=1)
    next_e = next_used[block_e]
    next_e = jnp.where(next_e < N_EXPERTS, next_e, -1)
    flat = lambda a: a.reshape(-1).astype(i32)
    return dict(seg=flat(seg // SEG_ALIGN), loc=flat(loc), dst=flat(dst), total=flat(total // SEG_ALIGN),
                tail=flat((region - rows_e) // SEG_ALIGN), tail_dst=flat(base + rows_e),
                block_e=block_e, first=first, parity=flat(parity), next_e=flat(next_e),
                n_used=n_used.reshape(1))


def kernel(x_prompt, x_sample, cache_na_k, cache_na_v, cache_gqa_k, cache_gqa_v, state_rwkv, c, c_ctx,
           w_mod, b_mod, norm1_g, norm2_g, w_in, rw_shift, rw_w0, rw_w_up, rw_a0, rw_a_up, rw_g_up,
           rw_k_k, rw_k_a, rw_r_k, rw_ln_g, rw_ln_b, na_rpb, q_norm, k_norm, w_out, router_w, router_b,
           moe_w_gu, moe_b_gu, moe_w_down, moe_b_down, final_norm_g):
    x = jnp.concatenate([x_prompt.reshape(T_CTX, D_MODEL), x_sample.reshape(T_LAT, D_MODEL)], axis=0)
    cvecs = jnp.concatenate([c_ctx[None, :], c, jnp.zeros((SUBLANES - N_GROUPS, D_MODEL), F32)], axis=0)
    mods = _mods_call(cvecs, w_mod, b_mod)
    mods = mods[:, :N_GROUPS].reshape(DEPTH, N_GROUPS, N_MOD, D_MODEL)
    rope = _rope_tables()
    w_in_bf = w_in.astype(BF16)
    w_out_bf = w_out.astype(BF16)
    rw_g_up_bf = rw_g_up.astype(BF16)
    final_g = final_norm_g.reshape(1, D_MODEL)

    na_tb = _na_bias_tables(na_rpb)
    caches = None
    states = None
    for l in range(DEPTH):
        lp = {
            "rw_shift": rw_shift[l], "rw_k_k": rw_k_k[l].reshape(1, RW_WIDTH),
            "rw_k_a": rw_k_a[l].reshape(1, RW_WIDTH), "rw_r_k": rw_r_k[l].reshape(1, RW_WIDTH),
            "rw_w0": rw_w0[l], "rw_w_up": rw_w_up[l], "rw_a0": rw_a0[l], "rw_a_up": rw_a_up[l],
            "rw_g_up": rw_g_up_bf[l], "rw_ln_g": rw_ln_g[l].reshape(1, RW_WIDTH),
            "rw_ln_b": rw_ln_b[l].reshape(1, RW_WIDTH), "w_out": w_out_bf[l],
            "norm2_g": norm2_g[l].reshape(1, D_MODEL), "router_w": router_w[l],
            "router_b": router_b[l].reshape(1, N_EXPERTS),
        }
        qk_g = jnp.concatenate([jnp.tile(q_norm[l], GQA_Q_HEADS), jnp.tile(k_norm[l], GQA_KV_HEADS)])
        zrw, zna, zgq, zgkv = _in_proj_call(x, mods[l], norm1_g[l].reshape(1, D_MODEL), w_in_bf[l],
                                            qk_g.reshape(1, GQ_WIDTH + GKV_WIDTH), rope)

        s0_lat = state_rwkv[:, l]
        rw_ctx = _rwkv_call(zrw, s0_lat, lp, (), n_seq=BATCH, seq_len=SEQ, row_base=0,
                            has_init=False, emit_state=True, state_prev=states, layer=l)
        yf, yb, bonus, gate = _rwkv_call(zrw, s0_lat, lp, rw_ctx[:4], n_seq=DEC_BATCH, seq_len=DEC_SEQ,
                                         row_base=T_CTX, has_init=True, emit_state=False)
        states = rw_ctx[4]

        att, *caches = _ctx_attn_call(zna, zgq, zgkv, caches, l)
        att = _lat_na_call(zna, cache_na_k[:, l].reshape(DEC_BATCH, PAST_LEN, NA_WIDTH),
                           cache_na_v[:, l].reshape(DEC_BATCH, PAST_LEN, NA_WIDTH), na_tb[l], att)
        att = _lat_gqa_call(zgq, zgkv, cache_gqa_k[:, l].reshape(DEC_BATCH, PAST_LEN, GKV_WIDTH),
                            cache_gqa_v[:, l].reshape(DEC_BATCH, PAST_LEN, GKV_WIDTH), att)

        x1, h2, slot, gates, counts = _out_proj_call(x, yf, yb, bonus, gate, att, mods[l], lp)
        tabs = _routing_tables(counts)
        xs = _dispatch_call(tabs, h2, slot)
        ys = _expert_call(tabs, xs, moe_w_gu, moe_b_gu, moe_w_down, moe_b_down, l)
        x = _combine_call(tabs, ys, x1, slot, gates, mods[l], final_g, l == DEPTH - 1)

    y_prompt = x[0].reshape(BATCH, SEQ, D_MODEL)
    y_sample = x[1].reshape(DEC_BATCH, DEC_SEQ, D_MODEL)
    heads = lambda t, n: t.reshape(BATCH, DEPTH, SEQ, n, HEAD_DIM)
    return (y_prompt, y_sample, heads(caches[0], NA_HEADS), heads(caches[1], NA_HEADS),
            heads(caches[2], GQA_KV_HEADS), heads(caches[3], GQA_KV_HEADS), states)
```

```python
import functools

import numpy as np
import jax
import jax.numpy as jnp
from jax import lax
from jax.experimental import pallas as pl
from jax.experimental.pallas import tpu as pltpu

F32 = jnp.float32
BF16 = jnp.bfloat16

D_MODEL = 1024
BATCH = 32
SEQ = 256
DEPTH = 4
DEC_BATCH = 2
DEC_SEQ = 2048
PAST_LEN = 512
GRID_W = 64
GRID_H = DEC_SEQ // GRID_W
HEAD_DIM = 64
RW_HEADS = 4
RW_WIDTH = RW_HEADS * HEAD_DIM
DECAY_LORA = 64
ICLR_LORA = 64
GATE_LORA = 128
NA_HEADS = 4
NA_WIDTH = NA_HEADS * HEAD_DIM
NA_WIN_ROWS = 8
NA_WIN_COLS = 16
GQA_Q_HEADS = 8
GQA_KV_HEADS = 2
GQA_GROUP = GQA_Q_HEADS // GQA_KV_HEADS
GQ_WIDTH = GQA_Q_HEADS * HEAD_DIM
GKV_WIDTH = GQA_KV_HEADS * HEAD_DIM
RW_PROJ = 3 * RW_WIDTH + DECAY_LORA + ICLR_LORA + GATE_LORA
NA_PROJ = 3 * NA_WIDTH
GQA_PROJ = GQ_WIDTH + 2 * GKV_WIDTH
IN_PROJ = RW_PROJ + NA_PROJ + GQA_PROJ
ATT_WIDTH = GQ_WIDTH + NA_WIDTH
N_EXPERTS = 32
TOP_K = 4
D_FF = D_MODEL
SWIGLU_LIMIT = 7.0
SWIGLU_ALPHA = 1.702
ROPE_THETA = 10000.0
NORM_EPS = 1e-6
LNX_EPS = 64e-5
N_MOD = 6
ATT_SCALE = HEAD_DIM ** -0.5
NEG_BIG = -1e30

T_CTX = BATCH * SEQ
T_LAT = DEC_BATCH * DEC_SEQ
T_ALL = T_CTX + T_LAT
N_GROUPS = 1 + DEC_BATCH

LANES = 128
SUBLANES = 8
TM_TOK = 512
CHUNK = 64
RW_SEQ_GROUP = 4
TQ_GQA = 512
TM_EXP = 256
WEIGHT_DMA_PARTS = 4
FF_SLAB = 1024
SEG_ALIGN = 16
SEG_MAX_BIT = (TM_TOK // SEG_ALIGN).bit_length() - 1
TAIL_MAX_BIT = (TM_EXP // SEG_ALIGN - 1).bit_length() - 1
TOTAL_MAX_BIT = (-(-(TOP_K * TM_TOK + N_EXPERTS * (SEG_ALIGN - 1)) // SEG_ALIGN)).bit_length() - 1
LOCAL_ROWS = -(-(TOP_K * TM_TOK + N_EXPERTS * (SEG_ALIGN - 1)) // TM_TOK) * TM_TOK
N_SLOT_BLOCKS = (-(-(T_ALL * TOP_K + (T_ALL // TM_TOK) * N_EXPERTS * (SEG_ALIGN - 1)) // TM_EXP)
                 + N_EXPERTS)
N_SLOTS = N_SLOT_BLOCKS * TM_EXP
VMEM_LIMIT = 56 * 1024 * 1024

NT_DIMS = (((1,), (1,)), ((), ()))
TN_DIMS = (((0,), (0,)), ((), ()))


def _bdot(a, b, dims=None):
    a = a.astype(BF16)
    b = b.astype(BF16)
    if dims is None:
        return jnp.dot(a, b, preferred_element_type=F32)
    return lax.dot_general(a, b, dims, preferred_element_type=F32)


def _split(a):
    hi = a.astype(BF16)
    lo = (a - hi.astype(F32)).astype(BF16)
    return hi, lo


def _dot3(a, b, dims=None):
    ah, al = _split(a)
    bh, bl = _split(b)
    return _bdot(ah, bh, dims) + _bdot(ah, bl, dims) + _bdot(al, bh, dims)


def _dot_exact_lhs(a_exact, b):
    h1 = b.astype(BF16)
    r1 = b - h1.astype(F32)
    h2 = r1.astype(BF16)
    h3 = (r1 - h2.astype(F32)).astype(BF16)
    return _bdot(a_exact, h1) + _bdot(a_exact, h2) + _bdot(a_exact, h3)


def _head_ones(n):
    r = lax.broadcasted_iota(jnp.int32, (n, n), 0) // HEAD_DIM
    c = lax.broadcasted_iota(jnp.int32, (n, n), 1) // HEAD_DIM
    return (r == c).astype(BF16)


def _head_sum(x, ones_bd):
    hi, lo = _split(x)
    return (jnp.dot(hi, ones_bd, preferred_element_type=F32)
            + jnp.dot(lo, ones_bd, preferred_element_type=F32))


def _sigmoid(x):
    return 1.0 / (1.0 + jnp.exp(-x))


def _cparams(sem):
    return pltpu.CompilerParams(dimension_semantics=sem, vmem_limit_bytes=VMEM_LIMIT)


def _group_of_block(i, rows_per_block):
    first_lat = T_CTX // rows_per_block
    per_sample = DEC_SEQ // rows_per_block
    return jnp.where(i < first_lat, 0, 1 + (i - first_lat) // per_sample)


def _mods_kernel(c_ref, w_ref, b_ref, o_ref):
    c = c_ref[...]
    s = c * _sigmoid(c)
    o_ref[0] = _dot3(s, w_ref[0]) + b_ref[0]


def _mods_call(cvecs, w_mod, b_mod):
    tn = 1536
    n_rows = cvecs.shape[0]
    return pl.pallas_call(
        _mods_kernel,
        out_shape=jax.ShapeDtypeStruct((DEPTH, n_rows, N_MOD * D_MODEL), F32),
        grid=(DEPTH, N_MOD * D_MODEL // tn),
        in_specs=[
            pl.BlockSpec((n_rows, D_MODEL), lambda l, j: (0, 0)),
            pl.BlockSpec((1, D_MODEL, tn), lambda l, j: (l, 0, j)),
            pl.BlockSpec((1, 1, tn), lambda l, j: (l, 0, j)),
        ],
        out_specs=pl.BlockSpec((1, n_rows, tn), lambda l, j: (l, 0, j)),
        compiler_params=_cparams(("arbitrary", "arbitrary")),
        name="adaln_mods",
    )(cvecs, w_mod, b_mod.reshape(DEPTH, 1, N_MOD * D_MODEL))


def _rms(x, g):
    ms = jnp.mean(x * x, axis=-1, keepdims=True)
    return x * lax.rsqrt(ms + NORM_EPS) * g


def _in_proj_kernel(x_ref, mod_ref, g_ref, w_ref, qkg_ref, rc_ref, rs1_ref, rs2_ref,
                    zrw_ref, zna_ref, zgq_ref, zgkv_ref):
    x = x_ref[...]
    shift1 = mod_ref[0, 0:1, :]
    scale1 = mod_ref[0, 1:2, :]
    h = _rms(x, g_ref[...]) * (1.0 + scale1) + shift1
    z = jnp.dot(h.astype(BF16), w_ref[...], preferred_element_type=F32)
    zrw_ref[...] = z[:, :RW_PROJ]
    zna_ref[...] = z[:, RW_PROJ:RW_PROJ + NA_PROJ]
    qk_w = GQ_WIDTH + GKV_WIDTH
    qk = z[:, RW_PROJ + NA_PROJ:RW_PROJ + NA_PROJ + qk_w]
    ones_bd = _head_ones(LANES)
    sq = qk * qk
    ssq = jnp.concatenate(
        [_head_sum(sq[:, j * LANES:(j + 1) * LANES], ones_bd) for j in range(qk_w // LANES)], axis=1)
    qkn = qk * lax.rsqrt(ssq * (1.0 / HEAD_DIM) + NORM_EPS) * qkg_ref[...]
    reps = qk_w // LANES
    rc = jnp.concatenate([rc_ref[...]] * reps, axis=1)
    rs1 = jnp.concatenate([rs1_ref[...]] * reps, axis=1)
    rs2 = jnp.concatenate([rs2_ref[...]] * reps, axis=1)
    half = HEAD_DIM // 4
    qkr = qkn * rc + pltpu.roll(qkn, half, 1) * rs1 + pltpu.roll(qkn, qk_w - half, 1) * rs2
    zgq_ref[...] = qkr[:, :GQ_WIDTH]
    zgkv_ref[:, :GKV_WIDTH] = qkr[:, GQ_WIDTH:]
    zgkv_ref[:, GKV_WIDTH:] = z[:, RW_PROJ + NA_PROJ + qk_w:]


def _rope_tables():
    t = np.arange(DEC_SEQ)
    pos = np.stack([t // GRID_W, t % GRID_W], axis=1).astype(np.float32)
    axis_dim = HEAD_DIM // 2
    inv = ROPE_THETA ** (-np.arange(0, axis_dim, 2, dtype=np.float32) / axis_dim)
    d = np.arange(LANES) % HEAD_DIM
    part = d // axis_dim
    within = d % axis_dim
    freq = within % (axis_dim // 2)
    second = within // (axis_dim // 2)
    ang = jnp.asarray(pos)[:, part] * jnp.asarray(inv)[freq][None, :]
    cos = jnp.cos(ang)
    sin = jnp.sin(ang)
    s1 = jnp.where(second[None, :] == 1, sin, 0.0)
    s2 = jnp.where(second[None, :] == 0, -sin, 0.0)
    ident = jnp.ones((TM_TOK, LANES), F32)
    zero = jnp.zeros((TM_TOK, LANES), F32)
    return (jnp.concatenate([cos, ident], 0), jnp.concatenate([s1, zero], 0),
            jnp.concatenate([s2, zero], 0))


def _in_proj_call(x, mods_l, g1, w_in_bf, qk_g, rope):
    n_blk = T_ALL // TM_TOK
    lat_blk = DEC_SEQ // TM_TOK
    first_lat = T_CTX // TM_TOK

    def rope_idx(i):
        return (jnp.where(i < first_lat, lat_blk, (i - first_lat) % lat_blk), 0)

    row = lambda i: (i, 0)
    rope_spec = pl.BlockSpec((TM_TOK, LANES), rope_idx)
    return pl.pallas_call(
        _in_proj_kernel,
        out_shape=(jax.ShapeDtypeStruct((T_ALL, RW_PROJ), F32),
                   jax.ShapeDtypeStruct((T_ALL, NA_PROJ), F32),
                   jax.ShapeDtypeStruct((T_ALL, GQ_WIDTH), F32),
                   jax.ShapeDtypeStruct((T_ALL, 2 * GKV_WIDTH), F32)),
        grid=(n_blk,),
        in_specs=[
            pl.BlockSpec((TM_TOK, D_MODEL), row),
            pl.BlockSpec((1, N_MOD, D_MODEL), lambda i: (_group_of_block(i, TM_TOK), 0, 0)),
            pl.BlockSpec((1, D_MODEL), lambda i: (0, 0)),
            pl.BlockSpec((D_MODEL, IN_PROJ), lambda i: (0, 0)),
            pl.BlockSpec((1, GQ_WIDTH + GKV_WIDTH), lambda i: (0, 0)),
            rope_spec, rope_spec, rope_spec,
        ],
        out_specs=(pl.BlockSpec((TM_TOK, RW_PROJ), row), pl.BlockSpec((TM_TOK, NA_PROJ), row),
                   pl.BlockSpec((TM_TOK, GQ_WIDTH), row), pl.BlockSpec((TM_TOK, 2 * GKV_WIDTH), row)),
        compiler_params=_cparams(("arbitrary",)),
        name="in_proj",
    )(x, mods_l, g1, w_in_bf, qk_g, *rope)


def _softplus(x):
    return jnp.maximum(x, 0.0) + jnp.log(1.0 + jnp.exp(-jnp.abs(x)))


def _rw_pre(z, zprev, znext, shift_ref, kk_ref, ka_ref, rk_ref, w0_ref, wup_ref, a0_ref, aup_ref, d,
            ones_bd):
    rows = lax.broadcasted_iota(jnp.int32, z.shape, 0)
    zp = jnp.where(rows == 0, zprev, pltpu.roll(z, 1, 0))
    zn = jnp.where(rows == CHUNK - 1, znext, pltpu.roll(z, CHUNK - 1, 0))
    zs = zp * shift_ref[0:1, :] + z * shift_ref[1:2, :] + zn * shift_ref[2:3, :]
    r = zs[:, 0:RW_WIDTH]
    k = zs[:, RW_WIDTH:2 * RW_WIDTH]
    v = zs[:, 2 * RW_WIDTH:3 * RW_WIDTH]
    o = 3 * RW_WIDTH
    wd = zs[:, o:o + DECAY_LORA]
    ad = zs[:, o + DECAY_LORA:o + DECAY_LORA + ICLR_LORA]
    gd = zs[:, o + DECAY_LORA + ICLR_LORA:]
    kk = k * kk_ref[...]
    kk = kk / jnp.maximum(jnp.sqrt(_head_sum(kk * kk, ones_bd)), 1e-12)
    tw = jnp.tanh(wd)
    wl = w0_ref[d:d + 1, :] + _dot3(tw, wup_ref[d])
    lw = -jnp.exp(-_softplus(-wl) - 0.5)
    a_sig = _sigmoid(a0_ref[d:d + 1, :] + _dot3(ad, aup_ref[d]))
    k_d = k * (1.0 + (a_sig - 1.0) * ka_ref[...])
    bonus = _head_sum(r * k_d * rk_ref[...], ones_bd) * v
    return dict(r=r, k=k_d, v=v, a=-kk, b=kk * a_sig, lw=lw, bonus=bonus, gd=gd, ad=ad, k_raw=k)


def _chunk_masks(rev):
    t = lax.broadcasted_iota(jnp.int32, (CHUNK, CHUNK), 0)
    j = lax.broadcasted_iota(jnp.int32, (CHUNK, CHUNK), 1)
    return ((j >= t), (j > t)) if rev else ((j <= t), (j < t))


def _wkv_scale(p, incl, rev):
    lw = p["lw"]
    cs = _dot_exact_lhs(incl.astype(BF16), lw)
    tot = cs[0:1, :] if rev else cs[CHUNK - 1:CHUNK, :]
    e_inv = jnp.exp(-cs)
    e_rem = jnp.exp(tot - cs)
    bf = lambda x: x.astype(BF16)
    return dict(at=bf(p["a"] * jnp.exp(cs - lw)), rt=bf(p["r"] * jnp.exp(cs)),
                bt=bf(p["b"] * e_inv), kt=bf(p["k"] * e_inv),
                bh=bf(p["b"] * e_rem), kh=bf(p["k"] * e_rem), v=bf(p["v"]), gtot=jnp.exp(tot))


def _wkv_chunks(items):
    bf = lambda x: x.astype(BF16)
    c = CHUNK
    n_sq = int(np.log2(c))
    ar = [jnp.concatenate([it["at"], it["rt"]], 0) for it in items]
    m_b = [_bdot(a, it["bt"], NT_DIMS) for a, it in zip(ar, items)]
    m_k = [_bdot(a, it["kt"], NT_DIMS) for a, it in zip(ar, items)]
    s_bf = [bf(it["s"]) for it in items]
    xs = [bf(jnp.where(it["strict"], m[:c], 0.0)) for m, it in zip(m_b, items)]
    a_rb = [bf(jnp.where(it["incl"], m[c:], 0.0)) for m, it in zip(m_b, items)]
    a_ak = [bf(jnp.where(it["strict"], m[:c], 0.0)) for m, it in zip(m_k, items)]
    a_rk = [bf(jnp.where(it["incl"], m[c:], 0.0)) for m, it in zip(m_k, items)]
    ws = [_bdot(ak, it["v"]) + _bdot(it["at"], s, NT_DIMS) for ak, it, s in zip(a_ak, items, s_bf)]
    for step in range(n_sq):
        wb = [bf(w) for w in ws]
        ws = [w + _bdot(x, b) for w, x, b in zip(ws, xs, wb)]
        if step < n_sq - 1:
            xs = [bf(_bdot(x, x)) for x in xs]
    wb = [bf(w) for w in ws]
    ys = [_bdot(rb, w) + _bdot(rk, it["v"]) + _bdot(it["rt"], s, NT_DIMS)
          for rb, rk, w, it, s in zip(a_rb, a_rk, wb, items, s_bf)]
    s_new = [it["s"] * it["gtot"] + _bdot(w, it["bh"], TN_DIMS) + _bdot(it["v"], it["kh"], TN_DIMS)
             for w, it in zip(wb, items)]
    return ys, s_new


def _rwkv_kernel(zf_ref, zfp_ref, zfn_ref, zb_ref, zbp_ref, zbn_ref, s0_ref,
                 shift_ref, kk_ref, ka_ref, rk_ref, w0_ref, wup_ref, a0_ref, aup_ref, gup_ref,
                 *refs, n_chunks, n_group, has_init, emit_state, n_alias):
    refs = refs[n_alias:]
    if emit_state:
        yf_ref, yb_ref, bonus_ref, gate_ref, st_ref, h_ref = refs
    else:
        yf_ref, yb_ref, bonus_ref, gate_ref, h_ref = refs
        st_ref = None
    i = pl.program_id(1)
    ones_bd = _head_ones(RW_WIDTH)

    @pl.when(i == 0)
    def _():
        if has_init:
            h_ref[...] = s0_ref[...]
        else:
            h_ref[...] = jnp.zeros(h_ref.shape, F32)

    params = (shift_ref, kk_ref, ka_ref, rk_ref, w0_ref, wup_ref, a0_ref, aup_ref)
    first = i == 0
    last = i == n_chunks - 1
    zero_row = jnp.zeros((1, RW_PROJ), F32)
    masks = (_chunk_masks(False), _chunk_masks(True))
    items = []
    for g in range(n_group):
        pf = _rw_pre(zf_ref[g], jnp.where(first, zero_row, zfp_ref[g, 0, SUBLANES - 1:SUBLANES, :]),
                     jnp.where(last, zero_row, zfn_ref[g, 0, 0:1, :]), *params, 0, ones_bd)
        pb = _rw_pre(zb_ref[g], jnp.where(last, zero_row, zbp_ref[g, 0, SUBLANES - 1:SUBLANES, :]),
                     jnp.where(first, zero_row, zbn_ref[g, 0, 0:1, :]), *params, 1, ones_bd)
        a_sig_b = _sigmoid(a0_ref[1:2, :] + _dot3(pf["ad"], aup_ref[1]))
        k_b = pf["k_raw"] * (1.0 + (a_sig_b - 1.0) * ka_ref[...])
        bonus_ref[g] = pf["bonus"] + _head_sum(pf["r"] * k_b * rk_ref[...], ones_bd) * pf["v"]
        gate_ref[g] = _bdot(_sigmoid(pf["gd"]), gup_ref[...])
        for d, p in ((0, pf), (1, pb)):
            incl, strict = masks[d]
            sc = _wkv_scale(p, incl, d == 1)
            for h in range(RW_HEADS):
                sl = slice(h * HEAD_DIM, (h + 1) * HEAD_DIM)
                it = {k: v[:, sl] for k, v in sc.items()}
                it.update(s=h_ref[g, d, h], incl=incl, strict=strict, where=(g, d, h))
                items.append(it)

    ys, s_new = _wkv_chunks(items)
    for it, y, s in zip(items, ys, s_new):
        g, d, h = it["where"]
        y_ref = yb_ref if d else yf_ref
        y_ref[g, :, h * HEAD_DIM:(h + 1) * HEAD_DIM] = y
        h_ref[g, d, h] = s

    if emit_state:
        @pl.when(last)
        def _():
            for it, s in zip(items, s_new):
                g, d, h = it["where"]
                st_ref[g, 0, d, h] = s


def _rwkv_call(zrw, s0, lp, prev_outs, *, n_seq, seq_len, row_base, has_init, emit_state,
               state_prev=None, layer=0):
    g = min(RW_SEQ_GROUP, n_seq)
    n_chunks = seq_len // CHUNK
    n_rows8 = seq_len // SUBLANES
    per8 = CHUNK // SUBLANES
    n_view = T_ALL // seq_len
    base_g = row_base // seq_len // g
    z3 = zrw.reshape(n_view, seq_len, RW_PROJ)
    z4 = zrw.reshape(n_view, n_rows8, SUBLANES, RW_PROJ)

    fwd = lambda i: i
    bwd = lambda i: n_chunks - 1 - i
    main = lambda c: pl.BlockSpec((g, CHUNK, RW_PROJ), lambda b, i: (base_g + b, c(i), 0))
    prev8 = lambda c: pl.BlockSpec((g, 1, SUBLANES, RW_PROJ),
                                   lambda b, i: (base_g + b, jnp.maximum(c(i) * per8 - 1, 0), 0, 0))
    next8 = lambda c: pl.BlockSpec((g, 1, SUBLANES, RW_PROJ),
                                   lambda b, i: (base_g + b, jnp.minimum((c(i) + 1) * per8, n_rows8 - 1), 0, 0))
    full = lambda shape: pl.BlockSpec(shape, lambda b, i: (0,) * len(shape))
    out_f = pl.BlockSpec((g, CHUNK, RW_WIDTH), lambda b, i: (base_g + b, i, 0))
    out_b = pl.BlockSpec((g, CHUNK, RW_WIDTH), lambda b, i: (base_g + b, bwd(i), 0))
    tok = jax.ShapeDtypeStruct((n_view, seq_len, RW_WIDTH), F32)
    out_shape = [tok, tok, tok, tok]
    out_specs = [out_f, out_b, out_f, out_f]
    state_block = (g, 2, RW_HEADS, HEAD_DIM, HEAD_DIM)
    prev_views = [p.reshape(n_view, seq_len, RW_WIDTH) for p in prev_outs]
    n_in = 16
    aliases = {n_in + j: j for j in range(len(prev_views))}
    if emit_state:
        out_shape.append(jax.ShapeDtypeStruct((n_seq, DEPTH, 2, RW_HEADS, HEAD_DIM, HEAD_DIM), F32))
        out_specs.append(pl.BlockSpec((g, 1, 2, RW_HEADS, HEAD_DIM, HEAD_DIM),
                                      lambda b, i: (b, layer, 0, 0, 0, 0)))
        if state_prev is not None:
            aliases[n_in + len(prev_views)] = 4
            prev_views.append(state_prev)
    n_alias = len(prev_views)
    state_spec = pl.BlockSpec(state_block, lambda b, i: (b if has_init else 0, 0, 0, 0, 0))
    kern = functools.partial(_rwkv_kernel, n_chunks=n_chunks, n_group=g, has_init=has_init,
                             emit_state=emit_state, n_alias=n_alias)
    outs = pl.pallas_call(
        kern,
        out_shape=tuple(out_shape),
        grid=(n_seq // g, n_chunks),
        in_specs=[main(fwd), prev8(fwd), next8(fwd), main(bwd), prev8(bwd), next8(bwd), state_spec,
                  full((3, RW_PROJ)), full((1, RW_WIDTH)), full((1, RW_WIDTH)), full((1, RW_WIDTH)),
                  full((2, RW_WIDTH)), full((2, DECAY_LORA, RW_WIDTH)),
                  full((2, RW_WIDTH)), full((2, ICLR_LORA, RW_WIDTH)), full((GATE_LORA, RW_WIDTH))]
                 + [pl.BlockSpec(memory_space=pl.ANY)] * n_alias,
        out_specs=tuple(out_specs),
        scratch_shapes=[pltpu.VMEM(state_block, F32)],
        input_output_aliases=aliases,
        compiler_params=_cparams(("arbitrary", "arbitrary")),
        name="rwkv_scan_init" if has_init else "rwkv_scan_zero",
    )(z3, z4, z4, z3, z4, z4, s0, lp["rw_shift"], lp["rw_k_k"], lp["rw_k_a"], lp["rw_r_k"],
      lp["rw_w0"], lp["rw_w_up"], lp["rw_a0"], lp["rw_a_up"], lp["rw_g_up"], *prev_views)
    return tuple(o.reshape(T_ALL, RW_WIDTH) for o in outs[:4]) + tuple(outs[4:])


def _softmax_pv(scores, values):
    m = scores[0].max(axis=-1, keepdims=True)
    for s in scores[1:]:
        m = jnp.maximum(m, s.max(axis=-1, keepdims=True))
    es = [jnp.exp(s - m) for s in scores]
    l = es[0].sum(axis=-1, keepdims=True)
    for e in es[1:]:
        l = l + e.sum(axis=-1, keepdims=True)
    o = _bdot(es[0], values[0])
    for e, v in zip(es[1:], values[1:]):
        o = o + _bdot(e, v)
    return o * (1.0 / l)


def _head(x, h):
    return x[:, h * HEAD_DIM:(h + 1) * HEAD_DIM]


def _ctx_attn_kernel(zna_ref, zgq_ref, zgkv_ref, *refs):
    att_ref, nk_ref, nv_ref, gk_ref, gv_ref = refs[-5:]
    zna = zna_ref[...]
    q, k, v = zna[:, :NA_WIDTH], zna[:, NA_WIDTH:2 * NA_WIDTH], zna[:, 2 * NA_WIDTH:]
    nk_ref[0, 0] = k
    nv_ref[0, 0] = v
    gq = zgq_ref[...]
    gkv = zgkv_ref[...]
    gk, gv = gkv[:, :GKV_WIDTH], gkv[:, GKV_WIDTH:]
    gk_ref[0, 0] = gk
    gv_ref[0, 0] = gv
    gq, gk, gv = (gq * ATT_SCALE).astype(BF16), gk.astype(BF16), gv.astype(BF16)
    q, k, v = (q * ATT_SCALE).astype(BF16), k.astype(BF16), v.astype(BF16)
    for h in range(GQA_Q_HEADS):
        kv = h // GQA_GROUP
        s = _bdot(_head(gq, h), _head(gk, kv), NT_DIMS)
        att_ref[:, h * HEAD_DIM:(h + 1) * HEAD_DIM] = _softmax_pv([s], [_head(gv, kv)]).astype(BF16)
    for h in range(NA_HEADS):
        s = _bdot(_head(q, h), _head(k, h), NT_DIMS)
        o = _softmax_pv([s], [_head(v, h)])
        att_ref[:, GQ_WIDTH + h * HEAD_DIM:GQ_WIDTH + (h + 1) * HEAD_DIM] = o.astype(BF16)


def _ctx_attn_call(zna, zgq, zgkv, caches, layer):
    row = lambda b: (b, 0)
    bat = lambda b: (b, layer, 0, 0)
    cache = lambda w: jax.ShapeDtypeStruct((BATCH, DEPTH, SEQ, w), F32)
    n_alias = 0 if caches is None else len(caches)
    return pl.pallas_call(
        _ctx_attn_kernel,
        out_shape=(jax.ShapeDtypeStruct((T_ALL, ATT_WIDTH), BF16),
                   cache(NA_WIDTH), cache(NA_WIDTH), cache(GKV_WIDTH), cache(GKV_WIDTH)),
        grid=(BATCH,),
        in_specs=[pl.BlockSpec((SEQ, NA_PROJ), row), pl.BlockSpec((SEQ, GQ_WIDTH), row),
                  pl.BlockSpec((SEQ, 2 * GKV_WIDTH), row)] + [pl.BlockSpec(memory_space=pl.ANY)] * n_alias,
        out_specs=(pl.BlockSpec((SEQ, ATT_WIDTH), row),
                   pl.BlockSpec((1, 1, SEQ, NA_WIDTH), bat), pl.BlockSpec((1, 1, SEQ, NA_WIDTH), bat),
                   pl.BlockSpec((1, 1, SEQ, GKV_WIDTH), bat), pl.BlockSpec((1, 1, SEQ, GKV_WIDTH), bat)),
        input_output_aliases={3 + j: 1 + j for j in range(n_alias)},
        compiler_params=_cparams(("arbitrary",)),
        name="ctx_attention",
    )(zna, zgq, zgkv, *(caches or ()))


def _lat_na_kernel(zna_ref, ck_ref, cv_ref, tb_ref, att_in_ref, att_ref):
    del att_in_ref
    i = pl.program_id(1)
    start = jnp.clip(i - NA_WIN_ROWS // 2, 0, GRID_H - NA_WIN_ROWS)
    n_loc = NA_WIN_ROWS * GRID_W
    q = zna_ref[pl.ds(pl.multiple_of(i * GRID_W, GRID_W), GRID_W), 0:NA_WIDTH]
    q = (q * ATT_SCALE).astype(BF16)
    w0 = pl.multiple_of(start * GRID_W, GRID_W)
    kwin = zna_ref[pl.ds(w0, n_loc), NA_WIDTH:2 * NA_WIDTH].astype(BF16)
    vwin = zna_ref[pl.ds(w0, n_loc), 2 * NA_WIDTH:3 * NA_WIDTH].astype(BF16)
    kc = ck_ref[0].astype(BF16)
    vc = cv_ref[0].astype(BF16)
    dr0 = start - i + NA_WIN_ROWS - 1
    outs = []
    for h in range(NA_HEADS):
        bias = jnp.concatenate([tb_ref[h, dr0 + r] for r in range(NA_WIN_ROWS)], axis=1)
        qh = _head(q, h)
        s_loc = _bdot(qh, _head(kwin, h), NT_DIMS) + bias
        s_ctx = _bdot(qh, _head(kc, h), NT_DIMS)
        outs.append(_softmax_pv([s_loc, s_ctx], [_head(vwin, h), _head(vc, h)]))
    att_ref[...] = jnp.concatenate(outs, axis=1).astype(BF16)


def _na_bias_tables(rpb):
    w = np.arange(GRID_W)[:, None]
    kc = np.arange(GRID_W)[None, :]
    cs = np.clip(w - NA_WIN_COLS // 2, 0, GRID_W - NA_WIN_COLS)
    valid = (kc >= cs) & (kc < cs + NA_WIN_COLS)
    off = np.clip(kc - w + NA_WIN_COLS - 1, 0, 2 * NA_WIN_COLS - 2)
    onehot = (off[:, :, None] == np.arange(2 * NA_WIN_COLS - 1)).astype(np.float32)
    tb = jnp.einsum("lhrd,wkd->lhrwk", rpb, jnp.asarray(onehot), precision=lax.Precision.HIGHEST)
    return jnp.where(jnp.asarray(valid), tb, NEG_BIG).astype(F32)


def _lat_na_call(zna, ck, cv, tb, att):
    lat_blk = T_CTX // DEC_SEQ
    first_row = T_CTX // GRID_W
    return pl.pallas_call(
        _lat_na_kernel,
        out_shape=jax.ShapeDtypeStruct((T_ALL, ATT_WIDTH), BF16),
        grid=(DEC_BATCH, GRID_H),
        in_specs=[pl.BlockSpec((DEC_SEQ, NA_PROJ), lambda b, i: (lat_blk + b, 0)),
                  pl.BlockSpec((1, PAST_LEN, NA_WIDTH), lambda b, i: (b, 0, 0)),
                  pl.BlockSpec((1, PAST_LEN, NA_WIDTH), lambda b, i: (b, 0, 0)),
                  pl.BlockSpec((NA_HEADS, 2 * NA_WIN_ROWS - 1, GRID_W, GRID_W), lambda b, i: (0, 0, 0, 0)),
                  pl.BlockSpec(memory_space=pl.ANY)],
        out_specs=pl.BlockSpec((GRID_W, NA_WIDTH),
                               lambda b, i: (first_row + b * GRID_H + i, GQ_WIDTH // NA_WIDTH)),
        input_output_aliases={4: 0},
        compiler_params=_cparams(("arbitrary", "arbitrary")),
        name="latent_neighbourhood_attention",
    )(zna, ck, cv, tb, att)


def _lat_gqa_kernel(zgq_ref, zgkv_ref, ck_ref, cv_ref, att_in_ref, att_ref):
    del att_in_ref
    q = (zgq_ref[...] * ATT_SCALE).astype(BF16)
    kv = zgkv_ref[...].astype(BF16)
    kl, vl = kv[:, :GKV_WIDTH], kv[:, GKV_WIDTH:]
    kc = ck_ref[0].astype(BF16)
    vc = cv_ref[0].astype(BF16)
    for h in range(GQA_Q_HEADS):
        g = h // GQA_GROUP
        qh = _head(q, h)
        s_c = _bdot(qh, _head(kc, g), NT_DIMS)
        s_l = _bdot(qh, _head(kl, g), NT_DIMS)
        o = _softmax_pv([s_c, s_l], [_head(vc, g), _head(vl, g)])
        att_ref[:, h * HEAD_DIM:(h + 1) * HEAD_DIM] = o.astype(BF16)


def _lat_gqa_call(zgq, zgkv, ck, cv, att):
    n_q = DEC_SEQ // TQ_GQA
    first_q = T_CTX // TQ_GQA
    lat_blk = T_CTX // DEC_SEQ
    return pl.pallas_call(
        _lat_gqa_kernel,
        out_shape=jax.ShapeDtypeStruct((T_ALL, ATT_WIDTH), BF16),
        grid=(DEC_BATCH, n_q),
        in_specs=[pl.BlockSpec((TQ_GQA, GQ_WIDTH), lambda b, j: (first_q + b * n_q + j, 0)),
                  pl.BlockSpec((DEC_SEQ, 2 * GKV_WIDTH), lambda b, j: (lat_blk + b, 0)),
                  pl.BlockSpec((1, PAST_LEN, GKV_WIDTH), lambda b, j: (b, 0, 0)),
                  pl.BlockSpec((1, PAST_LEN, GKV_WIDTH), lambda b, j: (b, 0, 0)),
                  pl.BlockSpec(memory_space=pl.ANY)],
        out_specs=pl.BlockSpec((TQ_GQA, GQ_WIDTH), lambda b, j: (first_q + b * n_q + j, 0)),
        input_output_aliases={4: 0},
        compiler_params=_cparams(("arbitrary", "arbitrary")),
        name="latent_gqa_attention",
    )(zgq, zgkv, ck, cv, att)


def _out_proj_kernel(x_ref, yf_ref, yb_ref, bonus_ref, gate_ref, att_ref, mod_ref, lng_ref, lnb_ref,
                     wout_ref, g2_ref, rw_ref, rb_ref,
                     x1_ref, h2_ref, slot_ref, gates_ref, cnt_ref):
    ones_bd = _head_ones(RW_WIDTH)
    o = yf_ref[...] + yb_ref[...]
    mu = _head_sum(o, ones_bd) * (1.0 / HEAD_DIM)
    dlt = o - mu
    var = _head_sum(dlt * dlt, ones_bd) * (1.0 / HEAD_DIM)
    ln = dlt * lax.rsqrt(var + LNX_EPS) * lng_ref[...] + lnb_ref[...]
    rw = ((ln + bonus_ref[...]) * gate_ref[...]).astype(BF16)
    att = att_ref[...]
    mix = (jnp.dot(rw, wout_ref[0:RW_WIDTH, :], preferred_element_type=F32)
           + jnp.dot(att[:, :GQ_WIDTH], wout_ref[RW_WIDTH + NA_WIDTH:, :], preferred_element_type=F32)
           + jnp.dot(att[:, GQ_WIDTH:], wout_ref[RW_WIDTH:RW_WIDTH + NA_WIDTH, :],
                     preferred_element_type=F32))
    gate1 = mod_ref[0, 2:3, :]
    shift2 = mod_ref[0, 3:4, :]
    scale2 = mod_ref[0, 4:5, :]
    x1 = x_ref[...] + gate1 * mix
    x1_ref[...] = x1
    h2 = _rms(x1, g2_ref[...]) * (1.0 + scale2) + shift2
    h2_ref[...] = h2.astype(BF16)

    logits = _dot3(h2, rw_ref[...]) + rb_ref[...]
    tm = logits.shape[0]
    col = lax.broadcasted_iota(jnp.int32, (tm, N_EXPERTS), 1)
    lane4 = lax.broadcasted_iota(jnp.int32, (tm, TOP_K), 1)
    work = logits
    sels, vals = [], []
    for k in range(TOP_K):
        m = work.max(axis=-1, keepdims=True)
        idx = jnp.min(jnp.where(work == m, col, N_EXPERTS), axis=-1, keepdims=True)
        sel = col == idx
        sels.append(sel)
        vals.append(m)
        work = jnp.where(sel, -jnp.inf, work)
    es = [jnp.exp(v - vals[0]) for v in vals]
    inv = 1.0 / (es[0] + es[1] + es[2] + es[3])
    gates = jnp.zeros((tm, TOP_K), F32)
    for k in range(TOP_K):
        gates = jnp.where(lane4 == k, es[k] * inv, gates)
    assign = jnp.zeros((tm, N_EXPERTS), F32)
    for sel in sels:
        assign = assign + sel.astype(F32)
    r_i = lax.broadcasted_iota(jnp.int32, (tm, tm), 0)
    c_i = lax.broadcasted_iota(jnp.int32, (tm, tm), 1)
    before = jnp.dot((c_i < r_i).astype(BF16), assign.astype(BF16), preferred_element_type=F32)
    cnt = jnp.sum(assign, axis=0, keepdims=True)
    seg_units = jnp.floor((cnt + (SEG_ALIGN - 1)) * (1.0 / SEG_ALIGN))
    e_r = lax.broadcasted_iota(jnp.int32, (N_EXPERTS, N_EXPERTS), 0)
    e_c = lax.broadcasted_iota(jnp.int32, (N_EXPERTS, N_EXPERTS), 1)
    loc = SEG_ALIGN * jnp.dot(seg_units.astype(BF16), (e_r < e_c).astype(BF16),
                              preferred_element_type=F32)
    pos = before + loc
    slot = jnp.zeros((tm, TOP_K), F32)
    for k in range(TOP_K):
        sk = jnp.sum(jnp.where(sels[k], pos, 0.0), axis=-1, keepdims=True)
        slot = jnp.where(lane4 == k, sk, slot)
    slot_ref[...] = slot.astype(jnp.int32)
    gates_ref[...] = gates
    cnt_ref[0] = cnt.astype(jnp.int32)


def _out_proj_call(x, yf, yb, bonus, gate, att, mods_l, lp):
    n_blk = T_ALL // TM_TOK
    row = lambda i: (i, 0)
    full2 = lambda r, c: pl.BlockSpec((r, c), lambda i: (0, 0))
    tokw = lambda w: pl.BlockSpec((TM_TOK, w), row)
    return pl.pallas_call(
        _out_proj_kernel,
        out_shape=(jax.ShapeDtypeStruct((T_ALL, D_MODEL), F32),
                   jax.ShapeDtypeStruct((T_ALL, D_MODEL), BF16),
                   jax.ShapeDtypeStruct((T_ALL, TOP_K), jnp.int32),
                   jax.ShapeDtypeStruct((T_ALL, TOP_K), F32),
                   jax.ShapeDtypeStruct((n_blk, 1, N_EXPERTS), jnp.int32)),
        grid=(n_blk,),
        in_specs=[tokw(D_MODEL), tokw(RW_WIDTH), tokw(RW_WIDTH), tokw(RW_WIDTH), tokw(RW_WIDTH),
                  tokw(ATT_WIDTH),
                  pl.BlockSpec((1, N_MOD, D_MODEL), lambda i: (_group_of_block(i, TM_TOK), 0, 0)),
                  full2(1, RW_WIDTH), full2(1, RW_WIDTH), full2(D_MODEL, D_MODEL), full2(1, D_MODEL),
                  full2(D_MODEL, N_EXPERTS), full2(1, N_EXPERTS)],
        out_specs=(tokw(D_MODEL), tokw(D_MODEL), tokw(TOP_K), tokw(TOP_K),
                   pl.BlockSpec((1, 1, N_EXPERTS), lambda i: (i, 0, 0))),
        compiler_params=_cparams(("arbitrary",)),
        name="out_proj_router",
    )(x, yf, yb, bonus, gate, att, mods_l, lp["rw_ln_g"], lp["rw_ln_b"], lp["w_out"], lp["norm2_g"],
      lp["router_w"], lp["router_b"])


def _for_each_piece(n_units, max_bit, fn):
    for b in range(max_bit, -1, -1):
        @pl.when(((n_units >> b) & 1) == 1)
        def _(b=b):
            off = ((n_units >> (b + 1)) << (b + 1)) * SEG_ALIGN
            fn(pl.multiple_of(off, SEG_ALIGN), SEG_ALIGN << b)


def _segment_copies(local_ref, sorted_hbm, sem, blk, seg_ref, loc_ref, dst_ref, total_ref, to_sorted, wait):
    def copy(loc, dst, size):
        a = local_ref.at[pl.ds(pl.multiple_of(loc, SEG_ALIGN), size), :]
        b = sorted_hbm.at[pl.ds(pl.multiple_of(dst, SEG_ALIGN), size), :]
        return pltpu.make_async_copy(a, b, sem) if to_sorted else pltpu.make_async_copy(b, a, sem)

    if wait:
        _for_each_piece(total_ref[blk], TOTAL_MAX_BIT, lambda off, size: copy(0, 0, size).wait())
        return

    def body(e, carry):
        t = blk * N_EXPERTS + e
        loc = loc_ref[t]
        dst = dst_ref[t]
        _for_each_piece(seg_ref[t], SEG_MAX_BIT, lambda off, size: copy(loc + off, dst + off, size).start())
        return carry
    lax.fori_loop(0, N_EXPERTS, body, 0)


def _dispatch_kernel(seg_ref, loc_ref, dst_ref, total_ref, tail_ref, tail_dst_ref, h2_ref, slot_ref, xs_hbm,
                     xs_local, zero_buf, sem, zsem):
    j = pl.program_id(0)
    n = pl.num_programs(0)
    buf = j % 2

    @pl.when(j == 0)
    def _():
        zero_buf[...] = jnp.zeros(zero_buf.shape, BF16)

        def tails(wait):
            def body(e, carry):
                def piece(off, size):
                    cp = pltpu.make_async_copy(
                        zero_buf.at[pl.ds(0, size), :],
                        xs_hbm.at[pl.ds(pl.multiple_of(tail_dst_ref[e] + off, SEG_ALIGN), size), :], zsem)
                    if wait:
                        cp.wait()
                    else:
                        cp.start()
                _for_each_piece(tail_ref[e], TAIL_MAX_BIT, piece)
                return carry
            lax.fori_loop(0, N_EXPERTS, body, 0)
        tails(False)
        tails(True)

    lane = lax.broadcasted_iota(jnp.int32, (TM_TOK, LANES), 1)
    slots = slot_ref[...].astype(F32)
    wide = jnp.zeros((TM_TOK, LANES), F32)
    for k in range(TOP_K):
        wide = jnp.where(lane == k, slots[:, k:k + 1], wide)
    slot_rows = wide.T
    h2 = h2_ref[...]
    for c in range(LOCAL_ROWS // TM_TOK):
        row = lax.broadcasted_iota(jnp.int32, (TM_TOK, TM_TOK), 0).astype(F32) + float(c * TM_TOK)
        onehot = jnp.zeros((TM_TOK, TM_TOK), F32)
        for k in range(TOP_K):
            onehot = jnp.where(row == slot_rows[k:k + 1, :], 1.0, onehot)
        onehot = onehot.astype(BF16)
        xs_local[buf, c * TM_TOK:(c + 1) * TM_TOK, :] = jnp.dot(
            onehot, h2, preferred_element_type=F32).astype(BF16)

    tabs = (seg_ref, loc_ref, dst_ref, total_ref)

    @pl.when(j > 0)
    def _():
        _segment_copies(xs_local.at[1 - buf], xs_hbm, sem.at[1 - buf], j - 1, *tabs, True, True)

    _segment_copies(xs_local.at[buf], xs_hbm, sem.at[buf], j, *tabs, True, False)

    @pl.when(j == n - 1)
    def _():
        _segment_copies(xs_local.at[buf], xs_hbm, sem.at[buf], j, *tabs, True, True)


def _dispatch_call(tabs, h2, slot):
    n_blk = T_ALL // TM_TOK
    row = lambda j, *_: (j, 0)
    grid_spec = pltpu.PrefetchScalarGridSpec(
        num_scalar_prefetch=6,
        grid=(n_blk,),
        in_specs=[pl.BlockSpec((TM_TOK, D_MODEL), row), pl.BlockSpec((TM_TOK, TOP_K), row)],
        out_specs=pl.BlockSpec(memory_space=pl.ANY),
        scratch_shapes=[pltpu.VMEM((2, LOCAL_ROWS, D_MODEL), BF16), pltpu.VMEM((TM_EXP, D_MODEL), BF16),
                        pltpu.SemaphoreType.DMA((2,)), pltpu.SemaphoreType.DMA],
    )
    return pl.pallas_call(
        _dispatch_kernel,
        out_shape=jax.ShapeDtypeStruct((N_SLOTS, D_MODEL), BF16),
        grid_spec=grid_spec,
        compiler_params=_cparams(("arbitrary",)),
        name="moe_dispatch",
    )(tabs["seg"], tabs["loc"], tabs["dst"], tabs["total"], tabs["tail"], tabs["tail_dst"], h2, slot)


def _expert_kernel(be_ref, first_ref, par_ref, nexte_ref, meta_ref, x_ref, wgu_hbm, bgu_ref, wd_hbm, bd_ref,
                   y_ref, wgu_f, wd_f, wgu_bf, wd_bf, sem, *, layer):
    i = pl.program_id(0)
    n_used = meta_ref[0]

    def weight_copies(e, buf):
        rows = D_MODEL // WEIGHT_DMA_PARTS
        parts = []
        for p in range(WEIGHT_DMA_PARTS):
            sl = pl.ds(p * rows, rows)
            parts.append(pltpu.make_async_copy(wgu_hbm.at[layer, e, sl], wgu_f.at[buf, sl], sem.at[buf]))
            parts.append(pltpu.make_async_copy(wd_hbm.at[layer, e, sl], wd_f.at[buf, sl], sem.at[buf]))
        return parts

    @pl.when(i < n_used)
    def _():
        e = be_ref[i]
        buf = par_ref[i]
        is_first = first_ref[i] == 1

        @pl.when(is_first)
        def _():
            @pl.when(i == 0)
            def _():
                for cp in weight_copies(e, buf):
                    cp.start()
            for cp in weight_copies(e, buf):
                cp.wait()
            wgu_bf[...] = wgu_f[buf].astype(BF16)
            wd_bf[...] = wd_f[buf].astype(BF16)

        x = x_ref[...]
        y = bd_ref[0, 0]
        for c in range(D_FF // FF_SLAB):
            g_cols = slice(c * FF_SLAB, (c + 1) * FF_SLAB)
            l_cols = slice(D_FF + c * FF_SLAB, D_FF + (c + 1) * FF_SLAB)
            g = jnp.dot(x, wgu_bf[:, g_cols], preferred_element_type=F32) + bgu_ref[0, 0, :, g_cols]
            li = jnp.dot(x, wgu_bf[:, l_cols], preferred_element_type=F32) + bgu_ref[0, 0, :, l_cols]
            glu = jnp.minimum(g, SWIGLU_LIMIT)
            lin = jnp.clip(li, -SWIGLU_LIMIT, SWIGLU_LIMIT)
            act = glu * _sigmoid(SWIGLU_ALPHA * glu) * (lin + 1.0)
            y = y + jnp.dot(act.astype(BF16), wd_bf[g_cols, :], preferred_element_type=F32)
        y_ref[...] = y.astype(BF16)

        nxt = nexte_ref[i]

        @pl.when(is_first & (nxt >= 0))
        def _():
            for cp in weight_copies(nxt, 1 - buf):
                cp.start(priority=1)


def _expert_call(tabs, xs, w_gu, b_gu, w_down, b_down, layer):
    def bmap(i, be, first, par, nxt, meta_):
        return (layer, be[i], 0, 0)

    def rmap(i, be, first, par, nxt, meta_):
        return (jnp.minimum(i, meta_[0] - 1), 0)

    grid_spec = pltpu.PrefetchScalarGridSpec(
        num_scalar_prefetch=5,
        grid=(N_SLOT_BLOCKS,),
        in_specs=[pl.BlockSpec((TM_EXP, D_MODEL), rmap),
                  pl.BlockSpec(memory_space=pl.ANY),
                  pl.BlockSpec((1, 1, 1, 2 * D_FF), bmap),
                  pl.BlockSpec(memory_space=pl.ANY),
                  pl.BlockSpec((1, 1, 1, D_MODEL), bmap)],
        out_specs=pl.BlockSpec((TM_EXP, D_MODEL), rmap),
        scratch_shapes=[pltpu.VMEM((2, D_MODEL, 2 * D_FF), F32), pltpu.VMEM((2, D_FF, D_MODEL), F32),
                        pltpu.VMEM((D_MODEL, 2 * D_FF), BF16), pltpu.VMEM((D_FF, D_MODEL), BF16),
                        pltpu.SemaphoreType.DMA((2,))],
    )
    return pl.pallas_call(
        functools.partial(_expert_kernel, layer=layer),
        out_shape=jax.ShapeDtypeStruct((N_SLOTS, D_MODEL), BF16),
        grid_spec=grid_spec,
        compiler_params=_cparams(("arbitrary",)),
        name="moe_experts",
    )(tabs["block_e"], tabs["first"], tabs["parity"], tabs["next_e"], tabs["n_used"], xs, w_gu,
      b_gu.reshape(DEPTH, N_EXPERTS, 1, 2 * D_FF), w_down, b_down.reshape(DEPTH, N_EXPERTS, 1, D_MODEL))


def _combine_kernel(seg_ref, loc_ref, dst_ref, total_ref, ys_hbm, x1_ref, slot_ref, gates_ref, mod_ref,
                    fg_ref, *rest, final):
    *o_refs, ybuf, sem = rest
    j = pl.program_id(0)
    n = pl.num_programs(0)
    buf = j % 2

    tabs = (seg_ref, loc_ref, dst_ref, total_ref)

    @pl.when(j == 0)
    def _():
        _segment_copies(ybuf.at[0], ys_hbm, sem.at[0], 0, *tabs, False, False)

    @pl.when(j + 1 < n)
    def _():
        _segment_copies(ybuf.at[1 - buf], ys_hbm, sem.at[1 - buf], j + 1, *tabs, False, False)

    _segment_copies(ybuf.at[buf], ys_hbm, sem.at[buf], j, *tabs, False, True)

    slots = slot_ref[...]
    gates = gates_ref[...]
    total = total_ref[j] * SEG_ALIGN
    ff = jnp.zeros((TM_TOK, D_MODEL), F32)
    for c in range(LOCAL_ROWS // TM_TOK):
        col = lax.broadcasted_iota(jnp.int32, (TM_TOK, TM_TOK), 1) + c * TM_TOK
        q = jnp.zeros((TM_TOK, TM_TOK), F32)
        for k in range(TOP_K):
            q = jnp.where(col == slots[:, k:k + 1], gates[:, k:k + 1], q)
        rows = lax.broadcasted_iota(jnp.int32, (TM_TOK, D_MODEL), 0) + c * TM_TOK
        y = jnp.where(rows < total, ybuf[buf, c * TM_TOK:(c + 1) * TM_TOK, :], jnp.zeros((), BF16))
        ff = ff + jnp.dot(q.astype(BF16), y, preferred_element_type=F32)
    x = x1_ref[...] + mod_ref[0, 5:6, :] * ff
    if final:
        x = _rms(x, fg_ref[...])
        ctx_ref, lat_ref = o_refs
        first_lat = T_CTX // TM_TOK

        @pl.when(j < first_lat)
        def _():
            ctx_ref[...] = x

        @pl.when(j >= first_lat)
        def _():
            lat_ref[...] = x
    else:
        o_refs[0][...] = x


def _combine_call(tabs, ys, x1, slot, gates, mods_l, final_g, final):
    n_blk = T_ALL // TM_TOK
    row = lambda j, *_: (j, 0)
    first_lat = T_CTX // TM_TOK
    if final:
        out_shape = (jax.ShapeDtypeStruct((T_CTX, D_MODEL), F32), jax.ShapeDtypeStruct((T_LAT, D_MODEL), F32))
        out_specs = (pl.BlockSpec((TM_TOK, D_MODEL), lambda j, *_: (jnp.minimum(j, first_lat - 1), 0)),
                     pl.BlockSpec((TM_TOK, D_MODEL), lambda j, *_: (jnp.maximum(j - first_lat, 0), 0)))
    else:
        out_shape = jax.ShapeDtypeStruct((T_ALL, D_MODEL), F32)
        out_specs = pl.BlockSpec((TM_TOK, D_MODEL), row)
    grid_spec = pltpu.PrefetchScalarGridSpec(
        num_scalar_prefetch=4,
        grid=(n_blk,),
        in_specs=[pl.BlockSpec(memory_space=pl.ANY),
                  pl.BlockSpec((TM_TOK, D_MODEL), row),
                  pl.BlockSpec((TM_TOK, TOP_K), row),
                  pl.BlockSpec((TM_TOK, TOP_K), row),
                  pl.BlockSpec((1, N_MOD, D_MODEL), lambda j, *_: (_group_of_block(j, TM_TOK), 0, 0)),
                  pl.BlockSpec((1, D_MODEL), lambda j, *_: (0, 0))],
        out_specs=out_specs,
        scratch_shapes=[pltpu.VMEM((2, LOCAL_ROWS, D_MODEL), BF16), pltpu.SemaphoreType.DMA((2,))],
    )
    return pl.pallas_call(
        functools.partial(_combine_kernel, final=final),
        out_shape=out_shape,
        grid_spec=grid_spec,
        compiler_params=_cparams(("arbitrary",)),
        name="moe_combine_final" if final else "moe_combine",
    )(tabs["seg"], tabs["loc"], tabs["dst"], tabs["total"], ys, x1, slot, gates, mods_l, final_g)


def _routing_tables(cnt):
    i32 = jnp.int32
    cnt = cnt.reshape(T_ALL // TM_TOK, N_EXPERTS)
    seg = (cnt + SEG_ALIGN - 1) // SEG_ALIGN * SEG_ALIGN
    loc = jnp.cumsum(seg, axis=1) - seg
    total = jnp.sum(seg, axis=1)
    rows_e = jnp.sum(seg, axis=0)
    region = (rows_e + TM_EXP - 1) // TM_EXP * TM_EXP
    region_end = jnp.cumsum(region)
    base = region_end - region
    dst = base[None, :] + jnp.cumsum(seg, axis=0) - seg
    n_used = jnp.maximum(region_end[-1] // TM_EXP, 1).astype(i32)
    blk = jnp.minimum(jnp.arange(N_SLOT_BLOCKS, dtype=i32), n_used - 1)
    block_e = jnp.sum((region_end[None, :] <= (blk * TM_EXP)[:, None]).astype(i32), axis=1)
    block_e = jnp.minimum(block_e, N_EXPERTS - 1).astype(i32)
    first = jnp.concatenate([jnp.ones((1,), i32), (block_e[1:] != block_e[:-1]).astype(i32)])
    parity = (jnp.cumsum(first) - 1) % 2
    e_ids = jnp.arange(N_EXPERTS, dtype=i32)
    later_used = (e_ids[None, :] > e_ids[:, None]) & (region[None, :] > 0)
    next_used = jnp.min(jnp.where(later_used, e_ids[None, :], N_EXPERTS), axis=1)
    next_e = next_used[block_e]
    next_e = jnp.where(next_e < N_EXPERTS, next_e, -1)
    flat = lambda a: a.reshape(-1).astype(i32)
    return dict(seg=flat(seg // SEG_ALIGN), loc=flat(loc), dst=flat(dst), total=flat(total // SEG_ALIGN),
                tail=flat((region - rows_e) // SEG_ALIGN), tail_dst=flat(base + rows_e),
                block_e=block_e, first=first, parity=flat(parity), next_e=flat(next_e),
                n_used=n_used.reshape(1))


def kernel(x_prompt, x_sample, cache_na_k, cache_na_v, cache_gqa_k, cache_gqa_v, state_rwkv, c, c_ctx,
           w_mod, b_mod, norm1_g, norm2_g, w_in, rw_shift, rw_w0, rw_w_up, rw_a0, rw_a_up, rw_g_up,
           rw_k_k, rw_k_a, rw_r_k, rw_ln_g, rw_ln_b, na_rpb, q_norm, k_norm, w_out, router_w, router_b,
           moe_w_gu, moe_b_gu, moe_w_down, moe_b_down, final_norm_g):
    x = jnp.concatenate([x_prompt.reshape(T_CTX, D_MODEL), x_sample.reshape(T_LAT, D_MODEL)], axis=0)
    cvecs = jnp.concatenate([c_ctx[None, :], c, jnp.zeros((SUBLANES - N_GROUPS, D_MODEL), F32)], axis=0)
    mods = _mods_call(cvecs, w_mod, b_mod)
    mods = mods[:, :N_GROUPS].reshape(DEPTH, N_GROUPS, N_MOD, D_MODEL)
    rope = _rope_tables()
    w_in_bf = w_in.astype(BF16)
    w_out_bf = w_out.astype(BF16)
    rw_g_up_bf = rw_g_up.astype(BF16)
    final_g = final_norm_g.reshape(1, D_MODEL)

    na_tb = _na_bias_tables(na_rpb)
    caches = None
    states = None
    for l in range(DEPTH):
        lp = {
            "rw_shift": rw_shift[l], "rw_k_k": rw_k_k[l].reshape(1, RW_WIDTH),
            "rw_k_a": rw_k_a[l].reshape(1, RW_WIDTH), "rw_r_k": rw_r_k[l].reshape(1, RW_WIDTH),
            "rw_w0": rw_w0[l], "rw_w_up": rw_w_up[l], "rw_a0": rw_a0[l], "rw_a_up": rw_a_up[l],
            "rw_g_up": rw_g_up_bf[l], "rw_ln_g": rw_ln_g[l].reshape(1, RW_WIDTH),
            "rw_ln_b": rw_ln_b[l].reshape(1, RW_WIDTH), "w_out": w_out_bf[l],
            "norm2_g": norm2_g[l].reshape(1, D_MODEL), "router_w": router_w[l],
            "router_b": router_b[l].reshape(1, N_EXPERTS),
        }
        qk_g = jnp.concatenate([jnp.tile(q_norm[l], GQA_Q_HEADS), jnp.tile(k_norm[l], GQA_KV_HEADS)])
        zrw, zna, zgq, zgkv = _in_proj_call(x, mods[l], norm1_g[l].reshape(1, D_MODEL), w_in_bf[l],
                                            qk_g.reshape(1, GQ_WIDTH + GKV_WIDTH), rope)

        s0_lat = state_rwkv[:, l]
        rw_ctx = _rwkv_call(zrw, s0_lat, lp, (), n_seq=BATCH, seq_len=SEQ, row_base=0,
                            has_init=False, emit_state=True, state_prev=states, layer=l)
        yf, yb, bonus, gate = _rwkv_call(zrw, s0_lat, lp, rw_ctx[:4], n_seq=DEC_BATCH, seq_len=DEC_SEQ,
                                         row_base=T_CTX, has_init=True, emit_state=False)
        states = rw_ctx[4]

        att, *caches = _ctx_attn_call(zna, zgq, zgkv, caches, l)
        att = _lat_na_call(zna, cache_na_k[:, l].reshape(DEC_BATCH, PAST_LEN, NA_WIDTH),
                           cache_na_v[:, l].reshape(DEC_BATCH, PAST_LEN, NA_WIDTH), na_tb[l], att)
        att = _lat_gqa_call(zgq, zgkv, cache_gqa_k[:, l].reshape(DEC_BATCH, PAST_LEN, GKV_WIDTH),
                            cache_gqa_v[:, l].reshape(DEC_BATCH, PAST_LEN, GKV_WIDTH), att)

        x1, h2, slot, gates, counts = _out_proj_call(x, yf, yb, bonus, gate, att, mods[l], lp)
        tabs = _routing_tables(counts)
        xs = _dispatch_call(tabs, h2, slot)
        ys = _expert_call(tabs, xs, moe_w_gu, moe_b_gu, moe_w_down, moe_b_down, l)
        x = _combine_call(tabs, ys, x1, slot, gates, mods[l], final_g, l == DEPTH - 1)

    y_prompt = x[0].reshape(BATCH, SEQ, D_MODEL)
    y_sample = x[1].reshape(DEC_BATCH, DEC_SEQ, D_MODEL)
    heads = lambda t, n: t.reshape(BATCH, DEPTH, SEQ, n, HEAD_DIM)
    return (y_prompt, y_sample, heads(caches[0], NA_HEADS), heads(caches[1], NA_HEADS),
            heads(caches[2], GQA_KV_HEADS), heads(caches[3], GQA_KV_HEADS), states)
```

```python
import functools

import numpy as np
import jax
import jax.numpy as jnp
from jax import lax
from jax.experimental import pallas as pl
from jax.experimental.pallas import tpu as pltpu

F32 = jnp.float32
BF16 = jnp.bfloat16

D_MODEL = 1024
BATCH = 32
SEQ = 256
DEPTH = 4
DEC_BATCH = 2
DEC_SEQ = 2048
PAST_LEN = 512
GRID_W = 64
GRID_H = DEC_SEQ // GRID_W
HEAD_DIM = 64
RW_HEADS = 4
RW_WIDTH = RW_HEADS * HEAD_DIM
DECAY_LORA = 64
ICLR_LORA = 64
GATE_LORA = 128
NA_HEADS = 4
NA_WIDTH = NA_HEADS * HEAD_DIM
NA_WIN_ROWS = 8
NA_WIN_COLS = 16
GQA_Q_HEADS = 8
GQA_KV_HEADS = 2
GQA_GROUP = GQA_Q_HEADS // GQA_KV_HEADS
GQ_WIDTH = GQA_Q_HEADS * HEAD_DIM
GKV_WIDTH = GQA_KV_HEADS * HEAD_DIM
RW_PROJ = 3 * RW_WIDTH + DECAY_LORA + ICLR_LORA + GATE_LORA
NA_PROJ = 3 * NA_WIDTH
GQA_PROJ = GQ_WIDTH + 2 * GKV_WIDTH
IN_PROJ = RW_PROJ + NA_PROJ + GQA_PROJ
ATT_WIDTH = GQ_WIDTH + NA_WIDTH
N_EXPERTS = 32
TOP_K = 4
D_FF = D_MODEL
SWIGLU_LIMIT = 7.0
SWIGLU_ALPHA = 1.702
ROPE_THETA = 10000.0
NORM_EPS = 1e-6
LNX_EPS = 64e-5
N_MOD = 6
ATT_SCALE = HEAD_DIM ** -0.5
NEG_BIG = -1e30

T_CTX = BATCH * SEQ
T_LAT = DEC_BATCH * DEC_SEQ
T_ALL = T_CTX + T_LAT
N_GROUPS = 1 + DEC_BATCH

LANES = 128
SUBLANES = 8
TM_TOK = 512
CHUNK = 64
RW_SEQ_GROUP = 4
TQ_GQA = 512
CTX_SEQ_GROUP = 1
TM_EXP = 256
WEIGHT_DMA_PARTS = 4
FF_SLAB = 1024
SEG_ALIGN = 16
SEG_MAX_BIT = (TM_TOK // SEG_ALIGN).bit_length() - 1
TAIL_MAX_BIT = (TM_EXP // SEG_ALIGN - 1).bit_length() - 1
TOTAL_MAX_BIT = (-(-(TOP_K * TM_TOK + N_EXPERTS * (SEG_ALIGN - 1)) // SEG_ALIGN)).bit_length() - 1
LOCAL_ROWS = -(-(TOP_K * TM_TOK + N_EXPERTS * (SEG_ALIGN - 1)) // TM_TOK) * TM_TOK
N_SLOT_BLOCKS = (-(-(T_ALL * TOP_K + (T_ALL // TM_TOK) * N_EXPERTS * (SEG_ALIGN - 1)) // TM_EXP)
                 + N_EXPERTS)
N_SLOTS = N_SLOT_BLOCKS * TM_EXP
VMEM_LIMIT = 56 * 1024 * 1024

NT_DIMS = (((1,), (1,)), ((), ()))
TN_DIMS = (((0,), (0,)), ((), ()))


def _bdot(a, b, dims=None):
    a = a.astype(BF16)
    b = b.astype(BF16)
    if dims is None:
        return jnp.dot(a, b, preferred_element_type=F32)
    return lax.dot_general(a, b, dims, preferred_element_type=F32)


def _split(a):
    hi = a.astype(BF16)
    lo = (a - hi.astype(F32)).astype(BF16)
    return hi, lo


def _dot3(a, b, dims=None):
    ah, al = _split(a)
    bh, bl = _split(b)
    return _bdot(ah, bh, dims) + _bdot(ah, bl, dims) + _bdot(al, bh, dims)


def _dot_exact_lhs(a_exact, b):
    h1 = b.astype(BF16)
    r1 = b - h1.astype(F32)
    h2 = r1.astype(BF16)
    h3 = (r1 - h2.astype(F32)).astype(BF16)
    return _bdot(a_exact, h1) + _bdot(a_exact, h2) + _bdot(a_exact, h3)


def _head_ones(n):
    r = lax.broadcasted_iota(jnp.int32, (n, n), 0) // HEAD_DIM
    c = lax.broadcasted_iota(jnp.int32, (n, n), 1) // HEAD_DIM
    return (r == c).astype(BF16)


def _head_sum(x, ones_bd):
    hi, lo = _split(x)
    return (jnp.dot(hi, ones_bd, preferred_element_type=F32)
            + jnp.dot(lo, ones_bd, preferred_element_type=F32))


def _sigmoid(x):
    return 1.0 / (1.0 + jnp.exp(-x))


def _cparams(sem):
    return pltpu.CompilerParams(dimension_semantics=sem, vmem_limit_bytes=VMEM_LIMIT)


def _group_of_block(i, rows_per_block):
    first_lat = T_CTX // rows_per_block
    per_sample = DEC_SEQ // rows_per_block
    return jnp.where(i < first_lat, 0, 1 + (i - first_lat) // per_sample)


def _mods_kernel(c_ref, w_ref, b_ref, o_ref):
    c = c_ref[...]
    s = c * _sigmoid(c)
    o_ref[0] = _dot3(s, w_ref[0]) + b_ref[0]


def _mods_call(cvecs, w_mod, b_mod):
    tn = 1536
    n_rows = cvecs.shape[0]
    return pl.pallas_call(
        _mods_kernel,
        out_shape=jax.ShapeDtypeStruct((DEPTH, n_rows, N_MOD * D_MODEL), F32),
        grid=(DEPTH, N_MOD * D_MODEL // tn),
        in_specs=[
            pl.BlockSpec((n_rows, D_MODEL), lambda l, j: (0, 0)),
            pl.BlockSpec((1, D_MODEL, tn), lambda l, j: (l, 0, j)),
            pl.BlockSpec((1, 1, tn), lambda l, j: (l, 0, j)),
        ],
        out_specs=pl.BlockSpec((1, n_rows, tn), lambda l, j: (l, 0, j)),
        compiler_params=_cparams(("arbitrary", "arbitrary")),
        name="adaln_mods",
    )(cvecs, w_mod, b_mod.reshape(DEPTH, 1, N_MOD * D_MODEL))


def _rms(x, g):
    ms = jnp.mean(x * x, axis=-1, keepdims=True)
    return x * lax.rsqrt(ms + NORM_EPS) * g


def _in_proj_kernel(x_ref, mod_ref, g_ref, w_ref, qkg_ref, rc_ref, rs1_ref, rs2_ref,
                    zrw_ref, zna_ref, zgq_ref, zgkv_ref):
    x = x_ref[...]
    shift1 = mod_ref[0, 0:1, :]
    scale1 = mod_ref[0, 1:2, :]
    h = _rms(x, g_ref[...]) * (1.0 + scale1) + shift1
    z = jnp.dot(h.astype(BF16), w_ref[...], preferred_element_type=F32)
    zrw_ref[...] = z[:, :RW_PROJ]
    zna_ref[...] = z[:, RW_PROJ:RW_PROJ + NA_PROJ]
    qk_w = GQ_WIDTH + GKV_WIDTH
    qk = z[:, RW_PROJ + NA_PROJ:RW_PROJ + NA_PROJ + qk_w]
    ones_bd = _head_ones(LANES)
    sq = qk * qk
    ssq = jnp.concatenate(
        [_head_sum(sq[:, j * LANES:(j + 1) * LANES], ones_bd) for j in range(qk_w // LANES)], axis=1)
    qkn = qk * lax.rsqrt(ssq * (1.0 / HEAD_DIM) + NORM_EPS) * qkg_ref[...]
    reps = qk_w // LANES
    rc = jnp.concatenate([rc_ref[...]] * reps, axis=1)
    rs1 = jnp.concatenate([rs1_ref[...]] * reps, axis=1)
    rs2 = jnp.concatenate([rs2_ref[...]] * reps, axis=1)
    half = HEAD_DIM // 4
    qkr = qkn * rc + pltpu.roll(qkn, half, 1) * rs1 + pltpu.roll(qkn, qk_w - half, 1) * rs2
    zgq_ref[...] = qkr[:, :GQ_WIDTH]
    zgkv_ref[:, :GKV_WIDTH] = qkr[:, GQ_WIDTH:]
    zgkv_ref[:, GKV_WIDTH:] = z[:, RW_PROJ + NA_PROJ + qk_w:]


def _rope_tables():
    t = np.arange(DEC_SEQ)
    pos = np.stack([t // GRID_W, t % GRID_W], axis=1).astype(np.float32)
    axis_dim = HEAD_DIM // 2
    inv = ROPE_THETA ** (-np.arange(0, axis_dim, 2, dtype=np.float32) / axis_dim)
    d = np.arange(LANES) % HEAD_DIM
    part = d // axis_dim
    within = d % axis_dim
    freq = within % (axis_dim // 2)
    second = within // (axis_dim // 2)
    ang = jnp.asarray(pos)[:, part] * jnp.asarray(inv)[freq][None, :]
    cos = jnp.cos(ang)
    sin = jnp.sin(ang)
    s1 = jnp.where(second[None, :] == 1, sin, 0.0)
    s2 = jnp.where(second[None, :] == 0, -sin, 0.0)
    ident = jnp.ones((TM_TOK, LANES), F32)
    zero = jnp.zeros((TM_TOK, LANES), F32)
    return (jnp.concatenate([cos, ident], 0), jnp.concatenate([s1, zero], 0),
            jnp.concatenate([s2, zero], 0))


def _in_proj_call(x, mods_l, g1, w_in_bf, qk_g, rope):
    n_blk = T_ALL // TM_TOK
    lat_blk = DEC_SEQ // TM_TOK
    first_lat = T_CTX // TM_TOK

    def rope_idx(i):
        return (jnp.where(i < first_lat, lat_blk, (i - first_lat) % lat_blk), 0)

    row = lambda i: (i, 0)
    rope_spec = pl.BlockSpec((TM_TOK, LANES), rope_idx)
    return pl.pallas_call(
        _in_proj_kernel,
        out_shape=(jax.ShapeDtypeStruct((T_ALL, RW_PROJ), F32),
                   jax.ShapeDtypeStruct((T_ALL, NA_PROJ), F32),
                   jax.ShapeDtypeStruct((T_ALL, GQ_WIDTH), F32),
                   jax.ShapeDtypeStruct((T_ALL, 2 * GKV_WIDTH), F32)),
        grid=(n_blk,),
        in_specs=[
            pl.BlockSpec((TM_TOK, D_MODEL), row),
            pl.BlockSpec((1, N_MOD, D_MODEL), lambda i: (_group_of_block(i, TM_TOK), 0, 0)),
            pl.BlockSpec((1, D_MODEL), lambda i: (0, 0)),
            pl.BlockSpec((D_MODEL, IN_PROJ), lambda i: (0, 0)),
            pl.BlockSpec((1, GQ_WIDTH + GKV_WIDTH), lambda i: (0, 0)),
            rope_spec, rope_spec, rope_spec,
        ],
        out_specs=(pl.BlockSpec((TM_TOK, RW_PROJ), row), pl.BlockSpec((TM_TOK, NA_PROJ), row),
                   pl.BlockSpec((TM_TOK, GQ_WIDTH), row), pl.BlockSpec((TM_TOK, 2 * GKV_WIDTH), row)),
        compiler_params=_cparams(("arbitrary",)),
        name="in_proj",
    )(x, mods_l, g1, w_in_bf, qk_g, *rope)


def _softplus(x):
    return jnp.maximum(x, 0.0) + jnp.log(1.0 + jnp.exp(-jnp.abs(x)))


def _rw_pre(z, zprev, znext, shift_ref, kk_ref, ka_ref, rk_ref, w0_ref, wup_ref, a0_ref, aup_ref, d,
            ones_bd):
    rows = lax.broadcasted_iota(jnp.int32, z.shape, 0)
    zp = jnp.where(rows == 0, zprev, pltpu.roll(z, 1, 0))
    zn = jnp.where(rows == CHUNK - 1, znext, pltpu.roll(z, CHUNK - 1, 0))
    zs = zp * shift_ref[0:1, :] + z * shift_ref[1:2, :] + zn * shift_ref[2:3, :]
    r = zs[:, 0:RW_WIDTH]
    k = zs[:, RW_WIDTH:2 * RW_WIDTH]
    v = zs[:, 2 * RW_WIDTH:3 * RW_WIDTH]
    o = 3 * RW_WIDTH
    wd = zs[:, o:o + DECAY_LORA]
    ad = zs[:, o + DECAY_LORA:o + DECAY_LORA + ICLR_LORA]
    gd = zs[:, o + DECAY_LORA + ICLR_LORA:]
    kk = k * kk_ref[...]
    kk = kk / jnp.maximum(jnp.sqrt(_head_sum(kk * kk, ones_bd)), 1e-12)
    tw = jnp.tanh(wd)
    wl = w0_ref[d:d + 1, :] + _dot3(tw, wup_ref[d])
    lw = -jnp.exp(-_softplus(-wl) - 0.5)
    a_sig = _sigmoid(a0_ref[d:d + 1, :] + _dot3(ad, aup_ref[d]))
    k_d = k * (1.0 + (a_sig - 1.0) * ka_ref[...])
    bonus = _head_sum(r * k_d * rk_ref[...], ones_bd) * v
    return dict(r=r, k=k_d, v=v, a=-kk, b=kk * a_sig, lw=lw, bonus=bonus, gd=gd, ad=ad, k_raw=k)


def _chunk_masks(rev):
    t = lax.broadcasted_iota(jnp.int32, (CHUNK, CHUNK), 0)
    j = lax.broadcasted_iota(jnp.int32, (CHUNK, CHUNK), 1)
    return ((j >= t), (j > t)) if rev else ((j <= t), (j < t))


def _wkv_scale(p, incl, rev):
    lw = p["lw"]
    cs = _dot_exact_lhs(incl.astype(BF16), lw)
    tot = cs[0:1, :] if rev else cs[CHUNK - 1:CHUNK, :]
    e_inv = jnp.exp(-cs)
    e_rem = jnp.exp(tot - cs)
    bf = lambda x: x.astype(BF16)
    return dict(at=bf(p["a"] * jnp.exp(cs - lw)), rt=bf(p["r"] * jnp.exp(cs)),
                bt=bf(p["b"] * e_inv), kt=bf(p["k"] * e_inv),
                bh=bf(p["b"] * e_rem), kh=bf(p["k"] * e_rem), v=bf(p["v"]), gtot=jnp.exp(tot))


def _wkv_chunks(items):
    bf = lambda x: x.astype(BF16)
    c = CHUNK
    n_sq = int(np.log2(c))
    ar = [jnp.concatenate([it["at"], it["rt"]], 0) for it in items]
    m_b = [_bdot(a, it["bt"], NT_DIMS) for a, it in zip(ar, items)]
    m_k = [_bdot(a, it["kt"], NT_DIMS) for a, it in zip(ar, items)]
    s_bf = [bf(it["s"]) for it in items]
    xs = [bf(jnp.where(it["strict"], m[:c], 0.0)) for m, it in zip(m_b, items)]
    a_rb = [bf(jnp.where(it["incl"], m[c:], 0.0)) for m, it in zip(m_b, items)]
    a_ak = [bf(jnp.where(it["strict"], m[:c], 0.0)) for m, it in zip(m_k, items)]
    a_rk = [bf(jnp.where(it["incl"], m[c:], 0.0)) for m, it in zip(m_k, items)]
    ws = [_bdot(ak, it["v"]) + _bdot(it["at"], s, NT_DIMS) for ak, it, s in zip(a_ak, items, s_bf)]
    for step in range(n_sq):
        wb = [bf(w) for w in ws]
        ws = [w + _bdot(x, b) for w, x, b in zip(ws, xs, wb)]
        if step < n_sq - 1:
            xs = [bf(_bdot(x, x)) for x in xs]
    wb = [bf(w) for w in ws]
    ys = [_bdot(rb, w) + _bdot(rk, it["v"]) + _bdot(it["rt"], s, NT_DIMS)
          for rb, rk, w, it, s in zip(a_rb, a_rk, wb, items, s_bf)]
    s_new = [it["s"] * it["gtot"] + _bdot(w, it["bh"], TN_DIMS) + _bdot(it["v"], it["kh"], TN_DIMS)
             for w, it in zip(wb, items)]
    return ys, s_new


def _rwkv_kernel(zf_ref, zfp_ref, zfn_ref, zb_ref, zbp_ref, zbn_ref, s0_ref,
                 shift_ref, kk_ref, ka_ref, rk_ref, w0_ref, wup_ref, a0_ref, aup_ref, gup_ref,
                 *refs, n_chunks, n_group, has_init, emit_state, n_alias):
    refs = refs[n_alias:]
    if emit_state:
        yf_ref, yb_ref, bonus_ref, gate_ref, st_ref, h_ref = refs
    else:
        yf_ref, yb_ref, bonus_ref, gate_ref, h_ref = refs
        st_ref = None
    i = pl.program_id(1)
    ones_bd = _head_ones(RW_WIDTH)

    @pl.when(i == 0)
    def _():
        if has_init:
            h_ref[...] = s0_ref[...]
        else:
            h_ref[...] = jnp.zeros(h_ref.shape, F32)

    params = (shift_ref, kk_ref, ka_ref, rk_ref, w0_ref, wup_ref, a0_ref, aup_ref)
    first = i == 0
    last = i == n_chunks - 1
    zero_row = jnp.zeros((1, RW_PROJ), F32)
    masks = (_chunk_masks(False), _chunk_masks(True))
    items = []
    for g in range(n_group):
        pf = _rw_pre(zf_ref[g], jnp.where(first, zero_row, zfp_ref[g, 0, SUBLANES - 1:SUBLANES, :]),
                     jnp.where(last, zero_row, zfn_ref[g, 0, 0:1, :]), *params, 0, ones_bd)
        pb = _rw_pre(zb_ref[g], jnp.where(last, zero_row, zbp_ref[g, 0, SUBLANES - 1:SUBLANES, :]),
                     jnp.where(first, zero_row, zbn_ref[g, 0, 0:1, :]), *params, 1, ones_bd)
        a_sig_b = _sigmoid(a0_ref[1:2, :] + _dot3(pf["ad"], aup_ref[1]))
        k_b = pf["k_raw"] * (1.0 + (a_sig_b - 1.0) * ka_ref[...])
        bonus_ref[g] = pf["bonus"] + _head_sum(pf["r"] * k_b * rk_ref[...], ones_bd) * pf["v"]
        gate_ref[g] = _bdot(_sigmoid(pf["gd"]), gup_ref[...])
        for d, p in ((0, pf), (1, pb)):
            incl, strict = masks[d]
            sc = _wkv_scale(p, incl, d == 1)
            for h in range(RW_HEADS):
                sl = slice(h * HEAD_DIM, (h + 1) * HEAD_DIM)
                it = {k: v[:, sl] for k, v in sc.items()}
                it.update(s=h_ref[g, d, h], incl=incl, strict=strict, where=(g, d, h))
                items.append(it)

    ys, s_new = _wkv_chunks(items)
    for it, y, s in zip(items, ys, s_new):
        g, d, h = it["where"]
        y_ref = yb_ref if d else yf_ref
        y_ref[g, :, h * HEAD_DIM:(h + 1) * HEAD_DIM] = y
        h_ref[g, d, h] = s

    if emit_state:
        @pl.when(last)
        def _():
            for it, s in zip(items, s_new):
                g, d, h = it["where"]
                st_ref[g, 0, d, h] = s


def _rwkv_call(zrw, s0, lp, prev_outs, *, n_seq, seq_len, row_base, has_init, emit_state,
               state_prev=None, layer=0):
    g = min(RW_SEQ_GROUP, n_seq)
    n_chunks = seq_len // CHUNK
    n_rows8 = seq_len // SUBLANES
    per8 = CHUNK // SUBLANES
    n_view = T_ALL // seq_len
    base_g = row_base // seq_len // g
    z3 = zrw.reshape(n_view, seq_len, RW_PROJ)
    z4 = zrw.reshape(n_view, n_rows8, SUBLANES, RW_PROJ)

    fwd = lambda i: i
    bwd = lambda i: n_chunks - 1 - i
    main = lambda c: pl.BlockSpec((g, CHUNK, RW_PROJ), lambda b, i: (base_g + b, c(i), 0))
    prev8 = lambda c: pl.BlockSpec((g, 1, SUBLANES, RW_PROJ),
                                   lambda b, i: (base_g + b, jnp.maximum(c(i) * per8 - 1, 0), 0, 0))
    next8 = lambda c: pl.BlockSpec((g, 1, SUBLANES, RW_PROJ),
                                   lambda b, i: (base_g + b, jnp.minimum((c(i) + 1) * per8, n_rows8 - 1), 0, 0))
    full = lambda shape: pl.BlockSpec(shape, lambda b, i: (0,) * len(shape))
    out_f = pl.BlockSpec((g, CHUNK, RW_WIDTH), lambda b, i: (base_g + b, i, 0))
    out_b = pl.BlockSpec((g, CHUNK, RW_WIDTH), lambda b, i: (base_g + b, bwd(i), 0))
    tok = jax.ShapeDtypeStruct((n_view, seq_len, RW_WIDTH), F32)
    out_shape = [tok, tok, tok, tok]
    out_specs = [out_f, out_b, out_f, out_f]
    state_block = (g, 2, RW_HEADS, HEAD_DIM, HEAD_DIM)
    prev_views = [p.reshape(n_view, seq_len, RW_WIDTH) for p in prev_outs]
    n_in = 16
    aliases = {n_in + j: j for j in range(len(prev_views))}
    if emit_state:
        out_shape.append(jax.ShapeDtypeStruct((n_seq, DEPTH, 2, RW_HEADS, HEAD_DIM, HEAD_DIM), F32))
        out_specs.append(pl.BlockSpec((g, 1, 2, RW_HEADS, HEAD_DIM, HEAD_DIM),
                                      lambda b, i: (b, layer, 0, 0, 0, 0)))
        if state_prev is not None:
            aliases[n_in + len(prev_views)] = 4
            prev_views.append(state_prev)
    n_alias = len(prev_views)
    state_spec = pl.BlockSpec(state_block, lambda b, i: (b if has_init else 0, 0, 0, 0, 0))
    kern = functools.partial(_rwkv_kernel, n_chunks=n_chunks, n_group=g, has_init=has_init,
                             emit_state=emit_state, n_alias=n_alias)
    outs = pl.pallas_call(
        kern,
        out_shape=tuple(out_shape),
        grid=(n_seq // g, n_chunks),
        in_specs=[main(fwd), prev8(fwd), next8(fwd), main(bwd), prev8(bwd), next8(bwd), state_spec,
                  full((3, RW_PROJ)), full((1, RW_WIDTH)), full((1, RW_WIDTH)), full((1, RW_WIDTH)),
                  full((2, RW_WIDTH)), full((2, DECAY_LORA, RW_WIDTH)),
                  full((2, RW_WIDTH)), full((2, ICLR_LORA, RW_WIDTH)), full((GATE_LORA, RW_WIDTH))]
                 + [pl.BlockSpec(memory_space=pl.ANY)] * n_alias,
        out_specs=tuple(out_specs),
        scratch_shapes=[pltpu.VMEM(state_block, F32)],
        input_output_aliases=aliases,
        compiler_params=_cparams(("arbitrary", "arbitrary")),
        name="rwkv_scan_init" if has_init else "rwkv_scan_zero",
    )(z3, z4, z4, z3, z4, z4, s0, lp["rw_shift"], lp["rw_k_k"], lp["rw_k_a"], lp["rw_r_k"],
      lp["rw_w0"], lp["rw_w_up"], lp["rw_a0"], lp["rw_a_up"], lp["rw_g_up"], *prev_views)
    return tuple(o.reshape(T_ALL, RW_WIDTH) for o in outs[:4]) + tuple(outs[4:])


def _softmax_pv(scores, values):
    m = scores[0].max(axis=-1, keepdims=True)
    for s in scores[1:]:
        m = jnp.maximum(m, s.max(axis=-1, keepdims=True))
    es = [jnp.exp(s - m) for s in scores]
    l = es[0].sum(axis=-1, keepdims=True)
    for e in es[1:]:
        l = l + e.sum(axis=-1, keepdims=True)
    o = _bdot(es[0], values[0])
    for e, v in zip(es[1:], values[1:]):
        o = o + _bdot(e, v)
    return o * (1.0 / l)


def _head(x, h):
    return x[:, h * HEAD_DIM:(h + 1) * HEAD_DIM]


def _ctx_attn_kernel(zna_ref, zgq_ref, zgkv_ref, *refs):
    att_ref, nk_ref, nv_ref, gk_ref, gv_ref = refs[-5:]
    for b in range(CTX_SEQ_GROUP):
        rows = slice(b * SEQ, (b + 1) * SEQ)
        zna = zna_ref[rows, :]
        q, k, v = zna[:, :NA_WIDTH], zna[:, NA_WIDTH:2 * NA_WIDTH], zna[:, 2 * NA_WIDTH:]
        nk_ref[b, 0] = k
        nv_ref[b, 0] = v
        gq = zgq_ref[rows, :]
        gkv = zgkv_ref[rows, :]
        gk, gv = gkv[:, :GKV_WIDTH], gkv[:, GKV_WIDTH:]
        gk_ref[b, 0] = gk
        gv_ref[b, 0] = gv
        gq, gk, gv = (gq * ATT_SCALE).astype(BF16), gk.astype(BF16), gv.astype(BF16)
        q, k, v = (q * ATT_SCALE).astype(BF16), k.astype(BF16), v.astype(BF16)
        for h in range(GQA_Q_HEADS):
            kv = h // GQA_GROUP
            s = _bdot(_head(gq, h), _head(gk, kv), NT_DIMS)
            att_ref[rows, h * HEAD_DIM:(h + 1) * HEAD_DIM] = _softmax_pv([s], [_head(gv, kv)]).astype(BF16)
        for h in range(NA_HEADS):
            s = _bdot(_head(q, h), _head(k, h), NT_DIMS)
            o = _softmax_pv([s], [_head(v, h)])
            att_ref[rows, GQ_WIDTH + h * HEAD_DIM:GQ_WIDTH + (h + 1) * HEAD_DIM] = o.astype(BF16)


def _ctx_attn_call(zna, zgq, zgkv, caches, layer):
    g = CTX_SEQ_GROUP
    row = lambda b: (b, 0)
    bat = lambda b: (b, layer, 0, 0)
    cache = lambda w: jax.ShapeDtypeStruct((BATCH, DEPTH, SEQ, w), F32)
    n_alias = 0 if caches is None else len(caches)
    return pl.pallas_call(
        _ctx_attn_kernel,
        out_shape=(jax.ShapeDtypeStruct((T_ALL, ATT_WIDTH), BF16),
                   cache(NA_WIDTH), cache(NA_WIDTH), cache(GKV_WIDTH), cache(GKV_WIDTH)),
        grid=(BATCH // g,),
        in_specs=[pl.BlockSpec((g * SEQ, NA_PROJ), row), pl.BlockSpec((g * SEQ, GQ_WIDTH), row),
                  pl.BlockSpec((g * SEQ, 2 * GKV_WIDTH), row)] + [pl.BlockSpec(memory_space=pl.ANY)] * n_alias,
        out_specs=(pl.BlockSpec((g * SEQ, ATT_WIDTH), row),
                   pl.BlockSpec((g, 1, SEQ, NA_WIDTH), bat), pl.BlockSpec((g, 1, SEQ, NA_WIDTH), bat),
                   pl.BlockSpec((g, 1, SEQ, GKV_WIDTH), bat), pl.BlockSpec((g, 1, SEQ, GKV_WIDTH), bat)),
        input_output_aliases={3 + j: 1 + j for j in range(n_alias)},
        compiler_params=_cparams(("arbitrary",)),
        name="ctx_attention",
    )(zna, zgq, zgkv, *(caches or ()))


def _lat_na_kernel(zna_ref, ck_ref, cv_ref, tb_ref, att_in_ref, att_ref, kv_bf, ckv_bf):
    del att_in_ref
    i = pl.program_id(0)

    @pl.when(i == 0)
    def _():
        for b in range(DEC_BATCH):
            kv_bf[b] = zna_ref[b, :, NA_WIDTH:].astype(BF16)
            ckv_bf[b, :, :NA_WIDTH] = ck_ref[b].astype(BF16)
            ckv_bf[b, :, NA_WIDTH:] = cv_ref[b].astype(BF16)

    start = jnp.clip(i - NA_WIN_ROWS // 2, 0, GRID_H - NA_WIN_ROWS)
    n_loc = NA_WIN_ROWS * GRID_W
    w0 = pl.multiple_of(start * GRID_W, GRID_W)
    dr0 = start - i + NA_WIN_ROWS - 1
    qs, kvs, cs = [], [], []
    for b in range(DEC_BATCH):
        q = zna_ref[b, pl.ds(pl.multiple_of(i * GRID_W, GRID_W), GRID_W), 0:NA_WIDTH]
        qs.append((q * ATT_SCALE).astype(BF16))
        kvs.append(kv_bf[b, pl.ds(w0, n_loc), :])
        cs.append(ckv_bf[b])
    outs = [[] for _ in range(DEC_BATCH)]
    for h in range(NA_HEADS):
        bias = jnp.concatenate([tb_ref[h, dr0 + r] for r in range(NA_WIN_ROWS)], axis=1)
        for b in range(DEC_BATCH):
            qh = _head(qs[b], h)
            s_loc = _bdot(qh, _head(kvs[b], h), NT_DIMS) + bias
            s_ctx = _bdot(qh, _head(cs[b], h), NT_DIMS)
            outs[b].append(_softmax_pv([s_loc, s_ctx],
                                       [_head(kvs[b], NA_HEADS + h), _head(cs[b], NA_HEADS + h)]))
    for b in range(DEC_BATCH):
        att_ref[b] = jnp.concatenate(outs[b], axis=1).astype(BF16)


def _na_bias_tables(rpb):
    w = np.arange(GRID_W)[:, None]
    kc = np.arange(GRID_W)[None, :]
    cs = np.clip(w - NA_WIN_COLS // 2, 0, GRID_W - NA_WIN_COLS)
    valid = (kc >= cs) & (kc < cs + NA_WIN_COLS)
    off = np.clip(kc - w + NA_WIN_COLS - 1, 0, 2 * NA_WIN_COLS - 2)
    onehot = (off[:, :, None] == np.arange(2 * NA_WIN_COLS - 1)).astype(np.float32)
    tb = jnp.einsum("lhrd,wkd->lhrwk", rpb, jnp.asarray(onehot), precision=lax.Precision.HIGHEST)
    return jnp.where(jnp.asarray(valid), tb, NEG_BIG).astype(F32)


def _lat_na_call(zna, ck, cv, tb, att):
    n_view = T_ALL // DEC_SEQ
    lat_blk = T_CTX // DEC_SEQ // DEC_BATCH
    out = pl.pallas_call(
        _lat_na_kernel,
        out_shape=jax.ShapeDtypeStruct((n_view, DEC_SEQ, ATT_WIDTH), BF16),
        grid=(GRID_H,),
        in_specs=[pl.BlockSpec((DEC_BATCH, DEC_SEQ, NA_PROJ), lambda i: (lat_blk, 0, 0)),
                  pl.BlockSpec((DEC_BATCH, PAST_LEN, NA_WIDTH), lambda i: (0, 0, 0)),
                  pl.BlockSpec((DEC_BATCH, PAST_LEN, NA_WIDTH), lambda i: (0, 0, 0)),
                  pl.BlockSpec((NA_HEADS, 2 * NA_WIN_ROWS - 1, GRID_W, GRID_W), lambda i: (0, 0, 0, 0)),
                  pl.BlockSpec(memory_space=pl.ANY)],
        out_specs=pl.BlockSpec((DEC_BATCH, GRID_W, NA_WIDTH), lambda i: (lat_blk, i, GQ_WIDTH // NA_WIDTH)),
        scratch_shapes=[pltpu.VMEM((DEC_BATCH, DEC_SEQ, 2 * NA_WIDTH), BF16),
                        pltpu.VMEM((DEC_BATCH, PAST_LEN, 2 * NA_WIDTH), BF16)],
        input_output_aliases={4: 0},
        compiler_params=_cparams(("arbitrary",)),
        name="latent_neighbourhood_attention",
    )(zna.reshape(n_view, DEC_SEQ, NA_PROJ), ck, cv, tb, att.reshape(n_view, DEC_SEQ, ATT_WIDTH))
    return out.reshape(T_ALL, ATT_WIDTH)


def _lat_gqa_kernel(zgq_ref, zgkv_ref, ck_ref, cv_ref, att_in_ref, att_ref):
    del att_in_ref
    q = (zgq_ref[...] * ATT_SCALE).astype(BF16)
    kv = zgkv_ref[...].astype(BF16)
    kl, vl = kv[:, :GKV_WIDTH], kv[:, GKV_WIDTH:]
    kc = ck_ref[0].astype(BF16)
    vc = cv_ref[0].astype(BF16)
    for h in range(GQA_Q_HEADS):
        g = h // GQA_GROUP
        qh = _head(q, h)
        s_c = _bdot(qh, _head(kc, g), NT_DIMS)
        s_l = _bdot(qh, _head(kl, g), NT_DIMS)
        o = _softmax_pv([s_c, s_l], [_head(vc, g), _head(vl, g)])
        att_ref[:, h * HEAD_DIM:(h + 1) * HEAD_DIM] = o.astype(BF16)


def _lat_gqa_call(zgq, zgkv, ck, cv, att):
    n_q = DEC_SEQ // TQ_GQA
    first_q = T_CTX // TQ_GQA
    lat_blk = T_CTX // DEC_SEQ
    return pl.pallas_call(
        _lat_gqa_kernel,
        out_shape=jax.ShapeDtypeStruct((T_ALL, ATT_WIDTH), BF16),
        grid=(DEC_BATCH, n_q),
        in_specs=[pl.BlockSpec((TQ_GQA, GQ_WIDTH), lambda b, j: (first_q + b * n_q + j, 0)),
                  pl.BlockSpec((DEC_SEQ, 2 * GKV_WIDTH), lambda b, j: (lat_blk + b, 0)),
                  pl.BlockSpec((1, PAST_LEN, GKV_WIDTH), lambda b, j: (b, 0, 0)),
                  pl.BlockSpec((1, PAST_LEN, GKV_WIDTH), lambda b, j: (b, 0, 0)),
                  pl.BlockSpec(memory_space=pl.ANY)],
        out_specs=pl.BlockSpec((TQ_GQA, GQ_WIDTH), lambda b, j: (first_q + b * n_q + j, 0)),
        input_output_aliases={4: 0},
        compiler_params=_cparams(("arbitrary", "arbitrary")),
        name="latent_gqa_attention",
    )(zgq, zgkv, ck, cv, att)


def _out_proj_kernel(x_ref, yf_ref, yb_ref, bonus_ref, gate_ref, att_ref, mod_ref, lng_ref, lnb_ref,
                     wout_ref, g2_ref, rw_ref, rb_ref,
                     x1_ref, h2_ref, slot_ref, gates_ref, cnt_ref):
    ones_bd = _head_ones(RW_WIDTH)
    o = yf_ref[...] + yb_ref[...]
    mu = _head_sum(o, ones_bd) * (1.0 / HEAD_DIM)
    dlt = o - mu
    var = _head_sum(dlt * dlt, ones_bd) * (1.0 / HEAD_DIM)
    ln = dlt * lax.rsqrt(var + LNX_EPS) * lng_ref[...] + lnb_ref[...]
    rw = ((ln + bonus_ref[...]) * gate_ref[...]).astype(BF16)
    att = att_ref[...]
    mix = (jnp.dot(rw, wout_ref[0:RW_WIDTH, :], preferred_element_type=F32)
           + jnp.dot(att[:, :GQ_WIDTH], wout_ref[RW_WIDTH + NA_WIDTH:, :], preferred_element_type=F32)
           + jnp.dot(att[:, GQ_WIDTH:], wout_ref[RW_WIDTH:RW_WIDTH + NA_WIDTH, :],
                     preferred_element_type=F32))
    gate1 = mod_ref[0, 2:3, :]
    shift2 = mod_ref[0, 3:4, :]
    scale2 = mod_ref[0, 4:5, :]
    x1 = x_ref[...] + gate1 * mix
    x1_ref[...] = x1
    h2 = _rms(x1, g2_ref[...]) * (1.0 + scale2) + shift2
    h2_ref[...] = h2.astype(BF16)

    logits = _dot3(h2, rw_ref[...]) + rb_ref[...]
    tm = logits.shape[0]
    col = lax.broadcasted_iota(jnp.int32, (tm, N_EXPERTS), 1)
    lane4 = lax.broadcasted_iota(jnp.int32, (tm, TOP_K), 1)
    work = logits
    sels, vals = [], []
    for k in range(TOP_K):
        m = work.max(axis=-1, keepdims=True)
        idx = jnp.min(jnp.where(work == m, col, N_EXPERTS), axis=-1, keepdims=True)
        sel = col == idx
        sels.append(sel)
        vals.append(m)
        work = jnp.where(sel, -jnp.inf, work)
    es = [jnp.exp(v - vals[0]) for v in vals]
    inv = 1.0 / (es[0] + es[1] + es[2] + es[3])
    gates = jnp.zeros((tm, TOP_K), F32)
    for k in range(TOP_K):
        gates = jnp.where(lane4 == k, es[k] * inv, gates)
    assign = jnp.zeros((tm, N_EXPERTS), F32)
    for sel in sels:
        assign = assign + sel.astype(F32)
    r_i = lax.broadcasted_iota(jnp.int32, (tm, tm), 0)
    c_i = lax.broadcasted_iota(jnp.int32, (tm, tm), 1)
    before = jnp.dot((c_i < r_i).astype(BF16), assign.astype(BF16), preferred_element_type=F32)
    cnt = jnp.sum(assign, axis=0, keepdims=True)
    seg_units = jnp.floor((cnt + (SEG_ALIGN - 1)) * (1.0 / SEG_ALIGN))
    e_r = lax.broadcasted_iota(jnp.int32, (N_EXPERTS, N_EXPERTS), 0)
    e_c = lax.broadcasted_iota(jnp.int32, (N_EXPERTS, N_EXPERTS), 1)
    loc = SEG_ALIGN * jnp.dot(seg_units.astype(BF16), (e_r < e_c).astype(BF16),
                              preferred_element_type=F32)
    pos = before + loc
    slot = jnp.zeros((tm, TOP_K), F32)
    for k in range(TOP_K):
        sk = jnp.sum(jnp.where(sels[k], pos, 0.0), axis=-1, keepdims=True)
        slot = jnp.where(lane4 == k, sk, slot)
    slot_ref[...] = slot.astype(jnp.int32)
    gates_ref[...] = gates
    cnt_ref[0] = cnt.astype(jnp.int32)


def _out_proj_call(x, yf, yb, bonus, gate, att, mods_l, lp):
    n_blk = T_ALL // TM_TOK
    row = lambda i: (i, 0)
    full2 = lambda r, c: pl.BlockSpec((r, c), lambda i: (0, 0))
    tokw = lambda w: pl.BlockSpec((TM_TOK, w), row)
    return pl.pallas_call(
        _out_proj_kernel,
        out_shape=(jax.ShapeDtypeStruct((T_ALL, D_MODEL), F32),
                   jax.ShapeDtypeStruct((T_ALL, D_MODEL), BF16),
                   jax.ShapeDtypeStruct((T_ALL, TOP_K), jnp.int32),
                   jax.ShapeDtypeStruct((T_ALL, TOP_K), F32),
                   jax.ShapeDtypeStruct((n_blk, 1, N_EXPERTS), jnp.int32)),
        grid=(n_blk,),
        in_specs=[tokw(D_MODEL), tokw(RW_WIDTH), tokw(RW_WIDTH), tokw(RW_WIDTH), tokw(RW_WIDTH),
                  tokw(ATT_WIDTH),
                  pl.BlockSpec((1, N_MOD, D_MODEL), lambda i: (_group_of_block(i, TM_TOK), 0, 0)),
                  full2(1, RW_WIDTH), full2(1, RW_WIDTH), full2(D_MODEL, D_MODEL), full2(1, D_MODEL),
                  full2(D_MODEL, N_EXPERTS), full2(1, N_EXPERTS)],
        out_specs=(tokw(D_MODEL), tokw(D_MODEL), tokw(TOP_K), tokw(TOP_K),
                   pl.BlockSpec((1, 1, N_EXPERTS), lambda i: (i, 0, 0))),
        compiler_params=_cparams(("arbitrary",)),
        name="out_proj_router",
    )(x, yf, yb, bonus, gate, att, mods_l, lp["rw_ln_g"], lp["rw_ln_b"], lp["w_out"], lp["norm2_g"],
      lp["router_w"], lp["router_b"])


def _for_each_piece(n_units, max_bit, fn):
    for b in range(max_bit, -1, -1):
        @pl.when(((n_units >> b) & 1) == 1)
        def _(b=b):
            off = ((n_units >> (b + 1)) << (b + 1)) * SEG_ALIGN
            fn(pl.multiple_of(off, SEG_ALIGN), SEG_ALIGN << b)


def _segment_copies(local_ref, sorted_hbm, sem, blk, seg_ref, loc_ref, dst_ref, total_ref, to_sorted, wait):
    def copy(loc, dst, size):
        a = local_ref.at[pl.ds(pl.multiple_of(loc, SEG_ALIGN), size), :]
        b = sorted_hbm.at[pl.ds(pl.multiple_of(dst, SEG_ALIGN), size), :]
        return pltpu.make_async_copy(a, b, sem) if to_sorted else pltpu.make_async_copy(b, a, sem)

    if wait:
        _for_each_piece(total_ref[blk], TOTAL_MAX_BIT, lambda off, size: copy(0, 0, size).wait())
        return

    def body(e, carry):
        t = blk * N_EXPERTS + e
        loc = loc_ref[t]
        dst = dst_ref[t]
        _for_each_piece(seg_ref[t], SEG_MAX_BIT, lambda off, size: copy(loc + off, dst + off, size).start())
        return carry
    lax.fori_loop(0, N_EXPERTS, body, 0)


def _dispatch_kernel(seg_ref, loc_ref, dst_ref, total_ref, tail_ref, tail_dst_ref, h2_ref, slot_ref, xs_hbm,
                     xs_local, zero_buf, sem, zsem):
    j = pl.program_id(0)
    n = pl.num_programs(0)
    buf = j % 2

    @pl.when(j == 0)
    def _():
        zero_buf[...] = jnp.zeros(zero_buf.shape, BF16)

        def tails(wait):
            def body(e, carry):
                def piece(off, size):
                    cp = pltpu.make_async_copy(
                        zero_buf.at[pl.ds(0, size), :],
                        xs_hbm.at[pl.ds(pl.multiple_of(tail_dst_ref[e] + off, SEG_ALIGN), size), :], zsem)
                    if wait:
                        cp.wait()
                    else:
                        cp.start()
                _for_each_piece(tail_ref[e], TAIL_MAX_BIT, piece)
                return carry
            lax.fori_loop(0, N_EXPERTS, body, 0)
        tails(False)
        tails(True)

    lane = lax.broadcasted_iota(jnp.int32, (TM_TOK, LANES), 1)
    slots = slot_ref[...].astype(F32)
    wide = jnp.zeros((TM_TOK, LANES), F32)
    for k in range(TOP_K):
        wide = jnp.where(lane == k, slots[:, k:k + 1], wide)
    slot_rows = wide.T
    h2 = h2_ref[...]
    for c in range(LOCAL_ROWS // TM_TOK):
        row = lax.broadcasted_iota(jnp.int32, (TM_TOK, TM_TOK), 0).astype(F32) + float(c * TM_TOK)
        onehot = jnp.zeros((TM_TOK, TM_TOK), F32)
        for k in range(TOP_K):
            onehot = jnp.where(row == slot_rows[k:k + 1, :], 1.0, onehot)
        onehot = onehot.astype(BF16)
        xs_local[buf, c * TM_TOK:(c + 1) * TM_TOK, :] = jnp.dot(
            onehot, h2, preferred_element_type=F32).astype(BF16)

    tabs = (seg_ref, loc_ref, dst_ref, total_ref)

    @pl.when(j > 0)
    def _():
        _segment_copies(xs_local.at[1 - buf], xs_hbm, sem.at[1 - buf], j - 1, *tabs, True, True)

    _segment_copies(xs_local.at[buf], xs_hbm, sem.at[buf], j, *tabs, True, False)

    @pl.when(j == n - 1)
    def _():
        _segment_copies(xs_local.at[buf], xs_hbm, sem.at[buf], j, *tabs, True, True)


def _dispatch_call(tabs, h2, slot):
    n_blk = T_ALL // TM_TOK
    row = lambda j, *_: (j, 0)
    grid_spec = pltpu.PrefetchScalarGridSpec(
        num_scalar_prefetch=6,
        grid=(n_blk,),
        in_specs=[pl.BlockSpec((TM_TOK, D_MODEL), row), pl.BlockSpec((TM_TOK, TOP_K), row)],
        out_specs=pl.BlockSpec(memory_space=pl.ANY),
        scratch_shapes=[pltpu.VMEM((2, LOCAL_ROWS, D_MODEL), BF16), pltpu.VMEM((TM_EXP, D_MODEL), BF16),
                        pltpu.SemaphoreType.DMA((2,)), pltpu.SemaphoreType.DMA],
    )
    return pl.pallas_call(
        _dispatch_kernel,
        out_shape=jax.ShapeDtypeStruct((N_SLOTS, D_MODEL), BF16),
        grid_spec=grid_spec,
        compiler_params=_cparams(("arbitrary",)),
        name="moe_dispatch",
    )(tabs["seg"], tabs["loc"], tabs["dst"], tabs["total"], tabs["tail"], tabs["tail_dst"], h2, slot)


def _expert_kernel(be_ref, first_ref, par_ref, nexte_ref, meta_ref, x_ref, wgu_hbm, bgu_ref, wd_hbm, bd_ref,
                   y_ref, wgu_f, wd_f, wgu_bf, wd_bf, sem, *, layer):
    i = pl.program_id(0)
    n_used = meta_ref[0]

    def weight_copies(e, buf):
        rows = D_MODEL // WEIGHT_DMA_PARTS
        parts = []
        for p in range(WEIGHT_DMA_PARTS):
            sl = pl.ds(p * rows, rows)
            parts.append(pltpu.make_async_copy(wgu_hbm.at[layer, e, sl], wgu_f.at[buf, sl], sem.at[buf]))
            parts.append(pltpu.make_async_copy(wd_hbm.at[layer, e, sl], wd_f.at[buf, sl], sem.at[buf]))
        return parts

    @pl.when(i < n_used)
    def _():
        e = be_ref[i]
        buf = par_ref[i]
        is_first = first_ref[i] == 1

        @pl.when(is_first)
        def _():
            @pl.when(i == 0)
            def _():
                for cp in weight_copies(e, buf):
                    cp.start()
            for cp in weight_copies(e, buf):
                cp.wait()
            wgu_bf[...] = wgu_f[buf].astype(BF16)
            wd_bf[...] = wd_f[buf].astype(BF16)

        x = x_ref[...]
        y = bd_ref[0, 0]
        for c in range(D_FF // FF_SLAB):
            g_cols = slice(c * FF_SLAB, (c + 1) * FF_SLAB)
            l_cols = slice(D_FF + c * FF_SLAB, D_FF + (c + 1) * FF_SLAB)
            g = jnp.dot(x, wgu_bf[:, g_cols], preferred_element_type=F32) + bgu_ref[0, 0, :, g_cols]
            li = jnp.dot(x, wgu_bf[:, l_cols], preferred_element_type=F32) + bgu_ref[0, 0, :, l_cols]
            glu = jnp.minimum(g, SWIGLU_LIMIT)
            lin = jnp.clip(li, -SWIGLU_LIMIT, SWIGLU_LIMIT)
            act = glu * _sigmoid(SWIGLU_ALPHA * glu) * (lin + 1.0)
            y = y + jnp.dot(act.astype(BF16), wd_bf[g_cols, :], preferred_element_type=F32)
        y_ref[...] = y.astype(BF16)

        nxt = nexte_ref[i]

        @pl.when(is_first & (nxt >= 0))
        def _():
            for cp in weight_copies(nxt, 1 - buf):
                cp.start(priority=1)


def _expert_call(tabs, xs, w_gu, b_gu, w_down, b_down, layer):
    def bmap(i, be, first, par, nxt, meta_):
        return (layer, be[i], 0, 0)

    def rmap(i, be, first, par, nxt, meta_):
        return (jnp.minimum(i, meta_[0] - 1), 0)

    grid_spec = pltpu.PrefetchScalarGridSpec(
        num_scalar_prefetch=5,
        grid=(N_SLOT_BLOCKS,),
        in_specs=[pl.BlockSpec((TM_EXP, D_MODEL), rmap),
                  pl.BlockSpec(memory_space=pl.ANY),
                  pl.BlockSpec((1, 1, 1, 2 * D_FF), bmap),
                  pl.BlockSpec(memory_space=pl.ANY),
                  pl.BlockSpec((1, 1, 1, D_MODEL), bmap)],
        out_specs=pl.BlockSpec((TM_EXP, D_MODEL), rmap),
        scratch_shapes=[pltpu.VMEM((2, D_MODEL, 2 * D_FF), F32), pltpu.VMEM((2, D_FF, D_MODEL), F32),
                        pltpu.VMEM((D_MODEL, 2 * D_FF), BF16), pltpu.VMEM((D_FF, D_MODEL), BF16),
                        pltpu.SemaphoreType.DMA((2,))],
    )
    return pl.pallas_call(
        functools.partial(_expert_kernel, layer=layer),
        out_shape=jax.ShapeDtypeStruct((N_SLOTS, D_MODEL), BF16),
        grid_spec=grid_spec,
        compiler_params=_cparams(("arbitrary",)),
        name="moe_experts",
    )(tabs["block_e"], tabs["first"], tabs["parity"], tabs["next_e"], tabs["n_used"], xs, w_gu,
      b_gu.reshape(DEPTH, N_EXPERTS, 1, 2 * D_FF), w_down, b_down.reshape(DEPTH, N_EXPERTS, 1, D_MODEL))


def _combine_kernel(seg_ref, loc_ref, dst_ref, total_ref, ys_hbm, x1_ref, slot_ref, gates_ref, mod_ref,
                    fg_ref, *rest, final):
    *o_refs, ybuf, sem = rest
    j = pl.program_id(0)
    n = pl.num_programs(0)
    buf = j % 2

    tabs = (seg_ref, loc_ref, dst_ref, total_ref)

    @pl.when(j == 0)
    def _():
        _segment_copies(ybuf.at[0], ys_hbm, sem.at[0], 0, *tabs, False, False)

    @pl.when(j + 1 < n)
    def _():
        _segment_copies(ybuf.at[1 - buf], ys_hbm, sem.at[1 - buf], j + 1, *tabs, False, False)

    _segment_copies(ybuf.at[buf], ys_hbm, sem.at[buf], j, *tabs, False, True)

    slots = slot_ref[...]
    gates = gates_ref[...]
    total = total_ref[j] * SEG_ALIGN
    ff = jnp.zeros((TM_TOK, D_MODEL), F32)
    for c in range(LOCAL_ROWS // TM_TOK):
        col = lax.broadcasted_iota(jnp.int32, (TM_TOK, TM_TOK), 1) + c * TM_TOK
        q = jnp.zeros((TM_TOK, TM_TOK), F32)
        for k in range(TOP_K):
            q = jnp.where(col == slots[:, k:k + 1], gates[:, k:k + 1], q)
        rows = lax.broadcasted_iota(jnp.int32, (TM_TOK, D_MODEL), 0) + c * TM_TOK
        y = jnp.where(rows < total, ybuf[buf, c * TM_TOK:(c + 1) * TM_TOK, :], jnp.zeros((), BF16))
        ff = ff + jnp.dot(q.astype(BF16), y, preferred_element_type=F32)
    x = x1_ref[...] + mod_ref[0, 5:6, :] * ff
    if final:
        x = _rms(x, fg_ref[...])
        ctx_ref, lat_ref = o_refs
        first_lat = T_CTX // TM_TOK

        @pl.when(j < first_lat)
        def _():
            ctx_ref[...] = x

        @pl.when(j >= first_lat)
        def _():
            lat_ref[...] = x
    else:
        o_refs[0][...] = x


def _combine_call(tabs, ys, x1, slot, gates, mods_l, final_g, final):
    n_blk = T_ALL // TM_TOK
    row = lambda j, *_: (j, 0)
    first_lat = T_CTX // TM_TOK
    if final:
        out_shape = (jax.ShapeDtypeStruct((T_CTX, D_MODEL), F32), jax.ShapeDtypeStruct((T_LAT, D_MODEL), F32))
        out_specs = (pl.BlockSpec((TM_TOK, D_MODEL), lambda j, *_: (jnp.minimum(j, first_lat - 1), 0)),
                     pl.BlockSpec((TM_TOK, D_MODEL), lambda j, *_: (jnp.maximum(j - first_lat, 0), 0)))
    else:
        out_shape = jax.ShapeDtypeStruct((T_ALL, D_MODEL), F32)
        out_specs = pl.BlockSpec((TM_TOK, D_MODEL), row)
    grid_spec = pltpu.PrefetchScalarGridSpec(
        num_scalar_prefetch=4,
        grid=(n_blk,),
        in_specs=[pl.BlockSpec(memory_space=pl.ANY),
                  pl.BlockSpec((TM_TOK, D_MODEL), row),
                  pl.BlockSpec((TM_TOK, TOP_K), row),
                  pl.BlockSpec((TM_TOK, TOP_K), row),
                  pl.BlockSpec((1, N_MOD, D_MODEL), lambda j, *_: (_group_of_block(j, TM_TOK), 0, 0)),
                  pl.BlockSpec((1, D_MODEL), lambda j, *_: (0, 0))],
        out_specs=out_specs,
        scratch_shapes=[pltpu.VMEM((2, LOCAL_ROWS, D_MODEL), BF16), pltpu.SemaphoreType.DMA((2,))],
    )
    return pl.pallas_call(
        functools.partial(_combine_kernel, final=final),
        out_shape=out_shape,
        grid_spec=grid_spec,
        compiler_params=_cparams(("arbitrary",)),
        name="moe_combine_final" if final else "moe_combine",
    )(tabs["seg"], tabs["loc"], tabs["dst"], tabs["total"], ys, x1, slot, gates, mods_l, final_g)


def _routing_tables(cnt):
    i32 = jnp.int32
    cnt = cnt.reshape(T_ALL // TM_TOK, N_EXPERTS)
    seg = (cnt + SEG_ALIGN - 1) // SEG_ALIGN * SEG_ALIGN
    loc = jnp.cumsum(seg, axis=1) - seg
    total = jnp.sum(seg, axis=1)
    rows_e = jnp.sum(seg, axis=0)
    region = (rows_e + TM_EXP - 1) // TM_EXP * TM_EXP
    region_end = jnp.cumsum(region)
    base = region_end - region
    dst = base[None, :] + jnp.cumsum(seg, axis=0) - seg
    n_used = jnp.maximum(region_end[-1] // TM_EXP, 1).astype(i32)
    blk = jnp.minimum(jnp.arange(N_SLOT_BLOCKS, dtype=i32), n_used - 1)
    block_e = jnp.sum((region_end[None, :] <= (blk * TM_EXP)[:, None]).astype(i32), axis=1)
    block_e = jnp.minimum(block_e, N_EXPERTS - 1).astype(i32)
    first = jnp.concatenate([jnp.ones((1,), i32), (block_e[1:] != block_e[:-1]).astype(i32)])
    parity = (jnp.cumsum(first) - 1) % 2
    e_ids = jnp.arange(N_EXPERTS, dtype=i32)
    later_used = (e_ids[None, :] > e_ids[:, None]) & (region[None, :] > 0)
    next_used = jnp.min(jnp.where(later_used, e_ids[None, :], N_EXPERTS), axis=1)
    next_e = next_used[block_e]
    next_e = jnp.where(next_e < N_EXPERTS, next_e, -1)
    flat = lambda a: a.reshape(-1).astype(i32)
    return dict(seg=flat(seg // SEG_ALIGN), loc=flat(loc), dst=flat(dst), total=flat(total // SEG_ALIGN),
                tail=flat((region - rows_e) // SEG_ALIGN), tail_dst=flat(base + rows_e),
                block_e=block_e, first=first, parity=flat(parity), next_e=flat(next_e),
                n_used=n_used.reshape(1))


def kernel(x_prompt, x_sample, cache_na_k, cache_na_v, cache_gqa_k, cache_gqa_v, state_rwkv, c, c_ctx,
           w_mod, b_mod, norm1_g, norm2_g, w_in, rw_shift, rw_w0, rw_w_up, rw_a0, rw_a_up, rw_g_up,
           rw_k_k, rw_k_a, rw_r_k, rw_ln_g, rw_ln_b, na_rpb, q_norm, k_norm, w_out, router_w, router_b,
           moe_w_gu, moe_b_gu, moe_w_down, moe_b_down, final_norm_g):
    x = jnp.concatenate([x_prompt.reshape(T_CTX, D_MODEL), x_sample.reshape(T_LAT, D_MODEL)], axis=0)
    cvecs = jnp.concatenate([c_ctx[None, :], c, jnp.zeros((SUBLANES - N_GROUPS, D_MODEL), F32)], axis=0)
    mods = _mods_call(cvecs, w_mod, b_mod)
    mods = mods[:, :N_GROUPS].reshape(DEPTH, N_GROUPS, N_MOD, D_MODEL)
    rope = _rope_tables()
    w_in_bf = w_in.astype(BF16)
    w_out_bf = w_out.astype(BF16)
    rw_g_up_bf = rw_g_up.astype(BF16)
    final_g = final_norm_g.reshape(1, D_MODEL)

    na_tb = _na_bias_tables(na_rpb)
    caches = None
    states = None
    for l in range(DEPTH):
        lp = {
            "rw_shift": rw_shift[l], "rw_k_k": rw_k_k[l].reshape(1, RW_WIDTH),
            "rw_k_a": rw_k_a[l].reshape(1, RW_WIDTH), "rw_r_k": rw_r_k[l].reshape(1, RW_WIDTH),
            "rw_w0": rw_w0[l], "rw_w_up": rw_w_up[l], "rw_a0": rw_a0[l], "rw_a_up": rw_a_up[l],
            "rw_g_up": rw_g_up_bf[l], "rw_ln_g": rw_ln_g[l].reshape(1, RW_WIDTH),
            "rw_ln_b": rw_ln_b[l].reshape(1, RW_WIDTH), "w_out": w_out_bf[l],
            "norm2_g": norm2_g[l].reshape(1, D_MODEL), "router_w": router_w[l],
            "router_b": router_b[l].reshape(1, N_EXPERTS),
        }
        qk_g = jnp.concatenate([jnp.tile(q_norm[l], GQA_Q_HEADS), jnp.tile(k_norm[l], GQA_KV_HEADS)])
        zrw, zna, zgq, zgkv = _in_proj_call(x, mods[l], norm1_g[l].reshape(1, D_MODEL), w_in_bf[l],
                                            qk_g.reshape(1, GQ_WIDTH + GKV_WIDTH), rope)

        s0_lat = state_rwkv[:, l]
        rw_ctx = _rwkv_call(zrw, s0_lat, lp, (), n_seq=BATCH, seq_len=SEQ, row_base=0,
                            has_init=False, emit_state=True, state_prev=states, layer=l)
        yf, yb, bonus, gate = _rwkv_call(zrw, s0_lat, lp, rw_ctx[:4], n_seq=DEC_BATCH, seq_len=DEC_SEQ,
                                         row_base=T_CTX, has_init=True, emit_state=False)
        states = rw_ctx[4]

        att, *caches = _ctx_attn_call(zna, zgq, zgkv, caches, l)
        att = _lat_na_call(zna, cache_na_k[:, l].reshape(DEC_BATCH, PAST_LEN, NA_WIDTH),
                           cache_na_v[:, l].reshape(DEC_BATCH, PAST_LEN, NA_WIDTH), na_tb[l], att)
        att = _lat_gqa_call(zgq, zgkv, cache_gqa_k[:, l].reshape(DEC_BATCH, PAST_LEN, GKV_WIDTH),
                            cache_gqa_v[:, l].reshape(DEC_BATCH, PAST_LEN, GKV_WIDTH), att)

        x1, h2, slot, gates, counts = _out_proj_call(x, yf, yb, bonus, gate, att, mods[l], lp)
        tabs = _routing_tables(counts)
        xs = _dispatch_call(tabs, h2, slot)
        ys = _expert_call(tabs, xs, moe_w_gu, moe_b_gu, moe_w_down, moe_b_down, l)
        x = _combine_call(tabs, ys, x1, slot, gates, mods[l], final_g, l == DEPTH - 1)

    y_prompt = x[0].reshape(BATCH, SEQ, D_MODEL)
    y_sample = x[1].reshape(DEC_BATCH, DEC_SEQ, D_MODEL)
    heads = lambda t, n: t.reshape(BATCH, DEPTH, SEQ, n, HEAD_DIM)
    return (y_prompt, y_sample, heads(caches[0], NA_HEADS), heads(caches[1], NA_HEADS),
            heads(caches[2], GQA_KV_HEADS), heads(caches[3], GQA_KV_HEADS), states)
```

```python
import functools

import numpy as np
import jax
import jax.numpy as jnp
from jax import lax
from jax.experimental import pallas as pl
from jax.experimental.pallas import tpu as pltpu

F32 = jnp.float32
BF16 = jnp.bfloat16

D_MODEL = 1024
BATCH = 32
SEQ = 256
DEPTH = 4
DEC_BATCH = 2
DEC_SEQ = 2048
PAST_LEN = 512
GRID_W = 64
GRID_H = DEC_SEQ // GRID_W
HEAD_DIM = 64
RW_HEADS = 4
RW_WIDTH = RW_HEADS * HEAD_DIM
DECAY_LORA = 64
ICLR_LORA = 64
GATE_LORA = 128
NA_HEADS = 4
NA_WIDTH = NA_HEADS * HEAD_DIM
NA_WIN_ROWS = 8
NA_WIN_COLS = 16
GQA_Q_HEADS = 8
GQA_KV_HEADS = 2
GQA_GROUP = GQA_Q_HEADS // GQA_KV_HEADS
GQ_WIDTH = GQA_Q_HEADS * HEAD_DIM
GKV_WIDTH = GQA_KV_HEADS * HEAD_DIM
RW_PROJ = 3 * RW_WIDTH + DECAY_LORA + ICLR_LORA + GATE_LORA
NA_PROJ = 3 * NA_WIDTH
GQA_PROJ = GQ_WIDTH + 2 * GKV_WIDTH
IN_PROJ = RW_PROJ + NA_PROJ + GQA_PROJ
ATT_WIDTH = GQ_WIDTH + NA_WIDTH
N_EXPERTS = 32
TOP_K = 4
D_FF = D_MODEL
SWIGLU_LIMIT = 7.0
SWIGLU_ALPHA = 1.702
ROPE_THETA = 10000.0
NORM_EPS = 1e-6
LNX_EPS = 64e-5
N_MOD = 6
ATT_SCALE = HEAD_DIM ** -0.5
NEG_BIG = -1e30

T_CTX = BATCH * SEQ
T_LAT = DEC_BATCH * DEC_SEQ
T_ALL = T_CTX + T_LAT
N_GROUPS = 1 + DEC_BATCH

LANES = 128
SUBLANES = 8
TM_TOK = 512
CHUNK = 64
RW_SEQ_GROUP = 4
TQ_GQA = 512
CTX_SEQ_GROUP = 1
TM_EXP = 256
WEIGHT_DMA_PARTS = 4
FF_SLAB = 1024
SEG_ALIGN = 16
SEG_MAX_BIT = (TM_TOK // SEG_ALIGN).bit_length() - 1
TAIL_MAX_BIT = (TM_EXP // SEG_ALIGN - 1).bit_length() - 1
TOTAL_MAX_BIT = (-(-(TOP_K * TM_TOK + N_EXPERTS * (SEG_ALIGN - 1)) // SEG_ALIGN)).bit_length() - 1
LOCAL_ROWS = -(-(TOP_K * TM_TOK + N_EXPERTS * (SEG_ALIGN - 1)) // TM_TOK) * TM_TOK
N_SLOT_BLOCKS = (-(-(T_ALL * TOP_K + (T_ALL // TM_TOK) * N_EXPERTS * (SEG_ALIGN - 1)) // TM_EXP)
                 + N_EXPERTS)
N_SLOTS = N_SLOT_BLOCKS * TM_EXP
VMEM_LIMIT = 56 * 1024 * 1024

NT_DIMS = (((1,), (1,)), ((), ()))
TN_DIMS = (((0,), (0,)), ((), ()))


def _bdot(a, b, dims=None):
    a = a.astype(BF16)
    b = b.astype(BF16)
    if dims is None:
        return jnp.dot(a, b, preferred_element_type=F32)
    return lax.dot_general(a, b, dims, preferred_element_type=F32)


def _split(a):
    hi = a.astype(BF16)
    lo = (a - hi.astype(F32)).astype(BF16)
    return hi, lo


def _dot3(a, b, dims=None):
    ah, al = _split(a)
    bh, bl = _split(b)
    return _bdot(ah, bh, dims) + _bdot(ah, bl, dims) + _bdot(al, bh, dims)


def _dot_exact_lhs(a_exact, b):
    h1 = b.astype(BF16)
    r1 = b - h1.astype(F32)
    h2 = r1.astype(BF16)
    h3 = (r1 - h2.astype(F32)).astype(BF16)
    return _bdot(a_exact, h1) + _bdot(a_exact, h2) + _bdot(a_exact, h3)


def _head_ones(n):
    r = lax.broadcasted_iota(jnp.int32, (n, n), 0) // HEAD_DIM
    c = lax.broadcasted_iota(jnp.int32, (n, n), 1) // HEAD_DIM
    return (r == c).astype(BF16)


def _head_sum(x, ones_bd):
    hi, lo = _split(x)
    return (jnp.dot(hi, ones_bd, preferred_element_type=F32)
            + jnp.dot(lo, ones_bd, preferred_element_type=F32))


def _sigmoid(x):
    return 1.0 / (1.0 + jnp.exp(-x))


def _cparams(sem):
    return pltpu.CompilerParams(dimension_semantics=sem, vmem_limit_bytes=VMEM_LIMIT)


def _group_of_block(i, rows_per_block):
    first_lat = T_CTX // rows_per_block
    per_sample = DEC_SEQ // rows_per_block
    return jnp.where(i < first_lat, 0, 1 + (i - first_lat) // per_sample)


def _mods_kernel(c_ref, w_ref, b_ref, o_ref):
    c = c_ref[...]
    s = c * _sigmoid(c)
    o_ref[0] = _dot3(s, w_ref[0]) + b_ref[0]


def _mods_call(cvecs, w_mod, b_mod):
    tn = 1536
    n_rows = cvecs.shape[0]
    return pl.pallas_call(
        _mods_kernel,
        out_shape=jax.ShapeDtypeStruct((DEPTH, n_rows, N_MOD * D_MODEL), F32),
        grid=(DEPTH, N_MOD * D_MODEL // tn),
        in_specs=[
            pl.BlockSpec((n_rows, D_MODEL), lambda l, j: (0, 0)),
            pl.BlockSpec((1, D_MODEL, tn), lambda l, j: (l, 0, j)),
            pl.BlockSpec((1, 1, tn), lambda l, j: (l, 0, j)),
        ],
        out_specs=pl.BlockSpec((1, n_rows, tn), lambda l, j: (l, 0, j)),
        compiler_params=_cparams(("arbitrary", "arbitrary")),
        name="adaln_mods",
    )(cvecs, w_mod, b_mod.reshape(DEPTH, 1, N_MOD * D_MODEL))


def _rms(x, g):
    ms = jnp.mean(x * x, axis=-1, keepdims=True)
    return x * lax.rsqrt(ms + NORM_EPS) * g


def _in_proj_kernel(x_ref, mod_ref, g_ref, w_ref, qkg_ref, rc_ref, rs1_ref, rs2_ref,
                    zrw_ref, zna_ref, zgq_ref, zgkv_ref):
    x = x_ref[...]
    shift1 = mod_ref[0, 0:1, :]
    scale1 = mod_ref[0, 1:2, :]
    h = _rms(x, g_ref[...]) * (1.0 + scale1) + shift1
    z = jnp.dot(h.astype(BF16), w_ref[...], preferred_element_type=F32)
    zrw_ref[...] = z[:, :RW_PROJ]
    zna_ref[...] = z[:, RW_PROJ:RW_PROJ + NA_PROJ]
    qk_w = GQ_WIDTH + GKV_WIDTH
    qk = z[:, RW_PROJ + NA_PROJ:RW_PROJ + NA_PROJ + qk_w]
    ones_bd = _head_ones(LANES)
    sq = qk * qk
    ssq = jnp.concatenate(
        [_head_sum(sq[:, j * LANES:(j + 1) * LANES], ones_bd) for j in range(qk_w // LANES)], axis=1)
    qkn = qk * lax.rsqrt(ssq * (1.0 / HEAD_DIM) + NORM_EPS) * qkg_ref[...]
    reps = qk_w // LANES
    rc = jnp.concatenate([rc_ref[...]] * reps, axis=1)
    rs1 = jnp.concatenate([rs1_ref[...]] * reps, axis=1)
    rs2 = jnp.concatenate([rs2_ref[...]] * reps, axis=1)
    half = HEAD_DIM // 4
    qkr = qkn * rc + pltpu.roll(qkn, half, 1) * rs1 + pltpu.roll(qkn, qk_w - half, 1) * rs2
    zgq_ref[...] = qkr[:, :GQ_WIDTH]
    zgkv_ref[:, :GKV_WIDTH] = qkr[:, GQ_WIDTH:]
    zgkv_ref[:, GKV_WIDTH:] = z[:, RW_PROJ + NA_PROJ + qk_w:]


def _rope_tables():
    t = np.arange(DEC_SEQ)
    pos = np.stack([t // GRID_W, t % GRID_W], axis=1).astype(np.float32)
    axis_dim = HEAD_DIM // 2
    inv = ROPE_THETA ** (-np.arange(0, axis_dim, 2, dtype=np.float32) / axis_dim)
    d = np.arange(LANES) % HEAD_DIM
    part = d // axis_dim
    within = d % axis_dim
    freq = within % (axis_dim // 2)
    second = within // (axis_dim // 2)
    ang = jnp.asarray(pos)[:, part] * jnp.asarray(inv)[freq][None, :]
    cos = jnp.cos(ang)
    sin = jnp.sin(ang)
    s1 = jnp.where(second[None, :] == 1, sin, 0.0)
    s2 = jnp.where(second[None, :] == 0, -sin, 0.0)
    ident = jnp.ones((TM_TOK, LANES), F32)
    zero = jnp.zeros((TM_TOK, LANES), F32)
    return (jnp.concatenate([cos, ident], 0), jnp.concatenate([s1, zero], 0),
            jnp.concatenate([s2, zero], 0))


def _in_proj_call(x, mods_l, g1, w_in_bf, qk_g, rope):
    n_blk = T_ALL // TM_TOK
    lat_blk = DEC_SEQ // TM_TOK
    first_lat = T_CTX // TM_TOK

    def rope_idx(i):
        return (jnp.where(i < first_lat, lat_blk, (i - first_lat) % lat_blk), 0)

    row = lambda i: (i, 0)
    rope_spec = pl.BlockSpec((TM_TOK, LANES), rope_idx)
    return pl.pallas_call(
        _in_proj_kernel,
        out_shape=(jax.ShapeDtypeStruct((T_ALL, RW_PROJ), F32),
                   jax.ShapeDtypeStruct((T_ALL, NA_PROJ), F32),
                   jax.ShapeDtypeStruct((T_ALL, GQ_WIDTH), F32),
                   jax.ShapeDtypeStruct((T_ALL, 2 * GKV_WIDTH), F32)),
        grid=(n_blk,),
        in_specs=[
            pl.BlockSpec((TM_TOK, D_MODEL), row),
            pl.BlockSpec((1, N_MOD, D_MODEL), lambda i: (_group_of_block(i, TM_TOK), 0, 0)),
            pl.BlockSpec((1, D_MODEL), lambda i: (0, 0)),
            pl.BlockSpec((D_MODEL, IN_PROJ), lambda i: (0, 0)),
            pl.BlockSpec((1, GQ_WIDTH + GKV_WIDTH), lambda i: (0, 0)),
            rope_spec, rope_spec, rope_spec,
        ],
        out_specs=(pl.BlockSpec((TM_TOK, RW_PROJ), row), pl.BlockSpec((TM_TOK, NA_PROJ), row),
                   pl.BlockSpec((TM_TOK, GQ_WIDTH), row), pl.BlockSpec((TM_TOK, 2 * GKV_WIDTH), row)),
        compiler_params=_cparams(("arbitrary",)),
        name="in_proj",
    )(x, mods_l, g1, w_in_bf, qk_g, *rope)


def _softplus(x):
    return jnp.maximum(x, 0.0) + jnp.log(1.0 + jnp.exp(-jnp.abs(x)))


def _rw_pre(z, zprev, znext, shift_ref, kk_ref, ka_ref, rk_ref, w0_ref, wup_ref, a0_ref, aup_ref, d,
            ones_bd):
    rows = lax.broadcasted_iota(jnp.int32, z.shape, 0)
    zp = jnp.where(rows == 0, zprev, pltpu.roll(z, 1, 0))
    zn = jnp.where(rows == CHUNK - 1, znext, pltpu.roll(z, CHUNK - 1, 0))
    zs = zp * shift_ref[0:1, :] + z * shift_ref[1:2, :] + zn * shift_ref[2:3, :]
    r = zs[:, 0:RW_WIDTH]
    k = zs[:, RW_WIDTH:2 * RW_WIDTH]
    v = zs[:, 2 * RW_WIDTH:3 * RW_WIDTH]
    o = 3 * RW_WIDTH
    wd = zs[:, o:o + DECAY_LORA]
    ad = zs[:, o + DECAY_LORA:o + DECAY_LORA + ICLR_LORA]
    gd = zs[:, o + DECAY_LORA + ICLR_LORA:]
    kk = k * kk_ref[...]
    kk = kk / jnp.maximum(jnp.sqrt(_head_sum(kk * kk, ones_bd)), 1e-12)
    tw = jnp.tanh(wd)
    wl = w0_ref[d:d + 1, :] + _dot3(tw, wup_ref[d])
    lw = -jnp.exp(-_softplus(-wl) - 0.5)
    a_sig = _sigmoid(a0_ref[d:d + 1, :] + _dot3(ad, aup_ref[d]))
    k_d = k * (1.0 + (a_sig - 1.0) * ka_ref[...])
    bonus = _head_sum(r * k_d * rk_ref[...], ones_bd) * v
    return dict(r=r, k=k_d, v=v, a=-kk, b=kk * a_sig, lw=lw, bonus=bonus, gd=gd, ad=ad, k_raw=k)


def _chunk_masks(rev):
    t = lax.broadcasted_iota(jnp.int32, (CHUNK, CHUNK), 0)
    j = lax.broadcasted_iota(jnp.int32, (CHUNK, CHUNK), 1)
    return ((j >= t), (j > t)) if rev else ((j <= t), (j < t))


def _wkv_scale(p, incl, rev):
    lw = p["lw"]
    cs = _dot_exact_lhs(incl.astype(BF16), lw)
    tot = cs[0:1, :] if rev else cs[CHUNK - 1:CHUNK, :]
    e_inv = jnp.exp(-cs)
    e_rem = jnp.exp(tot - cs)
    bf = lambda x: x.astype(BF16)
    return dict(at=bf(p["a"] * jnp.exp(cs - lw)), rt=bf(p["r"] * jnp.exp(cs)),
                bt=bf(p["b"] * e_inv), kt=bf(p["k"] * e_inv),
                bh=bf(p["b"] * e_rem), kh=bf(p["k"] * e_rem), v=bf(p["v"]), gtot=jnp.exp(tot))


def _wkv_chunks(items):
    bf = lambda x: x.astype(BF16)
    c = CHUNK
    n_sq = int(np.log2(c))
    ar = [jnp.concatenate([it["at"], it["rt"]], 0) for it in items]
    m_b = [_bdot(a, it["bt"], NT_DIMS) for a, it in zip(ar, items)]
    m_k = [_bdot(a, it["kt"], NT_DIMS) for a, it in zip(ar, items)]
    s_bf = [bf(it["s"]) for it in items]
    xs = [bf(jnp.where(it["strict"], m[:c], 0.0)) for m, it in zip(m_b, items)]
    a_rb = [bf(jnp.where(it["incl"], m[c:], 0.0)) for m, it in zip(m_b, items)]
    a_ak = [bf(jnp.where(it["strict"], m[:c], 0.0)) for m, it in zip(m_k, items)]
    a_rk = [bf(jnp.where(it["incl"], m[c:], 0.0)) for m, it in zip(m_k, items)]
    ws = [_bdot(ak, it["v"]) + _bdot(it["at"], s, NT_DIMS) for ak, it, s in zip(a_ak, items, s_bf)]
    for step in range(n_sq):
        wb = [bf(w) for w in ws]
        ws = [w + _bdot(x, b) for w, x, b in zip(ws, xs, wb)]
        if step < n_sq - 1:
            xs = [bf(_bdot(x, x)) for x in xs]
    wb = [bf(w) for w in ws]
    ys = [_bdot(rb, w) + _bdot(rk, it["v"]) + _bdot(it["rt"], s, NT_DIMS)
          for rb, rk, w, it, s in zip(a_rb, a_rk, wb, items, s_bf)]
    s_new = [it["s"] * it["gtot"] + _bdot(w, it["bh"], TN_DIMS) + _bdot(it["v"], it["kh"], TN_DIMS)
             for w, it in zip(wb, items)]
    return ys, s_new


def _rwkv_kernel(zf_ref, zfp_ref, zfn_ref, zb_ref, zbp_ref, zbn_ref, s0_ref,
                 shift_ref, kk_ref, ka_ref, rk_ref, w0_ref, wup_ref, a0_ref, aup_ref, gup_ref,
                 *refs, n_chunks, n_group, has_init, emit_state, n_alias):
    refs = refs[n_alias:]
    if emit_state:
        yf_ref, yb_ref, bonus_ref, gate_ref, st_ref, h_ref = refs
    else:
        yf_ref, yb_ref, bonus_ref, gate_ref, h_ref = refs
        st_ref = None
    i = pl.program_id(1)
    ones_bd = _head_ones(RW_WIDTH)

    @pl.when(i == 0)
    def _():
        if has_init:
            h_ref[...] = s0_ref[...]
        else:
            h_ref[...] = jnp.zeros(h_ref.shape, F32)

    params = (shift_ref, kk_ref, ka_ref, rk_ref, w0_ref, wup_ref, a0_ref, aup_ref)
    first = i == 0
    last = i == n_chunks - 1
    zero_row = jnp.zeros((1, RW_PROJ), F32)
    masks = (_chunk_masks(False), _chunk_masks(True))
    items = []
    for g in range(n_group):
        pf = _rw_pre(zf_ref[g], jnp.where(first, zero_row, zfp_ref[g, 0, SUBLANES - 1:SUBLANES, :]),
                     jnp.where(last, zero_row, zfn_ref[g, 0, 0:1, :]), *params, 0, ones_bd)
        pb = _rw_pre(zb_ref[g], jnp.where(last, zero_row, zbp_ref[g, 0, SUBLANES - 1:SUBLANES, :]),
                     jnp.where(first, zero_row, zbn_ref[g, 0, 0:1, :]), *params, 1, ones_bd)
        a_sig_b = _sigmoid(a0_ref[1:2, :] + _dot3(pf["ad"], aup_ref[1]))
        k_b = pf["k_raw"] * (1.0 + (a_sig_b - 1.0) * ka_ref[...])
        bonus_ref[g] = pf["bonus"] + _head_sum(pf["r"] * k_b * rk_ref[...], ones_bd) * pf["v"]
        gate_ref[g] = _bdot(_sigmoid(pf["gd"]), gup_ref[...])
        for d, p in ((0, pf), (1, pb)):
            incl, strict = masks[d]
            sc = _wkv_scale(p, incl, d == 1)
            for h in range(RW_HEADS):
                sl = slice(h * HEAD_DIM, (h + 1) * HEAD_DIM)
                it = {k: v[:, sl] for k, v in sc.items()}
                it.update(s=h_ref[g, d, h], incl=incl, strict=strict, where=(g, d, h))
                items.append(it)

    ys, s_new = _wkv_chunks(items)
    for it, y, s in zip(items, ys, s_new):
        g, d, h = it["where"]
        y_ref = yb_ref if d else yf_ref
        y_ref[g, :, h * HEAD_DIM:(h + 1) * HEAD_DIM] = y
        h_ref[g, d, h] = s

    if emit_state:
        @pl.when(last)
        def _():
            for it, s in zip(items, s_new):
                g, d, h = it["where"]
                st_ref[g, 0, d, h] = s


def _rwkv_call(zrw, s0, lp, prev_outs, *, n_seq, seq_len, row_base, has_init, emit_state,
               state_prev=None, layer=0):
    g = min(RW_SEQ_GROUP, n_seq)
    n_chunks = seq_len // CHUNK
    n_rows8 = seq_len // SUBLANES
    per8 = CHUNK // SUBLANES
    n_view = T_ALL // seq_len
    base_g = row_base // seq_len // g
    z3 = zrw.reshape(n_view, seq_len, RW_PROJ)
    z4 = zrw.reshape(n_view, n_rows8, SUBLANES, RW_PROJ)

    fwd = lambda i: i
    bwd = lambda i: n_chunks - 1 - i
    main = lambda c: pl.BlockSpec((g, CHUNK, RW_PROJ), lambda b, i: (base_g + b, c(i), 0))
    prev8 = lambda c: pl.BlockSpec((g, 1, SUBLANES, RW_PROJ),
                                   lambda b, i: (base_g + b, jnp.maximum(c(i) * per8 - 1, 0), 0, 0))
    next8 = lambda c: pl.BlockSpec((g, 1, SUBLANES, RW_PROJ),
                                   lambda b, i: (base_g + b, jnp.minimum((c(i) + 1) * per8, n_rows8 - 1), 0, 0))
    full = lambda shape: pl.BlockSpec(shape, lambda b, i: (0,) * len(shape))
    out_f = pl.BlockSpec((g, CHUNK, RW_WIDTH), lambda b, i: (base_g + b, i, 0))
    out_b = pl.BlockSpec((g, CHUNK, RW_WIDTH), lambda b, i: (base_g + b, bwd(i), 0))
    tok = jax.ShapeDtypeStruct((n_view, seq_len, RW_WIDTH), F32)
    out_shape = [tok, tok, tok, tok]
    out_specs = [out_f, out_b, out_f, out_f]
    state_block = (g, 2, RW_HEADS, HEAD_DIM, HEAD_DIM)
    prev_views = [p.reshape(n_view, seq_len, RW_WIDTH) for p in prev_outs]
    n_in = 16
    aliases = {n_in + j: j for j in range(len(prev_views))}
    if emit_state:
        out_shape.append(jax.ShapeDtypeStruct((n_seq, DEPTH, 2, RW_HEADS, HEAD_DIM, HEAD_DIM), F32))
        out_specs.append(pl.BlockSpec((g, 1, 2, RW_HEADS, HEAD_DIM, HEAD_DIM),
                                      lambda b, i: (b, layer, 0, 0, 0, 0)))
        if state_prev is not None:
            aliases[n_in + len(prev_views)] = 4
            prev_views.append(state_prev)
    n_alias = len(prev_views)
    state_spec = pl.BlockSpec(state_block, lambda b, i: (b if has_init else 0, 0, 0, 0, 0))
    kern = functools.partial(_rwkv_kernel, n_chunks=n_chunks, n_group=g, has_init=has_init,
                             emit_state=emit_state, n_alias=n_alias)
    outs = pl.pallas_call(
        kern,
        out_shape=tuple(out_shape),
        grid=(n_seq // g, n_chunks),
        in_specs=[main(fwd), prev8(fwd), next8(fwd), main(bwd), prev8(bwd), next8(bwd), state_spec,
                  full((3, RW_PROJ)), full((1, RW_WIDTH)), full((1, RW_WIDTH)), full((1, RW_WIDTH)),
                  full((2, RW_WIDTH)), full((2, DECAY_LORA, RW_WIDTH)),
                  full((2, RW_WIDTH)), full((2, ICLR_LORA, RW_WIDTH)), full((GATE_LORA, RW_WIDTH))]
                 + [pl.BlockSpec(memory_space=pl.ANY)] * n_alias,
        out_specs=tuple(out_specs),
        scratch_shapes=[pltpu.VMEM(state_block, F32)],
        input_output_aliases=aliases,
        compiler_params=_cparams(("arbitrary", "arbitrary")),
        name="rwkv_scan_init" if has_init else "rwkv_scan_zero",
    )(z3, z4, z4, z3, z4, z4, s0, lp["rw_shift"], lp["rw_k_k"], lp["rw_k_a"], lp["rw_r_k"],
      lp["rw_w0"], lp["rw_w_up"], lp["rw_a0"], lp["rw_a_up"], lp["rw_g_up"], *prev_views)
    return tuple(o.reshape(T_ALL, RW_WIDTH) for o in outs[:4]) + tuple(outs[4:])


def _softmax_pv(scores, values):
    m = scores[0].max(axis=-1, keepdims=True)
    for s in scores[1:]:
        m = jnp.maximum(m, s.max(axis=-1, keepdims=True))
    es = [jnp.exp(s - m) for s in scores]
    l = es[0].sum(axis=-1, keepdims=True)
    for e in es[1:]:
        l = l + e.sum(axis=-1, keepdims=True)
    o = _bdot(es[0], values[0])
    for e, v in zip(es[1:], values[1:]):
        o = o + _bdot(e, v)
    return o * (1.0 / l)


def _head(x, h):
    return x[:, h * HEAD_DIM:(h + 1) * HEAD_DIM]


def _ctx_attn_kernel(zna_ref, zgq_ref, zgkv_ref, *refs):
    att_ref, nk_ref, nv_ref, gk_ref, gv_ref = refs[-5:]
    for b in range(CTX_SEQ_GROUP):
        rows = slice(b * SEQ, (b + 1) * SEQ)
        zna = zna_ref[rows, :]
        q, k, v = zna[:, :NA_WIDTH], zna[:, NA_WIDTH:2 * NA_WIDTH], zna[:, 2 * NA_WIDTH:]
        nk_ref[b, 0] = k
        nv_ref[b, 0] = v
        gq = zgq_ref[rows, :]
        gkv = zgkv_ref[rows, :]
        gk, gv = gkv[:, :GKV_WIDTH], gkv[:, GKV_WIDTH:]
        gk_ref[b, 0] = gk
        gv_ref[b, 0] = gv
        gq, gk, gv = (gq * ATT_SCALE).astype(BF16), gk.astype(BF16), gv.astype(BF16)
        q, k, v = (q * ATT_SCALE).astype(BF16), k.astype(BF16), v.astype(BF16)
        for h in range(GQA_Q_HEADS):
            kv = h // GQA_GROUP
            s = _bdot(_head(gq, h), _head(gk, kv), NT_DIMS)
            att_ref[rows, h * HEAD_DIM:(h + 1) * HEAD_DIM] = _softmax_pv([s], [_head(gv, kv)]).astype(BF16)
        for h in range(NA_HEADS):
            s = _bdot(_head(q, h), _head(k, h), NT_DIMS)
            o = _softmax_pv([s], [_head(v, h)])
            att_ref[rows, GQ_WIDTH + h * HEAD_DIM:GQ_WIDTH + (h + 1) * HEAD_DIM] = o.astype(BF16)


def _ctx_attn_call(zna, zgq, zgkv, caches, layer):
    g = CTX_SEQ_GROUP
    row = lambda b: (b, 0)
    bat = lambda b: (b, layer, 0, 0)
    cache = lambda w: jax.ShapeDtypeStruct((BATCH, DEPTH, SEQ, w), F32)
    n_alias = 0 if caches is None else len(caches)
    return pl.pallas_call(
        _ctx_attn_kernel,
        out_shape=(jax.ShapeDtypeStruct((T_ALL, ATT_WIDTH), BF16),
                   cache(NA_WIDTH), cache(NA_WIDTH), cache(GKV_WIDTH), cache(GKV_WIDTH)),
        grid=(BATCH // g,),
        in_specs=[pl.BlockSpec((g * SEQ, NA_PROJ), row), pl.BlockSpec((g * SEQ, GQ_WIDTH), row),
                  pl.BlockSpec((g * SEQ, 2 * GKV_WIDTH), row)] + [pl.BlockSpec(memory_space=pl.ANY)] * n_alias,
        out_specs=(pl.BlockSpec((g * SEQ, ATT_WIDTH), row),
                   pl.BlockSpec((g, 1, SEQ, NA_WIDTH), bat), pl.BlockSpec((g, 1, SEQ, NA_WIDTH), bat),
                   pl.BlockSpec((g, 1, SEQ, GKV_WIDTH), bat), pl.BlockSpec((g, 1, SEQ, GKV_WIDTH), bat)),
        input_output_aliases={3 + j: 1 + j for j in range(n_alias)},
        compiler_params=_cparams(("arbitrary",)),
        name="ctx_attention",
    )(zna, zgq, zgkv, *(caches or ()))


def _lat_na_kernel(zna_ref, ck_ref, cv_ref, tb_ref, att_in_ref, att_ref, kv_bf, ckv_bf):
    del att_in_ref
    i = pl.program_id(0)

    @pl.when(i == 0)
    def _():
        for b in range(DEC_BATCH):
            kv_bf[b] = zna_ref[b, :, NA_WIDTH:].astype(BF16)
            ckv_bf[b, :, :NA_WIDTH] = ck_ref[b].astype(BF16)
            ckv_bf[b, :, NA_WIDTH:] = cv_ref[b].astype(BF16)

    start = jnp.clip(i - NA_WIN_ROWS // 2, 0, GRID_H - NA_WIN_ROWS)
    n_loc = NA_WIN_ROWS * GRID_W
    w0 = pl.multiple_of(start * GRID_W, GRID_W)
    dr0 = start - i + NA_WIN_ROWS - 1
    qs, kvs, cs = [], [], []
    for b in range(DEC_BATCH):
        q = zna_ref[b, pl.ds(pl.multiple_of(i * GRID_W, GRID_W), GRID_W), 0:NA_WIDTH]
        qs.append((q * ATT_SCALE).astype(BF16))
        kvs.append(kv_bf[b, pl.ds(w0, n_loc), :])
        cs.append(ckv_bf[b])
    outs = [[] for _ in range(DEC_BATCH)]
    for h in range(NA_HEADS):
        bias = jnp.concatenate([tb_ref[h, dr0 + r] for r in range(NA_WIN_ROWS)], axis=1)
        for b in range(DEC_BATCH):
            qh = _head(qs[b], h)
            s_loc = _bdot(qh, _head(kvs[b], h), NT_DIMS) + bias
            s_ctx = _bdot(qh, _head(cs[b], h), NT_DIMS)
            outs[b].append(_softmax_pv([s_loc, s_ctx],
                                       [_head(kvs[b], NA_HEADS + h), _head(cs[b], NA_HEADS + h)]))
    for b in range(DEC_BATCH):
        att_ref[b] = jnp.concatenate(outs[b], axis=1).astype(BF16)


def _na_bias_tables(rpb):
    n_off = 2 * NA_WIN_COLS - 1
    w = np.arange(GRID_W)[:, None]
    kc = np.arange(GRID_W)[None, :]
    cs = np.clip(w - NA_WIN_COLS // 2, 0, GRID_W - NA_WIN_COLS)
    valid = (kc >= cs) & (kc < cs + NA_WIN_COLS)
    off = np.where(valid, kc - w + NA_WIN_COLS - 1, n_off)
    onehot = (off[:, :, None] == np.arange(n_off + 1)).astype(np.float32)
    ext = jnp.concatenate([rpb, jnp.full(rpb.shape[:-1] + (1,), NEG_BIG, F32)], axis=-1)
    return jnp.einsum("lhrd,wkd->lhrwk", ext, jnp.asarray(onehot), precision=lax.Precision.HIGHEST)


def _lat_na_call(zna, ck, cv, tb, att):
    n_view = T_ALL // DEC_SEQ
    lat_blk = T_CTX // DEC_SEQ // DEC_BATCH
    out = pl.pallas_call(
        _lat_na_kernel,
        out_shape=jax.ShapeDtypeStruct((n_view, DEC_SEQ, ATT_WIDTH), BF16),
        grid=(GRID_H,),
        in_specs=[pl.BlockSpec((DEC_BATCH, DEC_SEQ, NA_PROJ), lambda i: (lat_blk, 0, 0)),
                  pl.BlockSpec((DEC_BATCH, PAST_LEN, NA_WIDTH), lambda i: (0, 0, 0)),
                  pl.BlockSpec((DEC_BATCH, PAST_LEN, NA_WIDTH), lambda i: (0, 0, 0)),
                  pl.BlockSpec((NA_HEADS, 2 * NA_WIN_ROWS - 1, GRID_W, GRID_W), lambda i: (0, 0, 0, 0)),
                  pl.BlockSpec(memory_space=pl.ANY)],
        out_specs=pl.BlockSpec((DEC_BATCH, GRID_W, NA_WIDTH), lambda i: (lat_blk, i, GQ_WIDTH // NA_WIDTH)),
        scratch_shapes=[pltpu.VMEM((DEC_BATCH, DEC_SEQ, 2 * NA_WIDTH), BF16),
                        pltpu.VMEM((DEC_BATCH, PAST_LEN, 2 * NA_WIDTH), BF16)],
        input_output_aliases={4: 0},
        compiler_params=_cparams(("arbitrary",)),
        name="latent_neighbourhood_attention",
    )(zna.reshape(n_view, DEC_SEQ, NA_PROJ), ck, cv, tb, att.reshape(n_view, DEC_SEQ, ATT_WIDTH))
    return out.reshape(T_ALL, ATT_WIDTH)


def _lat_gqa_kernel(zgq_ref, zgkv_ref, ck_ref, cv_ref, att_in_ref, att_ref):
    del att_in_ref
    q = (zgq_ref[...] * ATT_SCALE).astype(BF16)
    kv = zgkv_ref[...].astype(BF16)
    kl, vl = kv[:, :GKV_WIDTH], kv[:, GKV_WIDTH:]
    kc = ck_ref[0].astype(BF16)
    vc = cv_ref[0].astype(BF16)
    for h in range(GQA_Q_HEADS):
        g = h // GQA_GROUP
        qh = _head(q, h)
        s_c = _bdot(qh, _head(kc, g), NT_DIMS)
        s_l = _bdot(qh, _head(kl, g), NT_DIMS)
        o = _softmax_pv([s_c, s_l], [_head(vc, g), _head(vl, g)])
        att_ref[:, h * HEAD_DIM:(h + 1) * HEAD_DIM] = o.astype(BF16)


def _lat_gqa_call(zgq, zgkv, ck, cv, att):
    n_q = DEC_SEQ // TQ_GQA
    first_q = T_CTX // TQ_GQA
    lat_blk = T_CTX // DEC_SEQ
    return pl.pallas_call(
        _lat_gqa_kernel,
        out_shape=jax.ShapeDtypeStruct((T_ALL, ATT_WIDTH), BF16),
        grid=(DEC_BATCH, n_q),
        in_specs=[pl.BlockSpec((TQ_GQA, GQ_WIDTH), lambda b, j: (first_q + b * n_q + j, 0)),
                  pl.BlockSpec((DEC_SEQ, 2 * GKV_WIDTH), lambda b, j: (lat_blk + b, 0)),
                  pl.BlockSpec((1, PAST_LEN, GKV_WIDTH), lambda b, j: (b, 0, 0)),
                  pl.BlockSpec((1, PAST_LEN, GKV_WIDTH), lambda b, j: (b, 0, 0)),
                  pl.BlockSpec(memory_space=pl.ANY)],
        out_specs=pl.BlockSpec((TQ_GQA, GQ_WIDTH), lambda b, j: (first_q + b * n_q + j, 0)),
        input_output_aliases={4: 0},
        compiler_params=_cparams(("arbitrary", "arbitrary")),
        name="latent_gqa_attention",
    )(zgq, zgkv, ck, cv, att)


def _out_proj_kernel(x_ref, yf_ref, yb_ref, bonus_ref, gate_ref, att_ref, mod_ref, lng_ref, lnb_ref,
                     wout_ref, g2_ref, rw_ref, rb_ref,
                     x1_ref, h2_ref, slot_ref, gates_ref, cnt_ref):
    ones_bd = _head_ones(RW_WIDTH)
    o = yf_ref[...] + yb_ref[...]
    mu = _head_sum(o, ones_bd) * (1.0 / HEAD_DIM)
    dlt = o - mu
    var = _head_sum(dlt * dlt, ones_bd) * (1.0 / HEAD_DIM)
    ln = dlt * lax.rsqrt(var + LNX_EPS) * lng_ref[...] + lnb_ref[...]
    rw = ((ln + bonus_ref[...]) * gate_ref[...]).astype(BF16)
    att = att_ref[...]
    mix = (jnp.dot(rw, wout_ref[0:RW_WIDTH, :], preferred_element_type=F32)
           + jnp.dot(att[:, :GQ_WIDTH], wout_ref[RW_WIDTH + NA_WIDTH:, :], preferred_element_type=F32)
           + jnp.dot(att[:, GQ_WIDTH:], wout_ref[RW_WIDTH:RW_WIDTH + NA_WIDTH, :],
                     preferred_element_type=F32))
    gate1 = mod_ref[0, 2:3, :]
    shift2 = mod_ref[0, 3:4, :]
    scale2 = mod_ref[0, 4:5, :]
    x1 = x_ref[...] + gate1 * mix
    x1_ref[...] = x1
    h2 = _rms(x1, g2_ref[...]) * (1.0 + scale2) + shift2
    h2_ref[...] = h2.astype(BF16)

    logits = _dot3(h2, rw_ref[...]) + rb_ref[...]
    tm = logits.shape[0]
    col = lax.broadcasted_iota(jnp.int32, (tm, N_EXPERTS), 1)
    lane4 = lax.broadcasted_iota(jnp.int32, (tm, TOP_K), 1)
    work = logits
    sels, vals = [], []
    for k in range(TOP_K):
        m = work.max(axis=-1, keepdims=True)
        idx = jnp.min(jnp.where(work == m, col, N_EXPERTS), axis=-1, keepdims=True)
        sel = col == idx
        sels.append(sel)
        vals.append(m)
        work = jnp.where(sel, -jnp.inf, work)
    es = [jnp.exp(v - vals[0]) for v in vals]
    inv = 1.0 / (es[0] + es[1] + es[2] + es[3])
    gates = jnp.zeros((tm, TOP_K), F32)
    for k in range(TOP_K):
        gates = jnp.where(lane4 == k, es[k] * inv, gates)
    assign = jnp.zeros((tm, N_EXPERTS), F32)
    for sel in sels:
        assign = assign + sel.astype(F32)
    r_i = lax.broadcasted_iota(jnp.int32, (tm, tm), 0)
    c_i = lax.broadcasted_iota(jnp.int32, (tm, tm), 1)
    before = jnp.dot((c_i < r_i).astype(BF16), assign.astype(BF16), preferred_element_type=F32)
    cnt = jnp.sum(assign, axis=0, keepdims=True)
    seg_units = jnp.floor((cnt + (SEG_ALIGN - 1)) * (1.0 / SEG_ALIGN))
    e_r = lax.broadcasted_iota(jnp.int32, (N_EXPERTS, N_EXPERTS), 0)
    e_c = lax.broadcasted_iota(jnp.int32, (N_EXPERTS, N_EXPERTS), 1)
    loc = SEG_ALIGN * jnp.dot(seg_units.astype(BF16), (e_r < e_c).astype(BF16),
                              preferred_element_type=F32)
    pos = before + loc
    slot = jnp.zeros((tm, TOP_K), F32)
    for k in range(TOP_K):
        sk = jnp.sum(jnp.where(sels[k], pos, 0.0), axis=-1, keepdims=True)
        slot = jnp.where(lane4 == k, sk, slot)
    slot_ref[...] = slot.astype(jnp.int32)
    gates_ref[...] = gates
    cnt_ref[0] = cnt.astype(jnp.int32)


def _out_proj_call(x, yf, yb, bonus, gate, att, mods_l, lp):
    n_blk = T_ALL // TM_TOK
    row = lambda i: (i, 0)
    full2 = lambda r, c: pl.BlockSpec((r, c), lambda i: (0, 0))
    tokw = lambda w: pl.BlockSpec((TM_TOK, w), row)
    return pl.pallas_call(
        _out_proj_kernel,
        out_shape=(jax.ShapeDtypeStruct((T_ALL, D_MODEL), F32),
                   jax.ShapeDtypeStruct((T_ALL, D_MODEL), BF16),
                   jax.ShapeDtypeStruct((T_ALL, TOP_K), jnp.int32),
                   jax.ShapeDtypeStruct((T_ALL, TOP_K), F32),
                   jax.ShapeDtypeStruct((n_blk, 1, N_EXPERTS), jnp.int32)),
        grid=(n_blk,),
        in_specs=[tokw(D_MODEL), tokw(RW_WIDTH), tokw(RW_WIDTH), tokw(RW_WIDTH), tokw(RW_WIDTH),
                  tokw(ATT_WIDTH),
                  pl.BlockSpec((1, N_MOD, D_MODEL), lambda i: (_group_of_block(i, TM_TOK), 0, 0)),
                  full2(1, RW_WIDTH), full2(1, RW_WIDTH), full2(D_MODEL, D_MODEL), full2(1, D_MODEL),
                  full2(D_MODEL, N_EXPERTS), full2(1, N_EXPERTS)],
        out_specs=(tokw(D_MODEL), tokw(D_MODEL), tokw(TOP_K), tokw(TOP_K),
                   pl.BlockSpec((1, 1, N_EXPERTS), lambda i: (i, 0, 0))),
        compiler_params=_cparams(("arbitrary",)),
        name="out_proj_router",
    )(x, yf, yb, bonus, gate, att, mods_l, lp["rw_ln_g"], lp["rw_ln_b"], lp["w_out"], lp["norm2_g"],
      lp["router_w"], lp["router_b"])


def _for_each_piece(n_units, max_bit, fn):
    for b in range(max_bit, -1, -1):
        @pl.when(((n_units >> b) & 1) == 1)
        def _(b=b):
            off = ((n_units >> (b + 1)) << (b + 1)) * SEG_ALIGN
            fn(pl.multiple_of(off, SEG_ALIGN), SEG_ALIGN << b)


def _segment_copies(local_ref, sorted_hbm, sem, blk, seg_ref, loc_ref, dst_ref, total_ref, to_sorted, wait):
    def copy(loc, dst, size):
        a = local_ref.at[pl.ds(pl.multiple_of(loc, SEG_ALIGN), size), :]
        b = sorted_hbm.at[pl.ds(pl.multiple_of(dst, SEG_ALIGN), size), :]
        return pltpu.make_async_copy(a, b, sem) if to_sorted else pltpu.make_async_copy(b, a, sem)

    if wait:
        _for_each_piece(total_ref[blk], TOTAL_MAX_BIT, lambda off, size: copy(0, 0, size).wait())
        return

    def body(e, carry):
        t = blk * N_EXPERTS + e
        loc = loc_ref[t]
        dst = dst_ref[t]
        _for_each_piece(seg_ref[t], SEG_MAX_BIT, lambda off, size: copy(loc + off, dst + off, size).start())
        return carry
    lax.fori_loop(0, N_EXPERTS, body, 0)


def _dispatch_kernel(seg_ref, loc_ref, dst_ref, total_ref, tail_ref, tail_dst_ref, h2_ref, slot_ref, xs_hbm,
                     xs_local, zero_buf, sem, zsem):
    j = pl.program_id(0)
    n = pl.num_programs(0)
    buf = j % 2

    @pl.when(j == 0)
    def _():
        zero_buf[...] = jnp.zeros(zero_buf.shape, BF16)

        def tails(wait):
            def body(e, carry):
                def piece(off, size):
                    cp = pltpu.make_async_copy(
                        zero_buf.at[pl.ds(0, size), :],
                        xs_hbm.at[pl.ds(pl.multiple_of(tail_dst_ref[e] + off, SEG_ALIGN), size), :], zsem)
                    if wait:
                        cp.wait()
                    else:
                        cp.start()
                _for_each_piece(tail_ref[e], TAIL_MAX_BIT, piece)
                return carry
            lax.fori_loop(0, N_EXPERTS, body, 0)
        tails(False)
        tails(True)

    lane = lax.broadcasted_iota(jnp.int32, (TM_TOK, LANES), 1)
    slots = slot_ref[...].astype(F32)
    wide = jnp.zeros((TM_TOK, LANES), F32)
    for k in range(TOP_K):
        wide = jnp.where(lane == k, slots[:, k:k + 1], wide)
    slot_rows = wide.T
    h2 = h2_ref[...]
    for c in range(LOCAL_ROWS // TM_TOK):
        row = lax.broadcasted_iota(jnp.int32, (TM_TOK, TM_TOK), 0).astype(F32) + float(c * TM_TOK)
        onehot = jnp.zeros((TM_TOK, TM_TOK), F32)
        for k in range(TOP_K):
            onehot = jnp.where(row == slot_rows[k:k + 1, :], 1.0, onehot)
        onehot = onehot.astype(BF16)
        xs_local[buf, c * TM_TOK:(c + 1) * TM_TOK, :] = jnp.dot(
            onehot, h2, preferred_element_type=F32).astype(BF16)

    tabs = (seg_ref, loc_ref, dst_ref, total_ref)

    @pl.when(j > 0)
    def _():
        _segment_copies(xs_local.at[1 - buf], xs_hbm, sem.at[1 - buf], j - 1, *tabs, True, True)

    _segment_copies(xs_local.at[buf], xs_hbm, sem.at[buf], j, *tabs, True, False)

    @pl.when(j == n - 1)
    def _():
        _segment_copies(xs_local.at[buf], xs_hbm, sem.at[buf], j, *tabs, True, True)


def _dispatch_call(tabs, h2, slot):
    n_blk = T_ALL // TM_TOK
    row = lambda j, *_: (j, 0)
    grid_spec = pltpu.PrefetchScalarGridSpec(
        num_scalar_prefetch=6,
        grid=(n_blk,),
        in_specs=[pl.BlockSpec((TM_TOK, D_MODEL), row), pl.BlockSpec((TM_TOK, TOP_K), row)],
        out_specs=pl.BlockSpec(memory_space=pl.ANY),
        scratch_shapes=[pltpu.VMEM((2, LOCAL_ROWS, D_MODEL), BF16), pltpu.VMEM((TM_EXP, D_MODEL), BF16),
                        pltpu.SemaphoreType.DMA((2,)), pltpu.SemaphoreType.DMA],
    )
    return pl.pallas_call(
        _dispatch_kernel,
        out_shape=jax.ShapeDtypeStruct((N_SLOTS, D_MODEL), BF16),
        grid_spec=grid_spec,
        compiler_params=_cparams(("arbitrary",)),
        name="moe_dispatch",
    )(tabs["seg"], tabs["loc"], tabs["dst"], tabs["total"], tabs["tail"], tabs["tail_dst"], h2, slot)


def _expert_kernel(be_ref, first_ref, par_ref, nexte_ref, meta_ref, x_ref, wgu_hbm, bgu_ref, wd_hbm, bd_ref,
                   y_ref, wgu_f, wd_f, wgu_bf, wd_bf, sem, *, layer):
    i = pl.program_id(0)
    n_used = meta_ref[0]

    def weight_copies(e, buf):
        rows = D_MODEL // WEIGHT_DMA_PARTS
        parts = []
        for p in range(WEIGHT_DMA_PARTS):
            sl = pl.ds(p * rows, rows)
            parts.append(pltpu.make_async_copy(wgu_hbm.at[layer, e, sl], wgu_f.at[buf, sl], sem.at[buf]))
            parts.append(pltpu.make_async_copy(wd_hbm.at[layer, e, sl], wd_f.at[buf, sl], sem.at[buf]))
        return parts

    @pl.when(i < n_used)
    def _():
        e = be_ref[i]
        buf = par_ref[i]
        is_first = first_ref[i] == 1

        @pl.when(is_first)
        def _():
            @pl.when(i == 0)
            def _():
                for cp in weight_copies(e, buf):
                    cp.start()
            for cp in weight_copies(e, buf):
                cp.wait()
            wgu_bf[...] = wgu_f[buf].astype(BF16)
            wd_bf[...] = wd_f[buf].astype(BF16)

        x = x_ref[...]
        y = bd_ref[0, 0]
        for c in range(D_FF // FF_SLAB):
            g_cols = slice(c * FF_SLAB, (c + 1) * FF_SLAB)
            l_cols = slice(D_FF + c * FF_SLAB, D_FF + (c + 1) * FF_SLAB)
            g = jnp.dot(x, wgu_bf[:, g_cols], preferred_element_type=F32) + bgu_ref[0, 0, :, g_cols]
            li = jnp.dot(x, wgu_bf[:, l_cols], preferred_element_type=F32) + bgu_ref[0, 0, :, l_cols]
            glu = jnp.minimum(g, SWIGLU_LIMIT)
            lin = jnp.clip(li, -SWIGLU_LIMIT, SWIGLU_LIMIT)
            act = glu * _sigmoid(SWIGLU_ALPHA * glu) * (lin + 1.0)
            y = y + jnp.dot(act.astype(BF16), wd_bf[g_cols, :], preferred_element_type=F32)
        y_ref[...] = y.astype(BF16)

        nxt = nexte_ref[i]

        @pl.when(is_first & (nxt >= 0))
        def _():
            for cp in weight_copies(nxt, 1 - buf):
                cp.start(priority=1)


def _expert_call(tabs, xs, w_gu, b_gu, w_down, b_down, layer):
    def bmap(i, be, first, par, nxt, meta_):
        return (layer, be[i], 0, 0)

    def rmap(i, be, first, par, nxt, meta_):
        return (jnp.minimum(i, meta_[0] - 1), 0)

    grid_spec = pltpu.PrefetchScalarGridSpec(
        num_scalar_prefetch=5,
        grid=(N_SLOT_BLOCKS,),
        in_specs=[pl.BlockSpec((TM_EXP, D_MODEL), rmap),
                  pl.BlockSpec(memory_space=pl.ANY),
                  pl.BlockSpec((1, 1, 1, 2 * D_FF), bmap),
                  pl.BlockSpec(memory_space=pl.ANY),
                  pl.BlockSpec((1, 1, 1, D_MODEL), bmap)],
        out_specs=pl.BlockSpec((TM_EXP, D_MODEL), rmap),
        scratch_shapes=[pltpu.VMEM((2, D_MODEL, 2 * D_FF), F32), pltpu.VMEM((2, D_FF, D_MODEL), F32),
                        pltpu.VMEM((D_MODEL, 2 * D_FF), BF16), pltpu.VMEM((D_FF, D_MODEL), BF16),
                        pltpu.SemaphoreType.DMA((2,))],
    )
    return pl.pallas_call(
        functools.partial(_expert_kernel, layer=layer),
        out_shape=jax.ShapeDtypeStruct((N_SLOTS, D_MODEL), BF16),
        grid_spec=grid_spec,
        compiler_params=_cparams(("arbitrary",)),
        name="moe_experts",
    )(tabs["block_e"], tabs["first"], tabs["parity"], tabs["next_e"], tabs["n_used"], xs, w_gu,
      b_gu.reshape(DEPTH, N_EXPERTS, 1, 2 * D_FF), w_down, b_down.reshape(DEPTH, N_EXPERTS, 1, D_MODEL))


def _combine_kernel(seg_ref, loc_ref, dst_ref, total_ref, ys_hbm, x1_ref, slot_ref, gates_ref, mod_ref,
                    fg_ref, *rest, final):
    *o_refs, ybuf, sem = rest
    j = pl.program_id(0)
    n = pl.num_programs(0)
    buf = j % 2

    tabs = (seg_ref, loc_ref, dst_ref, total_ref)

    @pl.when(j == 0)
    def _():
        _segment_copies(ybuf.at[0], ys_hbm, sem.at[0], 0, *tabs, False, False)

    @pl.when(j + 1 < n)
    def _():
        _segment_copies(ybuf.at[1 - buf], ys_hbm, sem.at[1 - buf], j + 1, *tabs, False, False)

    _segment_copies(ybuf.at[buf], ys_hbm, sem.at[buf], j, *tabs, False, True)

    slots = slot_ref[...]
    gates = gates_ref[...]
    total = total_ref[j] * SEG_ALIGN
    ff = jnp.zeros((TM_TOK, D_MODEL), F32)
    for c in range(LOCAL_ROWS // TM_TOK):
        col = lax.broadcasted_iota(jnp.int32, (TM_TOK, TM_TOK), 1) + c * TM_TOK
        q = jnp.zeros((TM_TOK, TM_TOK), F32)
        for k in range(TOP_K):
            q = jnp.where(col == slots[:, k:k + 1], gates[:, k:k + 1], q)
        rows = lax.broadcasted_iota(jnp.int32, (TM_TOK, D_MODEL), 0) + c * TM_TOK
        y = jnp.where(rows < total, ybuf[buf, c * TM_TOK:(c + 1) * TM_TOK, :], jnp.zeros((), BF16))
        ff = ff + jnp.dot(q.astype(BF16), y, preferred_element_type=F32)
    x = x1_ref[...] + mod_ref[0, 5:6, :] * ff
    if final:
        x = _rms(x, fg_ref[...])
        ctx_ref, lat_ref = o_refs
        first_lat = T_CTX // TM_TOK

        @pl.when(j < first_lat)
        def _():
            ctx_ref[...] = x

        @pl.when(j >= first_lat)
        def _():
            lat_ref[...] = x
    else:
        o_refs[0][...] = x


def _combine_call(tabs, ys, x1, slot, gates, mods_l, final_g, final):
    n_blk = T_ALL // TM_TOK
    row = lambda j, *_: (j, 0)
    first_lat = T_CTX // TM_TOK
    if final:
        out_shape = (jax.ShapeDtypeStruct((T_CTX, D_MODEL), F32), jax.ShapeDtypeStruct((T_LAT, D_MODEL), F32))
        out_specs = (pl.BlockSpec((TM_TOK, D_MODEL), lambda j, *_: (jnp.minimum(j, first_lat - 1), 0)),
                     pl.BlockSpec((TM_TOK, D_MODEL), lambda j, *_: (jnp.maximum(j - first_lat, 0), 0)))
    else:
        out_shape = jax.ShapeDtypeStruct((T_ALL, D_MODEL), F32)
        out_specs = pl.BlockSpec((TM_TOK, D_MODEL), row)
    grid_spec = pltpu.PrefetchScalarGridSpec(
        num_scalar_prefetch=4,
        grid=(n_blk,),
        in_specs=[pl.BlockSpec(memory_space=pl.ANY),
                  pl.BlockSpec((TM_TOK, D_MODEL), row),
                  pl.BlockSpec((TM_TOK, TOP_K), row),
                  pl.BlockSpec((TM_TOK, TOP_K), row),
                  pl.BlockSpec((1, N_MOD, D_MODEL), lambda j, *_: (_group_of_block(j, TM_TOK), 0, 0)),
                  pl.BlockSpec((1, D_MODEL), lambda j, *_: (0, 0))],
        out_specs=out_specs,
        scratch_shapes=[pltpu.VMEM((2, LOCAL_ROWS, D_MODEL), BF16), pltpu.SemaphoreType.DMA((2,))],
    )
    return pl.pallas_call(
        functools.partial(_combine_kernel, final=final),
        out_shape=out_shape,
        grid_spec=grid_spec,
        compiler_params=_cparams(("arbitrary",)),
        name="moe_combine_final" if final else "moe_combine",
    )(tabs["seg"], tabs["loc"], tabs["dst"], tabs["total"], ys, x1, slot, gates, mods_l, final_g)


def _routing_tables(cnt):
    i32 = jnp.int32
    cnt = cnt.reshape(T_ALL // TM_TOK, N_EXPERTS)
    seg = (cnt + SEG_ALIGN - 1) // SEG_ALIGN * SEG_ALIGN
    loc = jnp.cumsum(seg, axis=1) - seg
    total = jnp.sum(seg, axis=1)
    rows_e = jnp.sum(seg, axis=0)
    region = (rows_e + TM_EXP - 1) // TM_EXP * TM_EXP
    region_end = jnp.cumsum(region)
    base = region_end - region
    dst = base[None, :] + jnp.cumsum(seg, axis=0) - seg
    n_used = jnp.maximum(region_end[-1] // TM_EXP, 1).astype(i32)
    blk = jnp.minimum(jnp.arange(N_SLOT_BLOCKS, dtype=i32), n_used - 1)
    block_e = jnp.sum((region_end[None, :] <= (blk * TM_EXP)[:, None]).astype(i32), axis=1)
    block_e = jnp.minimum(block_e, N_EXPERTS - 1).astype(i32)
    first = jnp.concatenate([jnp.ones((1,), i32), (block_e[1:] != block_e[:-1]).astype(i32)])
    parity = (jnp.cumsum(first) - 1) % 2
    e_ids = jnp.arange(N_EXPERTS, dtype=i32)
    later_used = (e_ids[None, :] > e_ids[:, None]) & (region[None, :] > 0)
    next_used = jnp.min(jnp.where(later_used, e_ids[None, :], N_EXPERTS), axis=1)
    next_e = next_used[block_e]
    next_e = jnp.where(next_e < N_EXPERTS, next_e, -1)
    flat = lambda a: a.reshape(-1).astype(i32)
    return dict(seg=flat(seg // SEG_ALIGN), loc=flat(loc), dst=flat(dst), total=flat(total // SEG_ALIGN),
                tail=flat((region - rows_e) // SEG_ALIGN), tail_dst=flat(base + rows_e),
                block_e=block_e, first=first, parity=flat(parity), next_e=flat(next_e),
                n_used=n_used.reshape(1))


def kernel(x_prompt, x_sample, cache_na_k, cache_na_v, cache_gqa_k, cache_gqa_v, state_rwkv, c, c_ctx,
           w_mod, b_mod, norm1_g, norm2_g, w_in, rw_shift, rw_w0, rw_w_up, rw_a0, rw_a_up, rw_g_up,
           rw_k_k, rw_k_a, rw_r_k, rw_ln_g, rw_ln_b, na_rpb, q_norm, k_norm, w_out, router_w, router_b,
           moe_w_gu, moe_b_gu, moe_w_down, moe_b_down, final_norm_g):
    x = jnp.concatenate([x_prompt.reshape(T_CTX, D_MODEL), x_sample.reshape(T_LAT, D_MODEL)], axis=0)
    cvecs = jnp.concatenate([c_ctx[None, :], c, jnp.zeros((SUBLANES - N_GROUPS, D_MODEL), F32)], axis=0)
    mods = _mods_call(cvecs, w_mod, b_mod)
    mods = mods[:, :N_GROUPS].reshape(DEPTH, N_GROUPS, N_MOD, D_MODEL)
    rope = _rope_tables()
    w_in_bf = w_in.astype(BF16)
    w_out_bf = w_out.astype(BF16)
    rw_g_up_bf = rw_g_up.astype(BF16)
    final_g = final_norm_g.reshape(1, D_MODEL)

    na_tb = _na_bias_tables(na_rpb)
    caches = None
    states = None
    for l in range(DEPTH):
        lp = {
            "rw_shift": rw_shift[l], "rw_k_k": rw_k_k[l].reshape(1, RW_WIDTH),
            "rw_k_a": rw_k_a[l].reshape(1, RW_WIDTH), "rw_r_k": rw_r_k[l].reshape(1, RW_WIDTH),
            "rw_w0": rw_w0[l], "rw_w_up": rw_w_up[l], "rw_a0": rw_a0[l], "rw_a_up": rw_a_up[l],
            "rw_g_up": rw_g_up_bf[l], "rw_ln_g": rw_ln_g[l].reshape(1, RW_WIDTH),
            "rw_ln_b": rw_ln_b[l].reshape(1, RW_WIDTH), "w_out": w_out_bf[l],
            "norm2_g": norm2_g[l].reshape(1, D_MODEL), "router_w": router_w[l],
            "router_b": router_b[l].reshape(1, N_EXPERTS),
        }
        qk_g = jnp.concatenate([jnp.tile(q_norm[l], GQA_Q_HEADS), jnp.tile(k_norm[l], GQA_KV_HEADS)])
        zrw, zna, zgq, zgkv = _in_proj_call(x, mods[l], norm1_g[l].reshape(1, D_MODEL), w_in_bf[l],
                                            qk_g.reshape(1, GQ_WIDTH + GKV_WIDTH), rope)

        s0_lat = state_rwkv[:, l]
        rw_ctx = _rwkv_call(zrw, s0_lat, lp, (), n_seq=BATCH, seq_len=SEQ, row_base=0,
                            has_init=False, emit_state=True, state_prev=states, layer=l)
        yf, yb, bonus, gate = _rwkv_call(zrw, s0_lat, lp, rw_ctx[:4], n_seq=DEC_BATCH, seq_len=DEC_SEQ,
                                         row_base=T_CTX, has_init=True, emit_state=False)
        states = rw_ctx[4]

        att, *caches = _ctx_attn_call(zna, zgq, zgkv, caches, l)
        att = _lat_na_call(zna, cache_na_k[:, l].reshape(DEC_BATCH, PAST_LEN, NA_WIDTH),
                           cache_na_v[:, l].reshape(DEC_BATCH, PAST_LEN, NA_WIDTH), na_tb[l], att)
        att = _lat_gqa_call(zgq, zgkv, cache_gqa_k[:, l].reshape(DEC_BATCH, PAST_LEN, GKV_WIDTH),
                            cache_gqa_v[:, l].reshape(DEC_BATCH, PAST_LEN, GKV_WIDTH), att)

        x1, h2, slot, gates, counts = _out_proj_call(x, yf, yb, bonus, gate, att, mods[l], lp)
        tabs = _routing_tables(counts)
        xs = _dispatch_call(tabs, h2, slot)
        ys = _expert_call(tabs, xs, moe_w_gu, moe_b_gu, moe_w_down, moe_b_down, l)
        x = _combine_call(tabs, ys, x1, slot, gates, mods[l], final_g, l == DEPTH - 1)

    y_prompt = x[0].reshape(BATCH, SEQ, D_MODEL)
    y_sample = x[1].reshape(DEC_BATCH, DEC_SEQ, D_MODEL)
    heads = lambda t, n: t.reshape(BATCH, DEPTH, SEQ, n, HEAD_DIM)
    return (y_prompt, y_sample, heads(caches[0], NA_HEADS), heads(caches[1], NA_HEADS),
            heads(caches[2], GQA_KV_HEADS), heads(caches[3], GQA_KV_HEADS), states)
```

```python
import functools

import numpy as np
import jax
import jax.numpy as jnp
from jax import lax
from jax.experimental import pallas as pl
from jax.experimental.pallas import tpu as pltpu

F32 = jnp.float32
BF16 = jnp.bfloat16

D_MODEL = 1024
BATCH = 32
SEQ = 256
DEPTH = 4
DEC_BATCH = 2
DEC_SEQ = 2048
PAST_LEN = 512
GRID_W = 64
GRID_H = DEC_SEQ // GRID_W
HEAD_DIM = 64
RW_HEADS = 4
RW_WIDTH = RW_HEADS * HEAD_DIM
DECAY_LORA = 64
ICLR_LORA = 64
GATE_LORA = 128
NA_HEADS = 4
NA_WIDTH = NA_HEADS * HEAD_DIM
NA_WIN_ROWS = 8
NA_WIN_COLS = 16
GQA_Q_HEADS = 8
GQA_KV_HEADS = 2
GQA_GROUP = GQA_Q_HEADS // GQA_KV_HEADS
GQ_WIDTH = GQA_Q_HEADS * HEAD_DIM
GKV_WIDTH = GQA_KV_HEADS * HEAD_DIM
RW_PROJ = 3 * RW_WIDTH + DECAY_LORA + ICLR_LORA + GATE_LORA
NA_PROJ = 3 * NA_WIDTH
GQA_PROJ = GQ_WIDTH + 2 * GKV_WIDTH
IN_PROJ = RW_PROJ + NA_PROJ + GQA_PROJ
ATT_WIDTH = GQ_WIDTH + NA_WIDTH
N_EXPERTS = 32
TOP_K = 4
D_FF = D_MODEL
SWIGLU_LIMIT = 7.0
SWIGLU_ALPHA = 1.702
ROPE_THETA = 10000.0
NORM_EPS = 1e-6
LNX_EPS = 64e-5
N_MOD = 6
ATT_SCALE = HEAD_DIM ** -0.5
NEG_BIG = -1e30

T_CTX = BATCH * SEQ
T_LAT = DEC_BATCH * DEC_SEQ
T_ALL = T_CTX + T_LAT
N_GROUPS = 1 + DEC_BATCH

LANES = 128
SUBLANES = 8
TM_TOK = 512
CHUNK = 64
RW_SEQ_GROUP = 4
TQ_GQA = 512
CTX_SEQ_GROUP = 1
TM_EXP = 256
WEIGHT_DMA_PARTS = 4
FF_SLAB = 1024
SEG_ALIGN = 16
SEG_MAX_BIT = (TM_TOK // SEG_ALIGN).bit_length() - 1
TAIL_MAX_BIT = (TM_EXP // SEG_ALIGN - 1).bit_length() - 1
TOTAL_MAX_BIT = (-(-(TOP_K * TM_TOK + N_EXPERTS * (SEG_ALIGN - 1)) // SEG_ALIGN)).bit_length() - 1
LOCAL_ROWS = -(-(TOP_K * TM_TOK + N_EXPERTS * (SEG_ALIGN - 1)) // TM_TOK) * TM_TOK
LOCAL_CHUNKS = tuple((r, TM_TOK) for r in range(0, LOCAL_ROWS - TM_TOK, TM_TOK)) + (
    (LOCAL_ROWS - TM_TOK, TM_TOK // 2), (LOCAL_ROWS - TM_TOK // 2, TM_TOK // 2))
N_SLOT_BLOCKS = (-(-(T_ALL * TOP_K + (T_ALL // TM_TOK) * N_EXPERTS * (SEG_ALIGN - 1)) // TM_EXP)
                 + N_EXPERTS)
N_SLOTS = N_SLOT_BLOCKS * TM_EXP
VMEM_LIMIT = 56 * 1024 * 1024

NT_DIMS = (((1,), (1,)), ((), ()))
TN_DIMS = (((0,), (0,)), ((), ()))


def _bdot(a, b, dims=None):
    a = a.astype(BF16)
    b = b.astype(BF16)
    if dims is None:
        return jnp.dot(a, b, preferred_element_type=F32)
    return lax.dot_general(a, b, dims, preferred_element_type=F32)


def _split(a):
    hi = a.astype(BF16)
    lo = (a - hi.astype(F32)).astype(BF16)
    return hi, lo


def _dot3(a, b, dims=None):
    ah, al = _split(a)
    bh, bl = _split(b)
    return _bdot(ah, bh, dims) + _bdot(ah, bl, dims) + _bdot(al, bh, dims)


def _dot_exact_lhs(a_exact, b):
    h1 = b.astype(BF16)
    r1 = b - h1.astype(F32)
    h2 = r1.astype(BF16)
    h3 = (r1 - h2.astype(F32)).astype(BF16)
    return _bdot(a_exact, h1) + _bdot(a_exact, h2) + _bdot(a_exact, h3)


def _head_ones(n):
    r = lax.broadcasted_iota(jnp.int32, (n, n), 0) // HEAD_DIM
    c = lax.broadcasted_iota(jnp.int32, (n, n), 1) // HEAD_DIM
    return (r == c).astype(BF16)


def _head_sum(x, ones_bd):
    hi, lo = _split(x)
    return (jnp.dot(hi, ones_bd, preferred_element_type=F32)
            + jnp.dot(lo, ones_bd, preferred_element_type=F32))


def _sigmoid(x):
    return 1.0 / (1.0 + jnp.exp(-x))


def _cparams(sem):
    return pltpu.CompilerParams(dimension_semantics=sem, vmem_limit_bytes=VMEM_LIMIT)


def _group_of_block(i, rows_per_block):
    first_lat = T_CTX // rows_per_block
    per_sample = DEC_SEQ // rows_per_block
    return jnp.where(i < first_lat, 0, 1 + (i - first_lat) // per_sample)


def _mods_kernel(c_ref, w_ref, b_ref, o_ref):
    c = c_ref[...]
    s = c * _sigmoid(c)
    o_ref[0] = _dot3(s, w_ref[0]) + b_ref[0]


def _mods_call(cvecs, w_mod, b_mod):
    tn = 1536
    n_rows = cvecs.shape[0]
    return pl.pallas_call(
        _mods_kernel,
        out_shape=jax.ShapeDtypeStruct((DEPTH, n_rows, N_MOD * D_MODEL), F32),
        grid=(DEPTH, N_MOD * D_MODEL // tn),
        in_specs=[
            pl.BlockSpec((n_rows, D_MODEL), lambda l, j: (0, 0)),
            pl.BlockSpec((1, D_MODEL, tn), lambda l, j: (l, 0, j)),
            pl.BlockSpec((1, 1, tn), lambda l, j: (l, 0, j)),
        ],
        out_specs=pl.BlockSpec((1, n_rows, tn), lambda l, j: (l, 0, j)),
        compiler_params=_cparams(("arbitrary", "arbitrary")),
        name="adaln_mods",
    )(cvecs, w_mod, b_mod.reshape(DEPTH, 1, N_MOD * D_MODEL))


def _rms(x, g):
    ms = jnp.mean(x * x, axis=-1, keepdims=True)
    return x * lax.rsqrt(ms + NORM_EPS) * g


def _in_proj_kernel(x_ref, mod_ref, g_ref, w_ref, qkg_ref, rc_ref, rs1_ref, rs2_ref,
                    zrw_ref, zna_ref, zgq_ref, zgkv_ref):
    x = x_ref[...]
    shift1 = mod_ref[0, 0:1, :]
    scale1 = mod_ref[0, 1:2, :]
    h = _rms(x, g_ref[...]) * (1.0 + scale1) + shift1
    z = jnp.dot(h.astype(BF16), w_ref[...], preferred_element_type=F32)
    zrw_ref[...] = z[:, :RW_PROJ]
    zna_ref[...] = z[:, RW_PROJ:RW_PROJ + NA_PROJ]
    qk_w = GQ_WIDTH + GKV_WIDTH
    qk = z[:, RW_PROJ + NA_PROJ:RW_PROJ + NA_PROJ + qk_w]
    ones_bd = _head_ones(LANES)
    sq = qk * qk
    ssq = jnp.concatenate(
        [_head_sum(sq[:, j * LANES:(j + 1) * LANES], ones_bd) for j in range(qk_w // LANES)], axis=1)
    qkn = qk * lax.rsqrt(ssq * (1.0 / HEAD_DIM) + NORM_EPS) * qkg_ref[...]
    reps = qk_w // LANES
    rc = jnp.concatenate([rc_ref[...]] * reps, axis=1)
    rs1 = jnp.concatenate([rs1_ref[...]] * reps, axis=1)
    rs2 = jnp.concatenate([rs2_ref[...]] * reps, axis=1)
    half = HEAD_DIM // 4
    qkr = qkn * rc + pltpu.roll(qkn, half, 1) * rs1 + pltpu.roll(qkn, qk_w - half, 1) * rs2
    zgq_ref[...] = qkr[:, :GQ_WIDTH]
    zgkv_ref[:, :GKV_WIDTH] = qkr[:, GQ_WIDTH:]
    zgkv_ref[:, GKV_WIDTH:] = z[:, RW_PROJ + NA_PROJ + qk_w:]


def _rope_tables():
    t = np.arange(DEC_SEQ)
    pos = np.stack([t // GRID_W, t % GRID_W], axis=1).astype(np.float32)
    axis_dim = HEAD_DIM // 2
    inv = ROPE_THETA ** (-np.arange(0, axis_dim, 2, dtype=np.float32) / axis_dim)
    d = np.arange(LANES) % HEAD_DIM
    part = d // axis_dim
    within = d % axis_dim
    freq = within % (axis_dim // 2)
    second = within // (axis_dim // 2)
    ang = jnp.asarray(pos)[:, part] * jnp.asarray(inv)[freq][None, :]
    cos = jnp.cos(ang)
    sin = jnp.sin(ang)
    s1 = jnp.where(second[None, :] == 1, sin, 0.0)
    s2 = jnp.where(second[None, :] == 0, -sin, 0.0)
    ident = jnp.ones((TM_TOK, LANES), F32)
    zero = jnp.zeros((TM_TOK, LANES), F32)
    return (jnp.concatenate([cos, ident], 0), jnp.concatenate([s1, zero], 0),
            jnp.concatenate([s2, zero], 0))


def _in_proj_call(x, mods_l, g1, w_in_bf, qk_g, rope):
    n_blk = T_ALL // TM_TOK
    lat_blk = DEC_SEQ // TM_TOK
    first_lat = T_CTX // TM_TOK

    def rope_idx(i):
        return (jnp.where(i < first_lat, lat_blk, (i - first_lat) % lat_blk), 0)

    row = lambda i: (i, 0)
    rope_spec = pl.BlockSpec((TM_TOK, LANES), rope_idx)
    return pl.pallas_call(
        _in_proj_kernel,
        out_shape=(jax.ShapeDtypeStruct((T_ALL, RW_PROJ), F32),
                   jax.ShapeDtypeStruct((T_ALL, NA_PROJ), F32),
                   jax.ShapeDtypeStruct((T_ALL, GQ_WIDTH), F32),
                   jax.ShapeDtypeStruct((T_ALL, 2 * GKV_WIDTH), F32)),
        grid=(n_blk,),
        in_specs=[
            pl.BlockSpec((TM_TOK, D_MODEL), row),
            pl.BlockSpec((1, N_MOD, D_MODEL), lambda i: (_group_of_block(i, TM_TOK), 0, 0)),
            pl.BlockSpec((1, D_MODEL), lambda i: (0, 0)),
            pl.BlockSpec((D_MODEL, IN_PROJ), lambda i: (0, 0)),
            pl.BlockSpec((1, GQ_WIDTH + GKV_WIDTH), lambda i: (0, 0)),
            rope_spec, rope_spec, rope_spec,
        ],
        out_specs=(pl.BlockSpec((TM_TOK, RW_PROJ), row), pl.BlockSpec((TM_TOK, NA_PROJ), row),
                   pl.BlockSpec((TM_TOK, GQ_WIDTH), row), pl.BlockSpec((TM_TOK, 2 * GKV_WIDTH), row)),
        compiler_params=_cparams(("arbitrary",)),
        name="in_proj",
    )(x, mods_l, g1, w_in_bf, qk_g, *rope)


def _softplus(x):
    return jnp.maximum(x, 0.0) + jnp.log(1.0 + jnp.exp(-jnp.abs(x)))


def _rw_pre(z, zprev, znext, shift_ref, kk_ref, ka_ref, rk_ref, w0_ref, wup_ref, a0_ref, aup_ref, d,
            ones_bd):
    rows = lax.broadcasted_iota(jnp.int32, z.shape, 0)
    zp = jnp.where(rows == 0, zprev, pltpu.roll(z, 1, 0))
    zn = jnp.where(rows == CHUNK - 1, znext, pltpu.roll(z, CHUNK - 1, 0))
    zs = zp * shift_ref[0:1, :] + z * shift_ref[1:2, :] + zn * shift_ref[2:3, :]
    r = zs[:, 0:RW_WIDTH]
    k = zs[:, RW_WIDTH:2 * RW_WIDTH]
    v = zs[:, 2 * RW_WIDTH:3 * RW_WIDTH]
    o = 3 * RW_WIDTH
    wd = zs[:, o:o + DECAY_LORA]
    ad = zs[:, o + DECAY_LORA:o + DECAY_LORA + ICLR_LORA]
    gd = zs[:, o + DECAY_LORA + ICLR_LORA:]
    kk = k * kk_ref[...]
    kk = kk / jnp.maximum(jnp.sqrt(_head_sum(kk * kk, ones_bd)), 1e-12)
    tw = jnp.tanh(wd)
    wl = w0_ref[d:d + 1, :] + _dot3(tw, wup_ref[d])
    lw = -jnp.exp(-_softplus(-wl) - 0.5)
    a_sig = _sigmoid(a0_ref[d:d + 1, :] + _dot3(ad, aup_ref[d]))
    k_d = k * (1.0 + (a_sig - 1.0) * ka_ref[...])
    bonus = _head_sum(r * k_d * rk_ref[...], ones_bd) * v
    return dict(r=r, k=k_d, v=v, a=-kk, b=kk * a_sig, lw=lw, bonus=bonus, gd=gd, ad=ad, k_raw=k)


def _chunk_masks(rev):
    t = lax.broadcasted_iota(jnp.int32, (CHUNK, CHUNK), 0)
    j = lax.broadcasted_iota(jnp.int32, (CHUNK, CHUNK), 1)
    return ((j >= t), (j > t)) if rev else ((j <= t), (j < t))


def _wkv_scale(p, incl, rev):
    lw = p["lw"]
    cs = _dot_exact_lhs(incl.astype(BF16), lw)
    tot = cs[0:1, :] if rev else cs[CHUNK - 1:CHUNK, :]
    e_inv = jnp.exp(-cs)
    e_rem = jnp.exp(tot - cs)
    bf = lambda x: x.astype(BF16)
    return dict(at=bf(p["a"] * jnp.exp(cs - lw)), rt=bf(p["r"] * jnp.exp(cs)),
                bt=bf(p["b"] * e_inv), kt=bf(p["k"] * e_inv),
                bh=bf(p["b"] * e_rem), kh=bf(p["k"] * e_rem), v=bf(p["v"]), gtot=jnp.exp(tot))


def _wkv_chunks(items):
    bf = lambda x: x.astype(BF16)
    c = CHUNK
    n_sq = int(np.log2(c))
    ar = [jnp.concatenate([it["at"], it["rt"]], 0) for it in items]
    m_b = [_bdot(a, it["bt"], NT_DIMS) for a, it in zip(ar, items)]
    m_k = [_bdot(a, it["kt"], NT_DIMS) for a, it in zip(ar, items)]
    s_bf = [bf(it["s"]) for it in items]
    xs = [bf(jnp.where(it["strict"], m[:c], 0.0)) for m, it in zip(m_b, items)]
    a_rb = [bf(jnp.where(it["incl"], m[c:], 0.0)) for m, it in zip(m_b, items)]
    a_ak = [bf(jnp.where(it["strict"], m[:c], 0.0)) for m, it in zip(m_k, items)]
    a_rk = [bf(jnp.where(it["incl"], m[c:], 0.0)) for m, it in zip(m_k, items)]
    ws = [_bdot(ak, it["v"]) + _bdot(it["at"], s, NT_DIMS) for ak, it, s in zip(a_ak, items, s_bf)]
    for step in range(n_sq):
        wb = [bf(w) for w in ws]
        ws = [w + _bdot(x, b) for w, x, b in zip(ws, xs, wb)]
        if step < n_sq - 1:
            xs = [bf(_bdot(x, x)) for x in xs]
    wb = [bf(w) for w in ws]
    ys = [_bdot(rb, w) + _bdot(rk, it["v"]) + _bdot(it["rt"], s, NT_DIMS)
          for rb, rk, w, it, s in zip(a_rb, a_rk, wb, items, s_bf)]
    s_new = [it["s"] * it["gtot"] + _bdot(w, it["bh"], TN_DIMS) + _bdot(it["v"], it["kh"], TN_DIMS)
             for w, it in zip(wb, items)]
    return ys, s_new


def _rwkv_kernel(zf_ref, zfp_ref, zfn_ref, zb_ref, zbp_ref, zbn_ref, s0_ref,
                 shift_ref, kk_ref, ka_ref, rk_ref, w0_ref, wup_ref, a0_ref, aup_ref, gup_ref,
                 *refs, n_chunks, n_group, has_init, emit_state, n_alias):
    refs = refs[n_alias:]
    if emit_state:
        yf_ref, yb_ref, bonus_ref, gate_ref, st_ref, h_ref = refs
    else:
        yf_ref, yb_ref, bonus_ref, gate_ref, h_ref = refs
        st_ref = None
    i = pl.program_id(1)
    ones_bd = _head_ones(RW_WIDTH)

    @pl.when(i == 0)
    def _():
        if has_init:
            h_ref[...] = s0_ref[...]
        else:
            h_ref[...] = jnp.zeros(h_ref.shape, F32)

    params = (shift_ref, kk_ref, ka_ref, rk_ref, w0_ref, wup_ref, a0_ref, aup_ref)
    first = i == 0
    last = i == n_chunks - 1
    zero_row = jnp.zeros((1, RW_PROJ), F32)
    masks = (_chunk_masks(False), _chunk_masks(True))
    items = []
    for g in range(n_group):
        pf = _rw_pre(zf_ref[g], jnp.where(first, zero_row, zfp_ref[g, 0, SUBLANES - 1:SUBLANES, :]),
                     jnp.where(last, zero_row, zfn_ref[g, 0, 0:1, :]), *params, 0, ones_bd)
        pb = _rw_pre(zb_ref[g], jnp.where(last, zero_row, zbp_ref[g, 0, SUBLANES - 1:SUBLANES, :]),
                     jnp.where(first, zero_row, zbn_ref[g, 0, 0:1, :]), *params, 1, ones_bd)
        a_sig_b = _sigmoid(a0_ref[1:2, :] + _dot3(pf["ad"], aup_ref[1]))
        k_b = pf["k_raw"] * (1.0 + (a_sig_b - 1.0) * ka_ref[...])
        bonus_ref[g] = pf["bonus"] + _head_sum(pf["r"] * k_b * rk_ref[...], ones_bd) * pf["v"]
        gate_ref[g] = _bdot(_sigmoid(pf["gd"]), gup_ref[...])
        for d, p in ((0, pf), (1, pb)):
            incl, strict = masks[d]
            sc = _wkv_scale(p, incl, d == 1)
            for h in range(RW_HEADS):
                sl = slice(h * HEAD_DIM, (h + 1) * HEAD_DIM)
                it = {k: v[:, sl] for k, v in sc.items()}
                it.update(s=h_ref[g, d, h], incl=incl, strict=strict, where=(g, d, h))
                items.append(it)

    ys, s_new = _wkv_chunks(items)
    for it, y, s in zip(items, ys, s_new):
        g, d, h = it["where"]
        y_ref = yb_ref if d else yf_ref
        y_ref[g, :, h * HEAD_DIM:(h + 1) * HEAD_DIM] = y
        h_ref[g, d, h] = s

    if emit_state:
        @pl.when(last)
        def _():
            for it, s in zip(items, s_new):
                g, d, h = it["where"]
                st_ref[g, 0, d, h] = s


def _rwkv_call(zrw, s0, lp, prev_outs, *, n_seq, seq_len, row_base, has_init, emit_state,
               state_prev=None, layer=0):
    g = min(RW_SEQ_GROUP, n_seq)
    n_chunks = seq_len // CHUNK
    n_rows8 = seq_len // SUBLANES
    per8 = CHUNK // SUBLANES
    n_view = T_ALL // seq_len
    base_g = row_base // seq_len // g
    z3 = zrw.reshape(n_view, seq_len, RW_PROJ)
    z4 = zrw.reshape(n_view, n_rows8, SUBLANES, RW_PROJ)

    fwd = lambda i: i
    bwd = lambda i: n_chunks - 1 - i
    main = lambda c: pl.BlockSpec((g, CHUNK, RW_PROJ), lambda b, i: (base_g + b, c(i), 0))
    prev8 = lambda c: pl.BlockSpec((g, 1, SUBLANES, RW_PROJ),
                                   lambda b, i: (base_g + b, jnp.maximum(c(i) * per8 - 1, 0), 0, 0))
    next8 = lambda c: pl.BlockSpec((g, 1, SUBLANES, RW_PROJ),
                                   lambda b, i: (base_g + b, jnp.minimum((c(i) + 1) * per8, n_rows8 - 1), 0, 0))
    full = lambda shape: pl.BlockSpec(shape, lambda b, i: (0,) * len(shape))
    out_f = pl.BlockSpec((g, CHUNK, RW_WIDTH), lambda b, i: (base_g + b, i, 0))
    out_b = pl.BlockSpec((g, CHUNK, RW_WIDTH), lambda b, i: (base_g + b, bwd(i), 0))
    tok = jax.ShapeDtypeStruct((n_view, seq_len, RW_WIDTH), F32)
    out_shape = [tok, tok, tok, tok]
    out_specs = [out_f, out_b, out_f, out_f]
    state_block = (g, 2, RW_HEADS, HEAD_DIM, HEAD_DIM)
    prev_views = [p.reshape(n_view, seq_len, RW_WIDTH) for p in prev_outs]
    n_in = 16
    aliases = {n_in + j: j for j in range(len(prev_views))}
    if emit_state:
        out_shape.append(jax.ShapeDtypeStruct((n_seq, DEPTH, 2, RW_HEADS, HEAD_DIM, HEAD_DIM), F32))
        out_specs.append(pl.BlockSpec((g, 1, 2, RW_HEADS, HEAD_DIM, HEAD_DIM),
                                      lambda b, i: (b, layer, 0, 0, 0, 0)))
        if state_prev is not None:
            aliases[n_in + len(prev_views)] = 4
            prev_views.append(state_prev)
    n_alias = len(prev_views)
    state_spec = pl.BlockSpec(state_block, lambda b, i: (b if has_init else 0, 0, 0, 0, 0))
    kern = functools.partial(_rwkv_kernel, n_chunks=n_chunks, n_group=g, has_init=has_init,
                             emit_state=emit_state, n_alias=n_alias)
    outs = pl.pallas_call(
        kern,
        out_shape=tuple(out_shape),
        grid=(n_seq // g, n_chunks),
        in_specs=[main(fwd), prev8(fwd), next8(fwd), main(bwd), prev8(bwd), next8(bwd), state_spec,
                  full((3, RW_PROJ)), full((1, RW_WIDTH)), full((1, RW_WIDTH)), full((1, RW_WIDTH)),
                  full((2, RW_WIDTH)), full((2, DECAY_LORA, RW_WIDTH)),
                  full((2, RW_WIDTH)), full((2, ICLR_LORA, RW_WIDTH)), full((GATE_LORA, RW_WIDTH))]
                 + [pl.BlockSpec(memory_space=pl.ANY)] * n_alias,
        out_specs=tuple(out_specs),
        scratch_shapes=[pltpu.VMEM(state_block, F32)],
        input_output_aliases=aliases,
        compiler_params=_cparams(("arbitrary", "arbitrary")),
        name="rwkv_scan_init" if has_init else "rwkv_scan_zero",
    )(z3, z4, z4, z3, z4, z4, s0, lp["rw_shift"], lp["rw_k_k"], lp["rw_k_a"], lp["rw_r_k"],
      lp["rw_w0"], lp["rw_w_up"], lp["rw_a0"], lp["rw_a_up"], lp["rw_g_up"], *prev_views)
    return tuple(o.reshape(T_ALL, RW_WIDTH) for o in outs[:4]) + tuple(outs[4:])


def _softmax_pv(scores, values):
    m = scores[0].max(axis=-1, keepdims=True)
    for s in scores[1:]:
        m = jnp.maximum(m, s.max(axis=-1, keepdims=True))
    es = [jnp.exp(s - m) for s in scores]
    l = es[0].sum(axis=-1, keepdims=True)
    for e in es[1:]:
        l = l + e.sum(axis=-1, keepdims=True)
    o = _bdot(es[0], values[0])
    for e, v in zip(es[1:], values[1:]):
        o = o + _bdot(e, v)
    return o * (1.0 / l)


def _head(x, h):
    return x[:, h * HEAD_DIM:(h + 1) * HEAD_DIM]


def _ctx_attn_kernel(zna_ref, zgq_ref, zgkv_ref, *refs):
    att_ref, nk_ref, nv_ref, gk_ref, gv_ref = refs[-5:]
    for b in range(CTX_SEQ_GROUP):
        rows = slice(b * SEQ, (b + 1) * SEQ)
        zna = zna_ref[rows, :]
        q, k, v = zna[:, :NA_WIDTH], zna[:, NA_WIDTH:2 * NA_WIDTH], zna[:, 2 * NA_WIDTH:]
        nk_ref[b, 0] = k
        nv_ref[b, 0] = v
        gq = zgq_ref[rows, :]
        gkv = zgkv_ref[rows, :]
        gk, gv = gkv[:, :GKV_WIDTH], gkv[:, GKV_WIDTH:]
        gk_ref[b, 0] = gk
        gv_ref[b, 0] = gv
        gq, gk, gv = (gq * ATT_SCALE).astype(BF16), gk.astype(BF16), gv.astype(BF16)
        q, k, v = (q * ATT_SCALE).astype(BF16), k.astype(BF16), v.astype(BF16)
        for h in range(GQA_Q_HEADS):
            kv = h // GQA_GROUP
            s = _bdot(_head(gq, h), _head(gk, kv), NT_DIMS)
            att_ref[rows, h * HEAD_DIM:(h + 1) * HEAD_DIM] = _softmax_pv([s], [_head(gv, kv)]).astype(BF16)
        for h in range(NA_HEADS):
            s = _bdot(_head(q, h), _head(k, h), NT_DIMS)
            o = _softmax_pv([s], [_head(v, h)])
            att_ref[rows, GQ_WIDTH + h * HEAD_DIM:GQ_WIDTH + (h + 1) * HEAD_DIM] = o.astype(BF16)


def _ctx_attn_call(zna, zgq, zgkv, caches, layer):
    g = CTX_SEQ_GROUP
    row = lambda b: (b, 0)
    bat = lambda b: (b, layer, 0, 0)
    cache = lambda w: jax.ShapeDtypeStruct((BATCH, DEPTH, SEQ, w), F32)
    n_alias = 0 if caches is None else len(caches)
    return pl.pallas_call(
        _ctx_attn_kernel,
        out_shape=(jax.ShapeDtypeStruct((T_ALL, ATT_WIDTH), BF16),
                   cache(NA_WIDTH), cache(NA_WIDTH), cache(GKV_WIDTH), cache(GKV_WIDTH)),
        grid=(BATCH // g,),
        in_specs=[pl.BlockSpec((g * SEQ, NA_PROJ), row), pl.BlockSpec((g * SEQ, GQ_WIDTH), row),
                  pl.BlockSpec((g * SEQ, 2 * GKV_WIDTH), row)] + [pl.BlockSpec(memory_space=pl.ANY)] * n_alias,
        out_specs=(pl.BlockSpec((g * SEQ, ATT_WIDTH), row),
                   pl.BlockSpec((g, 1, SEQ, NA_WIDTH), bat), pl.BlockSpec((g, 1, SEQ, NA_WIDTH), bat),
                   pl.BlockSpec((g, 1, SEQ, GKV_WIDTH), bat), pl.BlockSpec((g, 1, SEQ, GKV_WIDTH), bat)),
        input_output_aliases={3 + j: 1 + j for j in range(n_alias)},
        compiler_params=_cparams(("arbitrary",)),
        name="ctx_attention",
    )(zna, zgq, zgkv, *(caches or ()))


def _lat_na_kernel(zna_ref, ck_ref, cv_ref, tb_ref, att_in_ref, att_ref, kv_bf, ckv_bf):
    del att_in_ref
    i = pl.program_id(0)

    @pl.when(i == 0)
    def _():
        for b in range(DEC_BATCH):
            kv_bf[b] = zna_ref[b, :, NA_WIDTH:].astype(BF16)
            ckv_bf[b, :, :NA_WIDTH] = ck_ref[b].astype(BF16)
            ckv_bf[b, :, NA_WIDTH:] = cv_ref[b].astype(BF16)

    start = jnp.clip(i - NA_WIN_ROWS // 2, 0, GRID_H - NA_WIN_ROWS)
    n_loc = NA_WIN_ROWS * GRID_W
    w0 = pl.multiple_of(start * GRID_W, GRID_W)
    dr0 = start - i + NA_WIN_ROWS - 1
    qs, kvs, cs = [], [], []
    for b in range(DEC_BATCH):
        q = zna_ref[b, pl.ds(pl.multiple_of(i * GRID_W, GRID_W), GRID_W), 0:NA_WIDTH]
        qs.append((q * ATT_SCALE).astype(BF16))
        kvs.append(kv_bf[b, pl.ds(w0, n_loc), :])
        cs.append(ckv_bf[b])
    outs = [[] for _ in range(DEC_BATCH)]
    for h in range(NA_HEADS):
        bias = jnp.concatenate([tb_ref[h, dr0 + r] for r in range(NA_WIN_ROWS)], axis=1)
        for b in range(DEC_BATCH):
            qh = _head(qs[b], h)
            s_loc = _bdot(qh, _head(kvs[b], h), NT_DIMS) + bias
            s_ctx = _bdot(qh, _head(cs[b], h), NT_DIMS)
            outs[b].append(_softmax_pv([s_loc, s_ctx],
                                       [_head(kvs[b], NA_HEADS + h), _head(cs[b], NA_HEADS + h)]))
    for b in range(DEC_BATCH):
        att_ref[b] = jnp.concatenate(outs[b], axis=1).astype(BF16)


def _na_bias_tables(rpb):
    w = np.arange(GRID_W)[:, None]
    kc = np.arange(GRID_W)[None, :]
    cs = np.clip(w - NA_WIN_COLS // 2, 0, GRID_W - NA_WIN_COLS)
    valid = (kc >= cs) & (kc < cs + NA_WIN_COLS)
    off = np.clip(kc - w + NA_WIN_COLS - 1, 0, 2 * NA_WIN_COLS - 2)
    onehot = (off[:, :, None] == np.arange(2 * NA_WIN_COLS - 1)).astype(np.float32)
    tb = jnp.einsum("lhrd,wkd->lhrwk", rpb, jnp.asarray(onehot), precision=lax.Precision.HIGHEST)
    return jnp.where(jnp.asarray(valid), tb, NEG_BIG).astype(F32)


def _lat_na_call(zna, ck, cv, tb, att):
    n_view = T_ALL // DEC_SEQ
    lat_blk = T_CTX // DEC_SEQ // DEC_BATCH
    out = pl.pallas_call(
        _lat_na_kernel,
        out_shape=jax.ShapeDtypeStruct((n_view, DEC_SEQ, ATT_WIDTH), BF16),
        grid=(GRID_H,),
        in_specs=[pl.BlockSpec((DEC_BATCH, DEC_SEQ, NA_PROJ), lambda i: (lat_blk, 0, 0)),
                  pl.BlockSpec((DEC_BATCH, PAST_LEN, NA_WIDTH), lambda i: (0, 0, 0)),
                  pl.BlockSpec((DEC_BATCH, PAST_LEN, NA_WIDTH), lambda i: (0, 0, 0)),
                  pl.BlockSpec((NA_HEADS, 2 * NA_WIN_ROWS - 1, GRID_W, GRID_W), lambda i: (0, 0, 0, 0)),
                  pl.BlockSpec(memory_space=pl.ANY)],
        out_specs=pl.BlockSpec((DEC_BATCH, GRID_W, NA_WIDTH), lambda i: (lat_blk, i, GQ_WIDTH // NA_WIDTH)),
        scratch_shapes=[pltpu.VMEM((DEC_BATCH, DEC_SEQ, 2 * NA_WIDTH), BF16),
                        pltpu.VMEM((DEC_BATCH, PAST_LEN, 2 * NA_WIDTH), BF16)],
        input_output_aliases={4: 0},
        compiler_params=_cparams(("arbitrary",)),
        name="latent_neighbourhood_attention",
    )(zna.reshape(n_view, DEC_SEQ, NA_PROJ), ck, cv, tb, att.reshape(n_view, DEC_SEQ, ATT_WIDTH))
    return out.reshape(T_ALL, ATT_WIDTH)


def _lat_gqa_kernel(zgq_ref, zgkv_ref, ck_ref, cv_ref, att_in_ref, att_ref):
    del att_in_ref
    q = (zgq_ref[...] * ATT_SCALE).astype(BF16)
    kv = zgkv_ref[...].astype(BF16)
    kl, vl = kv[:, :GKV_WIDTH], kv[:, GKV_WIDTH:]
    kc = ck_ref[0].astype(BF16)
    vc = cv_ref[0].astype(BF16)
    for h in range(GQA_Q_HEADS):
        g = h // GQA_GROUP
        qh = _head(q, h)
        s_c = _bdot(qh, _head(kc, g), NT_DIMS)
        s_l = _bdot(qh, _head(kl, g), NT_DIMS)
        o = _softmax_pv([s_c, s_l], [_head(vc, g), _head(vl, g)])
        att_ref[:, h * HEAD_DIM:(h + 1) * HEAD_DIM] = o.astype(BF16)


def _lat_gqa_call(zgq, zgkv, ck, cv, att):
    n_q = DEC_SEQ // TQ_GQA
    first_q = T_CTX // TQ_GQA
    lat_blk = T_CTX // DEC_SEQ
    return pl.pallas_call(
        _lat_gqa_kernel,
        out_shape=jax.ShapeDtypeStruct((T_ALL, ATT_WIDTH), BF16),
        grid=(DEC_BATCH, n_q),
        in_specs=[pl.BlockSpec((TQ_GQA, GQ_WIDTH), lambda b, j: (first_q + b * n_q + j, 0)),
                  pl.BlockSpec((DEC_SEQ, 2 * GKV_WIDTH), lambda b, j: (lat_blk + b, 0)),
                  pl.BlockSpec((1, PAST_LEN, GKV_WIDTH), lambda b, j: (b, 0, 0)),
                  pl.BlockSpec((1, PAST_LEN, GKV_WIDTH), lambda b, j: (b, 0, 0)),
                  pl.BlockSpec(memory_space=pl.ANY)],
        out_specs=pl.BlockSpec((TQ_GQA, GQ_WIDTH), lambda b, j: (first_q + b * n_q + j, 0)),
        input_output_aliases={4: 0},
        compiler_params=_cparams(("arbitrary", "arbitrary")),
        name="latent_gqa_attention",
    )(zgq, zgkv, ck, cv, att)


def _out_proj_kernel(x_ref, yf_ref, yb_ref, bonus_ref, gate_ref, att_ref, mod_ref, lng_ref, lnb_ref,
                     wout_ref, g2_ref, rw_ref, rb_ref,
                     x1_ref, h2_ref, slot_ref, gates_ref, cnt_ref):
    ones_bd = _head_ones(RW_WIDTH)
    o = yf_ref[...] + yb_ref[...]
    mu = _head_sum(o, ones_bd) * (1.0 / HEAD_DIM)
    dlt = o - mu
    var = _head_sum(dlt * dlt, ones_bd) * (1.0 / HEAD_DIM)
    ln = dlt * lax.rsqrt(var + LNX_EPS) * lng_ref[...] + lnb_ref[...]
    rw = ((ln + bonus_ref[...]) * gate_ref[...]).astype(BF16)
    att = att_ref[...]
    mix = (jnp.dot(rw, wout_ref[0:RW_WIDTH, :], preferred_element_type=F32)
           + jnp.dot(att[:, :GQ_WIDTH], wout_ref[RW_WIDTH + NA_WIDTH:, :], preferred_element_type=F32)
           + jnp.dot(att[:, GQ_WIDTH:], wout_ref[RW_WIDTH:RW_WIDTH + NA_WIDTH, :],
                     preferred_element_type=F32))
    gate1 = mod_ref[0, 2:3, :]
    shift2 = mod_ref[0, 3:4, :]
    scale2 = mod_ref[0, 4:5, :]
    x1 = x_ref[...] + gate1 * mix
    x1_ref[...] = x1
    h2 = _rms(x1, g2_ref[...]) * (1.0 + scale2) + shift2
    h2_ref[...] = h2.astype(BF16)

    logits = _dot3(h2, rw_ref[...]) + rb_ref[...]
    tm = logits.shape[0]
    col = lax.broadcasted_iota(jnp.int32, (tm, N_EXPERTS), 1)
    lane4 = lax.broadcasted_iota(jnp.int32, (tm, TOP_K), 1)
    work = logits
    sels, vals = [], []
    for k in range(TOP_K):
        m = work.max(axis=-1, keepdims=True)
        idx = jnp.min(jnp.where(work == m, col, N_EXPERTS), axis=-1, keepdims=True)
        sel = col == idx
        sels.append(sel)
        vals.append(m)
        work = jnp.where(sel, -jnp.inf, work)
    es = [jnp.exp(v - vals[0]) for v in vals]
    inv = 1.0 / (es[0] + es[1] + es[2] + es[3])
    gates = jnp.zeros((tm, TOP_K), F32)
    for k in range(TOP_K):
        gates = jnp.where(lane4 == k, es[k] * inv, gates)
    assign = jnp.zeros((tm, N_EXPERTS), F32)
    for sel in sels:
        assign = assign + sel.astype(F32)
    r_i = lax.broadcasted_iota(jnp.int32, (tm, tm), 0)
    c_i = lax.broadcasted_iota(jnp.int32, (tm, tm), 1)
    before = jnp.dot((c_i < r_i).astype(BF16), assign.astype(BF16), preferred_element_type=F32)
    cnt = jnp.sum(assign, axis=0, keepdims=True)
    seg_units = jnp.floor((cnt + (SEG_ALIGN - 1)) * (1.0 / SEG_ALIGN))
    e_r = lax.broadcasted_iota(jnp.int32, (N_EXPERTS, N_EXPERTS), 0)
    e_c = lax.broadcasted_iota(jnp.int32, (N_EXPERTS, N_EXPERTS), 1)
    loc = SEG_ALIGN * jnp.dot(seg_units.astype(BF16), (e_r < e_c).astype(BF16),
                              preferred_element_type=F32)
    pos = before + loc
    slot = jnp.zeros((tm, TOP_K), F32)
    for k in range(TOP_K):
        sk = jnp.sum(jnp.where(sels[k], pos, 0.0), axis=-1, keepdims=True)
        slot = jnp.where(lane4 == k, sk, slot)
    slot_ref[...] = slot.astype(jnp.int32)
    gates_ref[...] = gates
    cnt_ref[0] = cnt.astype(jnp.int32)


def _out_proj_call(x, yf, yb, bonus, gate, att, mods_l, lp):
    n_blk = T_ALL // TM_TOK
    row = lambda i: (i, 0)
    full2 = lambda r, c: pl.BlockSpec((r, c), lambda i: (0, 0))
    tokw = lambda w: pl.BlockSpec((TM_TOK, w), row)
    return pl.pallas_call(
        _out_proj_kernel,
        out_shape=(jax.ShapeDtypeStruct((T_ALL, D_MODEL), F32),
                   jax.ShapeDtypeStruct((T_ALL, D_MODEL), BF16),
                   jax.ShapeDtypeStruct((T_ALL, TOP_K), jnp.int32),
                   jax.ShapeDtypeStruct((T_ALL, TOP_K), F32),
                   jax.ShapeDtypeStruct((n_blk, 1, N_EXPERTS), jnp.int32)),
        grid=(n_blk,),
        in_specs=[tokw(D_MODEL), tokw(RW_WIDTH), tokw(RW_WIDTH), tokw(RW_WIDTH), tokw(RW_WIDTH),
                  tokw(ATT_WIDTH),
                  pl.BlockSpec((1, N_MOD, D_MODEL), lambda i: (_group_of_block(i, TM_TOK), 0, 0)),
                  full2(1, RW_WIDTH), full2(1, RW_WIDTH), full2(D_MODEL, D_MODEL), full2(1, D_MODEL),
                  full2(D_MODEL, N_EXPERTS), full2(1, N_EXPERTS)],
        out_specs=(tokw(D_MODEL), tokw(D_MODEL), tokw(TOP_K), tokw(TOP_K),
                   pl.BlockSpec((1, 1, N_EXPERTS), lambda i: (i, 0, 0))),
        compiler_params=_cparams(("arbitrary",)),
        name="out_proj_router",
    )(x, yf, yb, bonus, gate, att, mods_l, lp["rw_ln_g"], lp["rw_ln_b"], lp["w_out"], lp["norm2_g"],
      lp["router_w"], lp["router_b"])


def _for_each_piece(n_units, max_bit, fn):
    for b in range(max_bit, -1, -1):
        @pl.when(((n_units >> b) & 1) == 1)
        def _(b=b):
            off = ((n_units >> (b + 1)) << (b + 1)) * SEG_ALIGN
            fn(pl.multiple_of(off, SEG_ALIGN), SEG_ALIGN << b)


def _segment_copies(local_ref, sorted_hbm, sem, blk, seg_ref, loc_ref, dst_ref, total_ref, to_sorted, wait):
    def copy(loc, dst, size):
        a = local_ref.at[pl.ds(pl.multiple_of(loc, SEG_ALIGN), size), :]
        b = sorted_hbm.at[pl.ds(pl.multiple_of(dst, SEG_ALIGN), size), :]
        return pltpu.make_async_copy(a, b, sem) if to_sorted else pltpu.make_async_copy(b, a, sem)

    if wait:
        _for_each_piece(total_ref[blk], TOTAL_MAX_BIT, lambda off, size: copy(0, 0, size).wait())
        return

    def body(e, carry):
        t = blk * N_EXPERTS + e
        loc = loc_ref[t]
        dst = dst_ref[t]
        _for_each_piece(seg_ref[t], SEG_MAX_BIT, lambda off, size: copy(loc + off, dst + off, size).start())
        return carry
    lax.fori_loop(0, N_EXPERTS, body, 0)


def _dispatch_kernel(seg_ref, loc_ref, dst_ref, total_ref, tail_ref, tail_dst_ref, h2_ref, slot_ref, xs_hbm,
                     xs_local, zero_buf, sem, zsem):
    j = pl.program_id(0)
    n = pl.num_programs(0)
    buf = j % 2

    @pl.when(j == 0)
    def _():
        zero_buf[...] = jnp.zeros(zero_buf.shape, BF16)

        def tails(wait):
            def body(e, carry):
                def piece(off, size):
                    cp = pltpu.make_async_copy(
                        zero_buf.at[pl.ds(0, size), :],
                        xs_hbm.at[pl.ds(pl.multiple_of(tail_dst_ref[e] + off, SEG_ALIGN), size), :], zsem)
                    if wait:
                        cp.wait()
                    else:
                        cp.start()
                _for_each_piece(tail_ref[e], TAIL_MAX_BIT, piece)
                return carry
            lax.fori_loop(0, N_EXPERTS, body, 0)
        tails(False)
        tails(True)

    lane = lax.broadcasted_iota(jnp.int32, (TM_TOK, LANES), 1)
    slots = slot_ref[...].astype(F32)
    wide = jnp.zeros((TM_TOK, LANES), F32)
    for k in range(TOP_K):
        wide = jnp.where(lane == k, slots[:, k:k + 1], wide)
    slot_rows = wide.T
    h2 = h2_ref[...]

    def sort_rows(r0, n_rows):
        row = lax.broadcasted_iota(jnp.int32, (n_rows, TM_TOK), 0).astype(F32) + float(r0)
        onehot = jnp.zeros((n_rows, TM_TOK), F32)
        for k in range(TOP_K):
            onehot = jnp.where(row == slot_rows[k:k + 1, :], 1.0, onehot)
        xs_local[buf, r0:r0 + n_rows, :] = jnp.dot(
            onehot.astype(BF16), h2, preferred_element_type=F32).astype(BF16)

    for r0, n_rows in LOCAL_CHUNKS[:-1]:
        sort_rows(r0, n_rows)

    @pl.when(total_ref[j] * SEG_ALIGN > LOCAL_CHUNKS[-1][0])
    def _():
        sort_rows(*LOCAL_CHUNKS[-1])

    tabs = (seg_ref, loc_ref, dst_ref, total_ref)

    @pl.when(j > 0)
    def _():
        _segment_copies(xs_local.at[1 - buf], xs_hbm, sem.at[1 - buf], j - 1, *tabs, True, True)

    _segment_copies(xs_local.at[buf], xs_hbm, sem.at[buf], j, *tabs, True, False)

    @pl.when(j == n - 1)
    def _():
        _segment_copies(xs_local.at[buf], xs_hbm, sem.at[buf], j, *tabs, True, True)


def _dispatch_call(tabs, h2, slot):
    n_blk = T_ALL // TM_TOK
    row = lambda j, *_: (j, 0)
    grid_spec = pltpu.PrefetchScalarGridSpec(
        num_scalar_prefetch=6,
        grid=(n_blk,),
        in_specs=[pl.BlockSpec((TM_TOK, D_MODEL), row), pl.BlockSpec((TM_TOK, TOP_K), row)],
        out_specs=pl.BlockSpec(memory_space=pl.ANY),
        scratch_shapes=[pltpu.VMEM((2, LOCAL_ROWS, D_MODEL), BF16), pltpu.VMEM((TM_EXP, D_MODEL), BF16),
                        pltpu.SemaphoreType.DMA((2,)), pltpu.SemaphoreType.DMA],
    )
    return pl.pallas_call(
        _dispatch_kernel,
        out_shape=jax.ShapeDtypeStruct((N_SLOTS, D_MODEL), BF16),
        grid_spec=grid_spec,
        compiler_params=_cparams(("arbitrary",)),
        name="moe_dispatch",
    )(tabs["seg"], tabs["loc"], tabs["dst"], tabs["total"], tabs["tail"], tabs["tail_dst"], h2, slot)


def _expert_kernel(be_ref, first_ref, par_ref, nexte_ref, meta_ref, x_ref, wgu_hbm, bgu_ref, wd_hbm, bd_ref,
                   y_ref, wgu_f, wd_f, wgu_bf, wd_bf, sem, *, layer):
    i = pl.program_id(0)
    n_used = meta_ref[0]

    def weight_copies(e, buf):
        rows = D_MODEL // WEIGHT_DMA_PARTS
        parts = []
        for p in range(WEIGHT_DMA_PARTS):
            sl = pl.ds(p * rows, rows)
            parts.append(pltpu.make_async_copy(wgu_hbm.at[layer, e, sl], wgu_f.at[buf, sl], sem.at[buf]))
            parts.append(pltpu.make_async_copy(wd_hbm.at[layer, e, sl], wd_f.at[buf, sl], sem.at[buf]))
        return parts

    @pl.when(i < n_used)
    def _():
        e = be_ref[i]
        buf = par_ref[i]
        is_first = first_ref[i] == 1

        @pl.when(is_first)
        def _():
            @pl.when(i == 0)
            def _():
                for cp in weight_copies(e, buf):
                    cp.start()
            for cp in weight_copies(e, buf):
                cp.wait()
            wgu_bf[...] = wgu_f[buf].astype(BF16)
            wd_bf[...] = wd_f[buf].astype(BF16)

        x = x_ref[...]
        y = bd_ref[0, 0]
        for c in range(D_FF // FF_SLAB):
            g_cols = slice(c * FF_SLAB, (c + 1) * FF_SLAB)
            l_cols = slice(D_FF + c * FF_SLAB, D_FF + (c + 1) * FF_SLAB)
            g = jnp.dot(x, wgu_bf[:, g_cols], preferred_element_type=F32) + bgu_ref[0, 0, :, g_cols]
            li = jnp.dot(x, wgu_bf[:, l_cols], preferred_element_type=F32) + bgu_ref[0, 0, :, l_cols]
            glu = jnp.minimum(g, SWIGLU_LIMIT)
            lin = jnp.clip(li, -SWIGLU_LIMIT, SWIGLU_LIMIT)
            act = glu * _sigmoid(SWIGLU_ALPHA * glu) * (lin + 1.0)
            y = y + jnp.dot(act.astype(BF16), wd_bf[g_cols, :], preferred_element_type=F32)
        y_ref[...] = y.astype(BF16)

        nxt = nexte_ref[i]

        @pl.when(is_first & (nxt >= 0))
        def _():
            for cp in weight_copies(nxt, 1 - buf):
                cp.start(priority=1)


def _expert_call(tabs, xs, w_gu, b_gu, w_down, b_down, layer):
    def bmap(i, be, first, par, nxt, meta_):
        return (layer, be[i], 0, 0)

    def rmap(i, be, first, par, nxt, meta_):
        return (jnp.minimum(i, meta_[0] - 1), 0)

    grid_spec = pltpu.PrefetchScalarGridSpec(
        num_scalar_prefetch=5,
        grid=(N_SLOT_BLOCKS,),
        in_specs=[pl.BlockSpec((TM_EXP, D_MODEL), rmap),
                  pl.BlockSpec(memory_space=pl.ANY),
                  pl.BlockSpec((1, 1, 1, 2 * D_FF), bmap),
                  pl.BlockSpec(memory_space=pl.ANY),
                  pl.BlockSpec((1, 1, 1, D_MODEL), bmap)],
        out_specs=pl.BlockSpec((TM_EXP, D_MODEL), rmap),
        scratch_shapes=[pltpu.VMEM((2, D_MODEL, 2 * D_FF), F32), pltpu.VMEM((2, D_FF, D_MODEL), F32),
                        pltpu.VMEM((D_MODEL, 2 * D_FF), BF16), pltpu.VMEM((D_FF, D_MODEL), BF16),
                        pltpu.SemaphoreType.DMA((2,))],
    )
    return pl.pallas_call(
        functools.partial(_expert_kernel, layer=layer),
        out_shape=jax.ShapeDtypeStruct((N_SLOTS, D_MODEL), BF16),
        grid_spec=grid_spec,
        compiler_params=_cparams(("arbitrary",)),
        name="moe_experts",
    )(tabs["block_e"], tabs["first"], tabs["parity"], tabs["next_e"], tabs["n_used"], xs, w_gu,
      b_gu.reshape(DEPTH, N_EXPERTS, 1, 2 * D_FF), w_down, b_down.reshape(DEPTH, N_EXPERTS, 1, D_MODEL))


def _combine_kernel(seg_ref, loc_ref, dst_ref, total_ref, ys_hbm, x1_ref, slot_ref, gates_ref, mod_ref,
                    fg_ref, *rest, final):
    *o_refs, ybuf, sem = rest
    j = pl.program_id(0)
    n = pl.num_programs(0)
    buf = j % 2

    tabs = (seg_ref, loc_ref, dst_ref, total_ref)

    @pl.when(j == 0)
    def _():
        _segment_copies(ybuf.at[0], ys_hbm, sem.at[0], 0, *tabs, False, False)

    @pl.when(j + 1 < n)
    def _():
        _segment_copies(ybuf.at[1 - buf], ys_hbm, sem.at[1 - buf], j + 1, *tabs, False, False)

    _segment_copies(ybuf.at[buf], ys_hbm, sem.at[buf], j, *tabs, False, True)

    slots = slot_ref[...]
    gates = gates_ref[...]
    total = total_ref[j] * SEG_ALIGN
    def gated_rows(r0, n_rows):
        col = lax.broadcasted_iota(jnp.int32, (TM_TOK, n_rows), 1) + r0
        q = jnp.zeros((TM_TOK, n_rows), F32)
        for k in range(TOP_K):
            q = jnp.where(col == slots[:, k:k + 1], gates[:, k:k + 1], q)
        rows = lax.broadcasted_iota(jnp.int32, (n_rows, D_MODEL), 0) + r0
        y = jnp.where(rows < total, ybuf[buf, r0:r0 + n_rows, :], jnp.zeros((), BF16))
        return jnp.dot(q.astype(BF16), y, preferred_element_type=F32)

    ff = gated_rows(0, TM_TOK)
    for c in range(1, LOCAL_ROWS // TM_TOK):
        ff = ff + gated_rows(c * TM_TOK, TM_TOK)
    x = x1_ref[...] + mod_ref[0, 5:6, :] * ff
    if final:
        x = _rms(x, fg_ref[...])
        ctx_ref, lat_ref = o_refs
        first_lat = T_CTX // TM_TOK

        @pl.when(j < first_lat)
        def _():
            ctx_ref[...] = x

        @pl.when(j >= first_lat)
        def _():
            lat_ref[...] = x
    else:
        o_refs[0][...] = x


def _combine_call(tabs, ys, x1, slot, gates, mods_l, final_g, final):
    n_blk = T_ALL // TM_TOK
    row = lambda j, *_: (j, 0)
    first_lat = T_CTX // TM_TOK
    if final:
        out_shape = (jax.ShapeDtypeStruct((T_CTX, D_MODEL), F32), jax.ShapeDtypeStruct((T_LAT, D_MODEL), F32))
        out_specs = (pl.BlockSpec((TM_TOK, D_MODEL), lambda j, *_: (jnp.minimum(j, first_lat - 1), 0)),
                     pl.BlockSpec((TM_TOK, D_MODEL), lambda j, *_: (jnp.maximum(j - first_lat, 0), 0)))
    else:
        out_shape = jax.ShapeDtypeStruct((T_ALL, D_MODEL), F32)
        out_specs = pl.BlockSpec((TM_TOK, D_MODEL), row)
    grid_spec = pltpu.PrefetchScalarGridSpec(
        num_scalar_prefetch=4,
        grid=(n_blk,),
        in_specs=[pl.BlockSpec(memory_space=pl.ANY),
                  pl.BlockSpec((TM_TOK, D_MODEL), row),
                  pl.BlockSpec((TM_TOK, TOP_K), row),
                  pl.BlockSpec((TM_TOK, TOP_K), row),
                  pl.BlockSpec((1, N_MOD, D_MODEL), lambda j, *_: (_group_of_block(j, TM_TOK), 0, 0)),
                  pl.BlockSpec((1, D_MODEL), lambda j, *_: (0, 0))],
        out_specs=out_specs,
        scratch_shapes=[pltpu.VMEM((2, LOCAL_ROWS, D_MODEL), BF16), pltpu.SemaphoreType.DMA((2,))],
    )
    return pl.pallas_call(
        functools.partial(_combine_kernel, final=final),
        out_shape=out_shape,
        grid_spec=grid_spec,
        compiler_params=_cparams(("arbitrary",)),
        name="moe_combine_final" if final else "moe_combine",
    )(tabs["seg"], tabs["loc"], tabs["dst"], tabs["total"], ys, x1, slot, gates, mods_l, final_g)


def _routing_tables(cnt):
    i32 = jnp.int32
    cnt = cnt.reshape(T_ALL // TM_TOK, N_EXPERTS)
    seg = (cnt + SEG_ALIGN - 1) // SEG_ALIGN * SEG_ALIGN
    loc = jnp.cumsum(seg, axis=1) - seg
    total = jnp.sum(seg, axis=1)
    rows_e = jnp.sum(seg, axis=0)
    region = (rows_e + TM_EXP - 1) // TM_EXP * TM_EXP
    region_end = jnp.cumsum(region)
    base = region_end - region
    dst = base[None, :] + jnp.cumsum(seg, axis=0) - seg
    n_used = jnp.maximum(region_end[-1] // TM_EXP, 1).astype(i32)
    blk = jnp.minimum(jnp.arange(N_SLOT_BLOCKS, dtype=i32), n_used - 1)
    block_e = jnp.sum((region_end[None, :] <= (blk * TM_EXP)[:, None]).astype(i32), axis=1)
    block_e = jnp.minimum(block_e, N_EXPERTS - 1).astype(i32)
    first = jnp.concatenate([jnp.ones((1,), i32), (block_e[1:] != block_e[:-1]).astype(i32)])
    parity = (jnp.cumsum(first) - 1) % 2
    e_ids = jnp.arange(N_EXPERTS, dtype=i32)
    later_used = (e_ids[None, :] > e_ids[:, None]) & (region[None, :] > 0)
    next_used = jnp.min(jnp.where(later_used, e_ids[None, :], N_EXPERTS), axis=1)
    next_e = next_used[block_e]
    next_e = jnp.where(next_e < N_EXPERTS, next_e, -1)
    flat = lambda a: a.reshape(-1).astype(i32)
    return dict(seg=flat(seg // SEG_ALIGN), loc=flat(loc), dst=flat(dst), total=flat(total // SEG_ALIGN),
                tail=flat((region - rows_e) // SEG_ALIGN), tail_dst=flat(base + rows_e),
                block_e=block_e, first=first, parity=flat(parity), next_e=flat(next_e),
                n_used=n_used.reshape(1))


def kernel(x_prompt, x_sample, cache_na_k, cache_na_v, cache_gqa_k, cache_gqa_v, state_rwkv, c, c_ctx,
           w_mod, b_mod, norm1_g, norm2_g, w_in, rw_shift, rw_w0, rw_w_up, rw_a0, rw_a_up, rw_g_up,
           rw_k_k, rw_k_a, rw_r_k, rw_ln_g, rw_ln_b, na_rpb, q_norm, k_norm, w_out, router_w, router_b,
           moe_w_gu, moe_b_gu, moe_w_down, moe_b_down, final_norm_g):
    x = jnp.concatenate([x_prompt.reshape(T_CTX, D_MODEL), x_sample.reshape(T_LAT, D_MODEL)], axis=0)
    cvecs = jnp.concatenate([c_ctx[None, :], c, jnp.zeros((SUBLANES - N_GROUPS, D_MODEL), F32)], axis=0)
    mods = _mods_call(cvecs, w_mod, b_mod)
    mods = mods[:, :N_GROUPS].reshape(DEPTH, N_GROUPS, N_MOD, D_MODEL)
    rope = _rope_tables()
    w_in_bf = w_in.astype(BF16)
    w_out_bf = w_out.astype(BF16)
    rw_g_up_bf = rw_g_up.astype(BF16)
    final_g = final_norm_g.reshape(1, D_MODEL)

    na_tb = _na_bias_tables(na_rpb)
    caches = None
    states = None
    for l in range(DEPTH):
        lp = {
            "rw_shift": rw_shift[l], "rw_k_k": rw_k_k[l].reshape(1, RW_WIDTH),
            "rw_k_a": rw_k_a[l].reshape(1, RW_WIDTH), "rw_r_k": rw_r_k[l].reshape(1, RW_WIDTH),
            "rw_w0": rw_w0[l], "rw_w_up": rw_w_up[l], "rw_a0": rw_a0[l], "rw_a_up": rw_a_up[l],
            "rw_g_up": rw_g_up_bf[l], "rw_ln_g": rw_ln_g[l].reshape(1, RW_WIDTH),
            "rw_ln_b": rw_ln_b[l].reshape(1, RW_WIDTH), "w_out": w_out_bf[l],
            "norm2_g": norm2_g[l].reshape(1, D_MODEL), "router_w": router_w[l],
            "router_b": router_b[l].reshape(1, N_EXPERTS),
        }
        qk_g = jnp.concatenate([jnp.tile(q_norm[l], GQA_Q_HEADS), jnp.tile(k_norm[l], GQA_KV_HEADS)])
        zrw, zna, zgq, zgkv = _in_proj_call(x, mods[l], norm1_g[l].reshape(1, D_MODEL), w_in_bf[l],
                                            qk_g.reshape(1, GQ_WIDTH + GKV_WIDTH), rope)

        s0_lat = state_rwkv[:, l]
        rw_ctx = _rwkv_call(zrw, s0_lat, lp, (), n_seq=BATCH, seq_len=SEQ, row_base=0,
                            has_init=False, emit_state=True, state_prev=states, layer=l)
        yf, yb, bonus, gate = _rwkv_call(zrw, s0_lat, lp, rw_ctx[:4], n_seq=DEC_BATCH, seq_len=DEC_SEQ,
                                         row_base=T_CTX, has_init=True, emit_state=False)
        states = rw_ctx[4]

        att, *caches = _ctx_attn_call(zna, zgq, zgkv, caches, l)
        att = _lat_na_call(zna, cache_na_k[:, l].reshape(DEC_BATCH, PAST_LEN, NA_WIDTH),
                           cache_na_v[:, l].reshape(DEC_BATCH, PAST_LEN, NA_WIDTH), na_tb[l], att)
        att = _lat_gqa_call(zgq, zgkv, cache_gqa_k[:, l].reshape(DEC_BATCH, PAST_LEN, GKV_WIDTH),
                            cache_gqa_v[:, l].reshape(DEC_BATCH, PAST_LEN, GKV_WIDTH), att)

        x1, h2, slot, gates, counts = _out_proj_call(x, yf, yb, bonus, gate, att, mods[l], lp)
        tabs = _routing_tables(counts)
        xs = _dispatch_call(tabs, h2, slot)
        ys = _expert_call(tabs, xs, moe_w_gu, moe_b_gu, moe_w_down, moe_b_down, l)
        x = _combine_call(tabs, ys, x1, slot, gates, mods[l], final_g, l == DEPTH - 1)

    y_prompt = x[0].reshape(BATCH, SEQ, D_MODEL)
    y_sample = x[1].reshape(DEC_BATCH, DEC_SEQ, D_MODEL)
    heads = lambda t, n: t.reshape(BATCH, DEPTH, SEQ, n, HEAD_DIM)
    return (y_prompt, y_sample, heads(caches[0], NA_HEADS), heads(caches[1], NA_HEADS),
            heads(caches[2], GQA_KV_HEADS), heads(caches[3], GQA_KV_HEADS), states)
```
